```python
import jax, jax.numpy as jnp
from jax import lax
import numpy as np

D_MODEL = 2048
BATCH = 8
SEQ = 2048
DEPTH = 1

HEAD_DIM = 128
D_MIX = D_MODEL
A_HEADS = (D_MIX // 2) // HEAD_DIM
A_KV_HEADS = 2
A_WINDOW = 128
A_BLOCK = 128
B_HEADS = (D_MIX // 2) // HEAD_DIM
DILATED_PATTERNS = ((128, 1), (512, 4), (2048, 16))
B_BLOCK = 64
ROT_DIM = HEAD_DIM // 4
ROPE_THETA = 500000.0
D_FF = 4 * D_MODEL
PLE_DIM = 256
ALPHA = (2.0 * DEPTH) ** 0.25
BETA = (8.0 * DEPTH) ** -0.25
LN_EPS = 1e-5
RMS_EPS = 1e-6
NEG_INF = -1e30

A_Q = A_HEADS * HEAD_DIM
A_KV = A_KV_HEADS * HEAD_DIM
B_QKV = B_HEADS * HEAD_DIM
D_IN = A_Q + 2 * A_KV + 3 * B_QKV

kernel_name = "hybrid_window_gqa_dilated_attn_deepnorm"


def layer_norm(x, g, b):
    xf = x.astype(jnp.float32)
    mu = jnp.mean(xf, axis=-1, keepdims=True)
    var = jnp.mean(jnp.square(xf - mu), axis=-1, keepdims=True)
    y = (xf - mu) * lax.rsqrt(var + LN_EPS) * g.astype(jnp.float32) + b.astype(jnp.float32)
    return y.astype(x.dtype)


def rms_norm(x, g):
    xf = x.astype(jnp.float32)
    y = xf * lax.rsqrt(jnp.mean(jnp.square(xf), axis=-1, keepdims=True) + RMS_EPS) * g.astype(jnp.float32)
    return y.astype(x.dtype)


def rope_partial(t, cos, sin):
    half = ROT_DIM // 2
    t1 = t[..., :half].astype(jnp.float32)
    t2 = t[..., half:ROT_DIM].astype(jnp.float32)
    rot = jnp.concatenate([t1 * cos - t2 * sin, t2 * cos + t1 * sin], axis=-1)
    return jnp.concatenate([rot.astype(t.dtype), t[..., ROT_DIM:]], axis=-1)


def banded_attention(q, k, v, halo, block, sink=None):
    bt, seq_len, hq, d = q.shape
    hkv = k.shape[2]
    grp = hq // hkv
    nb = -(-seq_len // block)
    lp = nb * block
    kw = block + 2 * halo
    qb = jnp.pad(q, ((0, 0), (0, lp - seq_len), (0, 0), (0, 0))).reshape(bt, nb, block, hkv, grp, d)
    pad_kv = ((0, 0), (halo, halo + lp - seq_len), (0, 0), (0, 0))
    key_idx = (jnp.arange(nb) * block)[:, None] + jnp.arange(kw)[None, :]
    kb = jnp.pad(k, pad_kv)[:, key_idx]
    vb = jnp.pad(v, pad_kv)[:, key_idx]
    s = jnp.einsum('bnqhgd,bnkhd->bnhgqk', qb, kb,
                   preferred_element_type=jnp.float32) * (d ** -0.5)
    qpos = (jnp.arange(nb) * block)[:, None] + jnp.arange(block)[None, :]
    kpos = key_idx - halo
    mask = ((jnp.abs(qpos[:, :, None] - kpos[:, None, :]) <= halo)
            & (kpos >= 0)[:, None, :] & (kpos < seq_len)[:, None, :])
    s = jnp.where(mask[None, :, None, None], s, NEG_INF)
    m = jnp.max(s, axis=-1)
    if sink is not None:
        sk = sink.astype(jnp.float32).reshape(hkv, grp)[None, None, :, :, None]
        m = jnp.maximum(m, sk)
    e = jnp.exp(s - m[..., None])
    denom = jnp.sum(e, axis=-1)
    if sink is not None:
        denom = denom + jnp.exp(sk - m)
    o = jnp.einsum('bnhgqk,bnkhd->bnqhgd', e, vb.astype(jnp.float32))
    o = o / jnp.transpose(denom, (0, 1, 4, 2, 3))[..., None]
    lse = jnp.transpose(m + jnp.log(denom), (0, 1, 4, 2, 3)).reshape(bt, lp, hq)[:, :seq_len]
    o = o.reshape(bt, lp, hq, d)[:, :seq_len].astype(q.dtype)
    return o, lse


def dilated_attention(q, k, v):
    b, s, h, d = q.shape
    outs, lses = [], []
    for window, rate in DILATED_PATTERNS:
        halo = window // (2 * rate)
        sub = s // rate

        def to_res(t):
            return jnp.transpose(t.reshape(b, sub, rate, h, d), (0, 2, 1, 3, 4)).reshape(b * rate, sub, h, d)

        o, lse = banded_attention(to_res(q), to_res(k), to_res(v), halo, B_BLOCK)
        outs.append(jnp.transpose(o.reshape(b, rate, sub, h, d), (0, 2, 1, 3, 4)).reshape(b, s, h, d))
        lses.append(jnp.transpose(lse.reshape(b, rate, sub, h), (0, 2, 1, 3)).reshape(b, s, h))
    w = jax.nn.softmax(jnp.stack(lses, axis=0), axis=0)
    out = jnp.sum(w[..., None] * jnp.stack(outs, axis=0).astype(jnp.float32), axis=0)
    return out.astype(q.dtype)


def setup_inputs(seed: int = 0) -> dict:
    key = jax.random.key(seed)
    ks = jax.random.split(key, 20)
    f32 = jnp.float32
    x = jax.random.normal(ks[0], (BATCH, SEQ, D_MODEL), f32)
    p = jax.random.normal(ks[1], (DEPTH, BATCH, SEQ, PLE_DIM), f32)
    positions = (jnp.arange(SEQ, dtype=jnp.int32)[None, :]
                 + jax.random.randint(ks[2], (BATCH, 1), 0, 512, dtype=jnp.int32))
    col_scale = jnp.concatenate([
        jnp.ones((A_Q + A_KV,), f32), jnp.full((A_KV,), BETA, f32),
        jnp.ones((2 * B_QKV,), f32), jnp.full((B_QKV,), BETA, f32)])
    w_in = jax.random.normal(ks[3], (DEPTH, D_MODEL, D_IN), f32) * (D_MODEL ** -0.5) * col_scale
    sink_a = 0.5 * jax.random.normal(ks[4], (DEPTH, A_HEADS), f32)
    gn_a = 1.0 + 0.02 * jax.random.normal(ks[5], (DEPTH, A_Q), f32)
    gn_b = 1.0 + 0.02 * jax.random.normal(ks[6], (DEPTH, B_QKV), f32)
    w_o = jax.random.normal(ks[7], (DEPTH, D_MIX, D_MODEL), f32) * (D_MIX ** -0.5) * BETA
    ln1_g = 1.0 + 0.02 * jax.random.normal(ks[8], (DEPTH, D_MODEL), f32)
    ln1_b = 0.02 * jax.random.normal(ks[9], (DEPTH, D_MODEL), f32)
    w1 = jax.random.normal(ks[10], (DEPTH, D_MODEL, D_FF), f32) * (D_MODEL ** -0.5) * BETA
    w2 = jax.random.normal(ks[11], (DEPTH, D_FF, D_MODEL), f32) * (D_FF ** -0.5) * BETA
    w_ple = jax.random.normal(ks[12], (DEPTH, PLE_DIM, D_MODEL), f32) * (PLE_DIM ** -0.5) * BETA
    w_ple_gate = jax.random.normal(ks[13], (DEPTH, D_MODEL, D_MODEL), f32) * (D_MODEL ** -0.5)
    ln2_g = 1.0 + 0.02 * jax.random.normal(ks[14], (DEPTH, D_MODEL), f32)
    ln2_b = 0.02 * jax.random.normal(ks[15], (DEPTH, D_MODEL), f32)
    return {"x": x, "p": p, "positions": positions, "w_in": w_in, "sink_a": sink_a,
            "gn_a": gn_a, "gn_b": gn_b, "w_o": w_o, "ln1_g": ln1_g, "ln1_b": ln1_b,
            "w1": w1, "w2": w2, "w_ple": w_ple, "w_ple_gate": w_ple_gate,
            "ln2_g": ln2_g, "ln2_b": ln2_b}


def reference(x, p, positions, w_in, sink_a, gn_a, gn_b, w_o, ln1_g, ln1_b,
              w1, w2, w_ple, w_ple_gate, ln2_g, ln2_b):
    b, s, _ = x.shape
    inv_freq = ROPE_THETA ** (-jnp.arange(0, ROT_DIM, 2, dtype=jnp.float32) / ROT_DIM)
    ang = positions.astype(jnp.float32)[..., None] * inv_freq
    cos = jnp.cos(ang)[:, :, None, :]
    sin = jnp.sin(ang)[:, :, None, :]
    split_at = [A_Q, A_Q + A_KV, A_Q + 2 * A_KV, A_Q + 2 * A_KV + B_QKV, A_Q + 2 * A_KV + 2 * B_QKV]
    h = x
    for i in range(DEPTH):
        proj = h @ w_in[i]
        qa, ka, va, qb, kb, vb = jnp.split(proj, split_at, axis=-1)
        qa = rope_partial(qa.reshape(b, s, A_HEADS, HEAD_DIM), cos, sin)
        ka = rope_partial(ka.reshape(b, s, A_KV_HEADS, HEAD_DIM), cos, sin)
        va = va.reshape(b, s, A_KV_HEADS, HEAD_DIM)
        qb = rope_partial(qb.reshape(b, s, B_HEADS, HEAD_DIM), cos, sin)
        kb = rope_partial(kb.reshape(b, s, B_HEADS, HEAD_DIM), cos, sin)
        vb = vb.reshape(b, s, B_HEADS, HEAD_DIM)
        oa, _ = banded_attention(qa, ka, va, A_WINDOW, A_BLOCK, sink_a[i])
        ob = dilated_attention(qb, kb, vb)
        ya = rms_norm(oa.reshape(b, s, A_Q), gn_a[i])
        yb = rms_norm(ob.reshape(b, s, B_QKV), gn_b[i])
        mix = jnp.concatenate([ya, yb], axis=-1) @ w_o[i]
        h = layer_norm(ALPHA * h + mix, ln1_g[i], ln1_b[i])
        ff = jnp.square(jax.nn.relu(h @ w1[i])) @ w2[i]
        ple = (p[i] @ w_ple[i]) * jax.nn.sigmoid(h @ w_ple_gate[i])
        h = layer_norm(ALPHA * h + ff + ple, ln2_g[i], ln2_b[i])
    return h
```

```python
import functools

import jax
import jax.numpy as jnp
from jax import lax
from jax.experimental import pallas as pl
from jax.experimental.pallas import tpu as pltpu

D_MODEL = 2048
HEAD_DIM = 128
A_HEADS = 8
A_KV_HEADS = 2
A_WINDOW = 128
B_HEADS = 8
DILATED_PATTERNS = ((128, 1), (512, 4), (2048, 16))
ROT_DIM = HEAD_DIM // 4
ROPE_THETA = 500000.0
D_FF = 4 * D_MODEL
PLE_DIM = 256
DEPTH = 1
ALPHA = (2.0 * DEPTH) ** 0.25
LN_EPS = 1e-5
RMS_EPS = 1e-6
NEG_INF = -1e30

A_Q = A_HEADS * HEAD_DIM
A_KV = A_KV_HEADS * HEAD_DIM
B_QKV = B_HEADS * HEAD_DIM
A_COLS = A_Q + 2 * A_KV
B_COLS = 3 * B_QKV
D_IN = A_COLS + B_COLS

LANES = 128
Q_BLOCK = 128
LSE_LANES_PER_HEAD = LANES // B_HEADS
VMEM_LIMIT = 56 * 1024 * 1024

PROJ_TM = 1024
PROJ_TN = 512
MIX_TM = 512
FFN_TM = 512
FFN_TF = 1024
ROPE_TM = 2048


def _rope_table_kernel(pos_ref, freq_ref, cos_ref, sin_ref):
    ang = pos_ref[...].astype(jnp.float32) * freq_ref[...]
    lane = lax.broadcasted_iota(jnp.int32, ang.shape, 1)
    half = ROT_DIM // 2
    cos_ref[...] = jnp.where(lane < ROT_DIM, jnp.cos(ang), 1.0)
    s = jnp.sin(ang)
    sin_ref[...] = jnp.where(lane < half, -s, jnp.where(lane < ROT_DIM, s, 0.0))


def _rope_tables(positions):
    m = positions.size
    half = ROT_DIM // 2
    inv_freq = ROPE_THETA ** (-jnp.arange(0, ROT_DIM, 2, dtype=jnp.float32) / ROT_DIM)
    freq = jnp.tile(inv_freq, LANES // half).reshape(1, LANES)
    pos = positions.reshape(m, 1)
    out = jax.ShapeDtypeStruct((m, LANES), jnp.float32)
    return pl.pallas_call(
        _rope_table_kernel,
        grid=(m // ROPE_TM,),
        in_specs=[pl.BlockSpec((ROPE_TM, 1), lambda i: (i, 0)),
                  pl.BlockSpec((1, LANES), lambda i: (0, 0))],
        out_specs=[pl.BlockSpec((ROPE_TM, LANES), lambda i: (i, 0))] * 2,
        out_shape=[out, out],
        name="rope_table",
    )(pos, freq)


def _rope_group(t, cos, sin, lane):
    half = ROT_DIM // 2
    swapped = jnp.where(lane < half,
                        pltpu.roll(t, LANES - half, 1),
                        pltpu.roll(t, half, 1))
    return t * cos + swapped * sin


def _in_proj_kernel(x_ref, w_ref, cos_ref, sin_ref, oa_ref, ob_ref, xb_ref):
    j = pl.program_id(1)

    @pl.when(j == 0)
    def _():
        xb_ref[...] = x_ref[...].astype(jnp.bfloat16)

    acc = jnp.dot(xb_ref[...], w_ref[...], preferred_element_type=jnp.float32)
    groups = PROJ_TN // LANES
    q_scale = HEAD_DIM ** -0.5

    def emit(o_ref, rope_groups, scale):
        cos = cos_ref[...]
        sin = sin_ref[...]
        lane = lax.broadcasted_iota(jnp.int32, cos.shape, 1)
        for g in range(groups):
            t = acc[:, g * LANES:(g + 1) * LANES]
            if g < rope_groups:
                t = _rope_group(t, cos, sin, lane)
            if scale:
                t = t * q_scale
            o_ref[:, g * LANES:(g + 1) * LANES] = t.astype(o_ref.dtype)

    n_a = A_COLS // PROJ_TN
    qa_tiles = A_Q // PROJ_TN
    qb_tiles = B_QKV // PROJ_TN

    @pl.when(j < qa_tiles)
    def _():
        emit(oa_ref, groups, True)

    @pl.when(j == qa_tiles)
    def _():
        emit(oa_ref, A_KV // LANES, False)

    @pl.when((j >= n_a) & (j < n_a + qb_tiles))
    def _():
        emit(ob_ref, groups, True)

    @pl.when((j >= n_a + qb_tiles) & (j < n_a + 2 * qb_tiles))
    def _():
        emit(ob_ref, groups, False)

    @pl.when(j >= n_a + 2 * qb_tiles)
    def _():
        emit(ob_ref, 0, False)


def _in_proj(x2, w_in_b, cos_t, sin_t):
    m = x2.shape[0]
    n_a = A_COLS // PROJ_TN
    assert A_Q % PROJ_TN == 0 and A_COLS == A_Q + PROJ_TN and B_QKV % PROJ_TN == 0
    return pl.pallas_call(
        _in_proj_kernel,
        grid=(m // PROJ_TM, D_IN // PROJ_TN),
        in_specs=[pl.BlockSpec((PROJ_TM, D_MODEL), lambda i, j: (i, 0)),
                  pl.BlockSpec((D_MODEL, PROJ_TN), lambda i, j: (0, j)),
                  pl.BlockSpec((PROJ_TM, LANES), lambda i, j: (i, 0)),
                  pl.BlockSpec((PROJ_TM, LANES), lambda i, j: (i, 0))],
        out_specs=[pl.BlockSpec((PROJ_TM, PROJ_TN), lambda i, j: (i, jnp.minimum(j, n_a - 1))),
                   pl.BlockSpec((PROJ_TM, PROJ_TN), lambda i, j: (i, jnp.maximum(j - n_a, 0)))],
        out_shape=[jax.ShapeDtypeStruct((m, A_COLS), jnp.bfloat16),
                   jax.ShapeDtypeStruct((m, B_COLS), jnp.bfloat16)],
        scratch_shapes=[pltpu.VMEM((PROJ_TM, D_MODEL), jnp.bfloat16)],
        compiler_params=pltpu.CompilerParams(
            dimension_semantics=("arbitrary", "arbitrary"), vmem_limit_bytes=VMEM_LIMIT),
        name="in_proj",
    )(x2, w_in_b, cos_t, sin_t)


def _band_attn_kernel(*refs, length, halo, n_q_heads, n_kv_heads, has_sink, has_lse):
    refs = list(refs)
    q_ref, k_ref, v_ref = refs[:3]
    rest = refs[3:]
    sink_ref = rest.pop(0) if has_sink else None
    o_ref = rest.pop(0)
    lse_ref = rest.pop(0) if has_lse else None
    grp = n_q_heads // n_kv_heads
    window = min(Q_BLOCK + 2 * halo, length)

    def q_block(j, carry):
        q0 = pl.multiple_of(j * Q_BLOCK, Q_BLOCK)
        w0 = pl.multiple_of(jnp.clip(q0 - halo, 0, length - window), 64)
        qpos = q0 + lax.broadcasted_iota(jnp.int32, (Q_BLOCK, window), 0)
        kpos = w0 + lax.broadcasted_iota(jnp.int32, (Q_BLOCK, window), 1)
        bias = jnp.where(jnp.abs(qpos - kpos) <= halo, 0.0, NEG_INF).astype(jnp.float32)
        for h in range(n_q_heads):
            kh = h // grp
            q = q_ref[pl.ds(q0, Q_BLOCK), h * HEAD_DIM:(h + 1) * HEAD_DIM]
            k = k_ref[pl.ds(w0, window), kh * HEAD_DIM:(kh + 1) * HEAD_DIM]
            v = v_ref[pl.ds(w0, window), kh * HEAD_DIM:(kh + 1) * HEAD_DIM]
            s = lax.dot_general(q, k, (((1,), (1,)), ((), ())),
                                preferred_element_type=jnp.float32) + bias
            m = jnp.max(s, axis=-1, keepdims=True)
            if has_sink:
                sk = sink_ref[h]
                m = jnp.maximum(m, sk)
            e = jnp.exp(s - m)
            denom = jnp.sum(e, axis=-1, keepdims=True)
            if has_sink:
                denom = denom + jnp.exp(sk - m)
            o = jnp.dot(e.astype(v.dtype), v, preferred_element_type=jnp.float32) / denom
            o_ref[pl.ds(q0, Q_BLOCK), h * HEAD_DIM:(h + 1) * HEAD_DIM] = o.astype(o_ref.dtype)
            if has_lse:
                lse = m + jnp.log(denom)
                lse_ref[pl.ds(q0, Q_BLOCK), h * LSE_LANES_PER_HEAD:(h + 1) * LSE_LANES_PER_HEAD] = (
                    jnp.broadcast_to(lse, (Q_BLOCK, LSE_LANES_PER_HEAD)))
        return carry

    lax.fori_loop(0, length // Q_BLOCK, q_block, 0)


def _band_attn(proj, *, batch, seq, rate, halo, q_col, k_col, v_col, n_q_heads, n_kv_heads,
               sink=None, with_lse=False):
    c_tot = proj.shape[-1]
    length = seq // rate
    qw = n_q_heads * HEAD_DIM
    kw = n_kv_heads * HEAD_DIM
    assert rate == 1 or (c_tot % qw == 0 and c_tot % kw == 0)
    assert q_col % qw == 0 and k_col % kw == 0 and v_col % kw == 0
    assert length % Q_BLOCK == 0
    view = proj.reshape(batch, length, rate * c_tot)
    in_specs = [pl.BlockSpec((None, length, qw), lambda b, c: (b, 0, c * (c_tot // qw) + q_col // qw)),
                pl.BlockSpec((None, length, kw), lambda b, c: (b, 0, c * (c_tot // kw) + k_col // kw)),
                pl.BlockSpec((None, length, kw), lambda b, c: (b, 0, c * (c_tot // kw) + v_col // kw))]
    args = [view, view, view]
    if sink is not None:
        in_specs.append(pl.BlockSpec(memory_space=pltpu.SMEM))
        args.append(sink)
    out_specs = [pl.BlockSpec((None, length, qw), lambda b, c: (b, 0, c))]
    out_shape = [jax.ShapeDtypeStruct((batch, length, rate * qw), jnp.bfloat16)]
    if with_lse:
        out_specs.append(pl.BlockSpec((None, length, LANES), lambda b, c: (b, 0, c)))
        out_shape.append(jax.ShapeDtypeStruct((batch, length, rate * LANES), jnp.float32))
    kern = functools.partial(_band_attn_kernel, length=length, halo=halo, n_q_heads=n_q_heads,
                             n_kv_heads=n_kv_heads, has_sink=sink is not None, has_lse=with_lse)
    outs = pl.pallas_call(
        kern,
        grid=(batch, rate),
        in_specs=in_specs,
        out_specs=out_specs,
        out_shape=out_shape,
        compiler_params=pltpu.CompilerParams(
            dimension_semantics=("arbitrary", "arbitrary"), vmem_limit_bytes=VMEM_LIMIT),
        name=f"band_attn_r{rate}_h{halo}",
    )(*args)
    o = outs[0].reshape(batch * seq, qw)
    if with_lse:
        return o, outs[1].reshape(batch * seq, LANES)
    return o


def _layer_norm(y, g, b):
    mu = jnp.mean(y, axis=-1, keepdims=True)
    d = y - mu
    var = jnp.mean(d * d, axis=-1, keepdims=True)
    return d * lax.rsqrt(var + LN_EPS) * g + b


def _rms_norm(y, g):
    return y * lax.rsqrt(jnp.mean(y * y, axis=-1, keepdims=True) + RMS_EPS) * g


def _mix_ln1_kernel(oa_ref, o1_ref, o2_ref, o3_ref, l1_ref, l2_ref, l3_ref, x_ref, wo_ref,
                    gna_ref, gnb_ref, g_ref, b_ref, h_ref):
    l1, l2, l3 = l1_ref[...], l2_ref[...], l3_ref[...]
    lmax = jnp.maximum(jnp.maximum(l1, l2), l3)
    e1, e2, e3 = jnp.exp(l1 - lmax), jnp.exp(l2 - lmax), jnp.exp(l3 - lmax)
    tot = e1 + e2 + e3
    w1, w2, w3 = e1 / tot, e2 / tot, e3 / tot
    parts = []
    for h in range(B_HEADS):
        c = h * LSE_LANES_PER_HEAD
        cols = slice(h * HEAD_DIM, (h + 1) * HEAD_DIM)
        parts.append(w1[:, c:c + 1] * o1_ref[:, cols].astype(jnp.float32)
                     + w2[:, c:c + 1] * o2_ref[:, cols].astype(jnp.float32)
                     + w3[:, c:c + 1] * o3_ref[:, cols].astype(jnp.float32))
    ob = jnp.concatenate(parts, axis=-1)
    ya = _rms_norm(oa_ref[...].astype(jnp.float32), gna_ref[...])
    yb = _rms_norm(ob, gnb_ref[...])
    mix = (jnp.dot(ya.astype(jnp.bfloat16), wo_ref[:A_Q, :], preferred_element_type=jnp.float32)
           + jnp.dot(yb.astype(jnp.bfloat16), wo_ref[A_Q:, :], preferred_element_type=jnp.float32))
    h_ref[...] = _layer_norm(ALPHA * x_ref[...] + mix, g_ref[...], b_ref[...])


def _mix_ln1(oa, o_pats, l_pats, x2, wo_b, gn_a, gn_b, ln_g, ln_b):
    m = x2.shape[0]
    row = lambda w: pl.BlockSpec((MIX_TM, w), lambda i: (i, 0))
    full = lambda r, w: pl.BlockSpec((r, w), lambda i: (0, 0))
    return pl.pallas_call(
        _mix_ln1_kernel,
        grid=(m // MIX_TM,),
        in_specs=[row(A_Q)] + [row(B_QKV)] * 3 + [row(LANES)] * 3 + [row(D_MODEL),
                  full(D_MODEL, D_MODEL), full(1, A_Q), full(1, B_QKV), full(1, D_MODEL), full(1, D_MODEL)],
        out_specs=row(D_MODEL),
        out_shape=jax.ShapeDtypeStruct((m, D_MODEL), jnp.float32),
        compiler_params=pltpu.CompilerParams(
            dimension_semantics=("arbitrary",), vmem_limit_bytes=VMEM_LIMIT),
        name="mix_ln1",
    )(oa, *o_pats, *l_pats, x2, wo_b, gn_a, gn_b, ln_g, ln_b)


def _ffn_ln2_kernel(h_ref, w1g_ref, w2_ref, p_ref, wple_ref, g_ref, b_ref, o_ref, hb_ref, gate_ref):
    j = pl.program_id(1)
    n_ff = D_FF // FFN_TF
    n_all = (D_FF + D_MODEL) // FFN_TF

    @pl.when(j == 0)
    def _():
        hb_ref[...] = h_ref[...].astype(jnp.bfloat16)

    a = jnp.dot(hb_ref[...], w1g_ref[...], preferred_element_type=jnp.float32)

    @pl.when(j < n_ff)
    def _():
        r = jnp.maximum(a, 0.0)
        part = jnp.dot((r * r).astype(jnp.bfloat16), w2_ref[...], preferred_element_type=jnp.float32)

        @pl.when(j == 0)
        def _():
            o_ref[...] = part

        @pl.when(j > 0)
        def _():
            o_ref[...] += part

    for t in range(n_all - n_ff):
        @pl.when(j == n_ff + t)
        def _():
            gate_ref[:, t * FFN_TF:(t + 1) * FFN_TF] = 1.0 / (1.0 + jnp.exp(-a))

    @pl.when(j == n_all - 1)
    def _():
        ple = jnp.dot(p_ref[...].astype(jnp.bfloat16), wple_ref[...], preferred_element_type=jnp.float32)
        y = ALPHA * h_ref[...] + o_ref[...] + ple * gate_ref[...]
        o_ref[...] = _layer_norm(y, g_ref[...], b_ref[...])


def _ffn_ln2(h1, w1g_b, w2_b, p2, wple_b, ln_g, ln_b):
    m = h1.shape[0]
    n_ff = D_FF // FFN_TF
    n_all = (D_FF + D_MODEL) // FFN_TF
    return pl.pallas_call(
        _ffn_ln2_kernel,
        grid=(m // FFN_TM, n_all),
        in_specs=[pl.BlockSpec((FFN_TM, D_MODEL), lambda i, j: (i, 0)),
                  pl.BlockSpec((D_MODEL, FFN_TF), lambda i, j: (0, j)),
                  pl.BlockSpec((FFN_TF, D_MODEL), lambda i, j: (jnp.minimum(j, n_ff - 1), 0)),
                  pl.BlockSpec((FFN_TM, PLE_DIM), lambda i, j: (i, 0)),
                  pl.BlockSpec((PLE_DIM, D_MODEL), lambda i, j: (0, 0)),
                  pl.BlockSpec((1, D_MODEL), lambda i, j: (0, 0)),
                  pl.BlockSpec((1, D_MODEL), lambda i, j: (0, 0))],
        out_specs=pl.BlockSpec((FFN_TM, D_MODEL), lambda i, j: (i, 0)),
        out_shape=jax.ShapeDtypeStruct((m, D_MODEL), jnp.float32),
        scratch_shapes=[pltpu.VMEM((FFN_TM, D_MODEL), jnp.bfloat16),
                        pltpu.VMEM((FFN_TM, D_MODEL), jnp.float32)],
        compiler_params=pltpu.CompilerParams(
            dimension_semantics=("arbitrary", "arbitrary"), vmem_limit_bytes=VMEM_LIMIT),
        name="ffn_ln2",
    )(h1, w1g_b, w2_b, p2, wple_b, ln_g, ln_b)


def kernel(x, p, positions, w_in, sink_a, gn_a, gn_b, w_o, ln1_g, ln1_b, w1, w2, w_ple, w_ple_gate,
           ln2_g, ln2_b):
    batch, seq, d = x.shape
    m = batch * seq
    bf = jnp.bfloat16
    cos_t, sin_t = _rope_tables(positions)
    h = x.reshape(m, d)
    for i in range(DEPTH):
        proj_a, proj_b = _in_proj(h, w_in[i].astype(bf), cos_t, sin_t)
        oa = _band_attn(proj_a, batch=batch, seq=seq, rate=1, halo=A_WINDOW, q_col=0, k_col=A_Q,
                        v_col=A_Q + A_KV, n_q_heads=A_HEADS, n_kv_heads=A_KV_HEADS, sink=sink_a[i])
        o_pats, l_pats = [], []
        for window, rate in DILATED_PATTERNS:
            o, l = _band_attn(proj_b, batch=batch, seq=seq, rate=rate, halo=window // (2 * rate),
                              q_col=0, k_col=B_QKV, v_col=2 * B_QKV, n_q_heads=B_HEADS,
                              n_kv_heads=B_HEADS, with_lse=True)
            o_pats.append(o)
            l_pats.append(l)
        h1 = _mix_ln1(oa, o_pats, l_pats, h, w_o[i].astype(bf), gn_a[i].reshape(1, -1),
                      gn_b[i].reshape(1, -1), ln1_g[i].reshape(1, -1), ln1_b[i].reshape(1, -1))
        w1g = jnp.concatenate([w1[i], w_ple_gate[i]], axis=1).astype(bf)
        h = _ffn_ln2(h1, w1g, w2[i].astype(bf), p[i].reshape(m, PLE_DIM), w_ple[i].astype(bf),
                     ln2_g[i].reshape(1, -1), ln2_b[i].reshape(1, -1))
    return h.reshape(batch, seq, d)
```

```python
import functools

import jax
import jax.numpy as jnp
from jax import lax
from jax.experimental import pallas as pl
from jax.experimental.pallas import tpu as pltpu

D_MODEL = 2048
HEAD_DIM = 128
A_HEADS = 8
A_KV_HEADS = 2
A_WINDOW = 128
B_HEADS = 8
DILATED_PATTERNS = ((128, 1), (512, 4), (2048, 16))
ROT_DIM = HEAD_DIM // 4
ROPE_THETA = 500000.0
D_FF = 4 * D_MODEL
PLE_DIM = 256
DEPTH = 1
ALPHA = (2.0 * DEPTH) ** 0.25
LN_EPS = 1e-5
RMS_EPS = 1e-6
NEG_INF = -1e30

A_Q = A_HEADS * HEAD_DIM
A_KV = A_KV_HEADS * HEAD_DIM
B_QKV = B_HEADS * HEAD_DIM
D_IN = A_Q + 2 * A_KV + 3 * B_QKV
N_PROJ_HEADS = D_IN // HEAD_DIM
QA_HEAD0 = 0
KA_HEAD0 = A_HEADS
VA_HEAD0 = A_HEADS + A_KV_HEADS
QB_HEAD0 = A_HEADS + 2 * A_KV_HEADS
KB_HEAD0 = QB_HEAD0 + B_HEADS
VB_HEAD0 = KB_HEAD0 + B_HEADS

LANES = 128
MXU_COLS = 256
Q_BLOCK = 128
VMEM_LIMIT = 56 * 1024 * 1024

PROJ_TM = 1024
PROJ_TN = 1536
MIX_TM = 512
FFN_TM = 512
FFN_TF = 1024
FFN_SUB = 512
ROPE_TM = 2048
KIND_Q, KIND_K, KIND_V = 0, 1, 2


def _rope_table_kernel(pos_ref, freq_ref, cos_ref, sin_ref):
    ang = pos_ref[...].astype(jnp.float32) * freq_ref[...]
    lane = lax.broadcasted_iota(jnp.int32, ang.shape, 1)
    half = ROT_DIM // 2
    c = jnp.where(lane < ROT_DIM, jnp.cos(ang), 1.0)
    s = jnp.sin(ang)
    s = jnp.where(lane < half, -s, jnp.where(lane < ROT_DIM, s, 0.0))
    q_scale = HEAD_DIM ** -0.5
    cos_ref[KIND_Q] = c * q_scale
    sin_ref[KIND_Q] = s * q_scale
    cos_ref[KIND_K] = c
    sin_ref[KIND_K] = s
    cos_ref[KIND_V] = jnp.ones_like(c)
    sin_ref[KIND_V] = jnp.zeros_like(s)


def _rope_tables(positions):
    m = positions.size
    half = ROT_DIM // 2
    inv_freq = ROPE_THETA ** (-jnp.arange(0, ROT_DIM, 2, dtype=jnp.float32) / ROT_DIM)
    freq = jnp.tile(inv_freq, LANES // half).reshape(1, LANES)
    pos = positions.reshape(m, 1)
    out = jax.ShapeDtypeStruct((3, m, LANES), jnp.float32)
    return pl.pallas_call(
        _rope_table_kernel,
        grid=(m // ROPE_TM,),
        in_specs=[pl.BlockSpec((ROPE_TM, 1), lambda i: (i, 0)),
                  pl.BlockSpec((1, LANES), lambda i: (0, 0))],
        out_specs=[pl.BlockSpec((3, ROPE_TM, LANES), lambda i: (0, i, 0))] * 2,
        out_shape=[out, out],
        name="rope_table",
    )(pos, freq)


def _in_proj_kernel(x_ref, w_ref, cos_ref, sin_ref, o_ref, xb_ref):
    j = pl.program_id(1)

    @pl.when(j == 0)
    def _():
        xb_ref[...] = x_ref[...].astype(jnp.bfloat16)

    half = ROT_DIM // 2
    lane = lax.broadcasted_iota(jnp.int32, (PROJ_TM, LANES), 1)
    heads_per_tile = PROJ_TN // HEAD_DIM
    heads_per_chunk = MXU_COLS // HEAD_DIM
    for c in range(PROJ_TN // MXU_COLS):
        acc = jnp.dot(xb_ref[...], w_ref[:, c * MXU_COLS:(c + 1) * MXU_COLS],
                      preferred_element_type=jnp.float32)
        for g in range(heads_per_chunk):
            local = c * heads_per_chunk + g
            head = j * heads_per_tile + local
            is_q = (head < KA_HEAD0) | ((head >= QB_HEAD0) & (head < KB_HEAD0))
            is_k = ((head >= KA_HEAD0) & (head < VA_HEAD0)) | ((head >= KB_HEAD0) & (head < VB_HEAD0))
            kind = jnp.where(is_q, KIND_Q, jnp.where(is_k, KIND_K, KIND_V))
            t = acc[:, g * HEAD_DIM:(g + 1) * HEAD_DIM]
            swapped = jnp.where(lane < half,
                                pltpu.roll(t, LANES - half, 1),
                                pltpu.roll(t, half, 1))
            o_ref[local] = t * cos_ref[kind] + swapped * sin_ref[kind]


def _in_proj(x2, w_in_b, cos_t, sin_t):
    m = x2.shape[0]
    heads_per_tile = PROJ_TN // HEAD_DIM
    return pl.pallas_call(
        _in_proj_kernel,
        grid=(m // PROJ_TM, D_IN // PROJ_TN),
        in_specs=[pl.BlockSpec((PROJ_TM, D_MODEL), lambda i, j: (i, 0)),
                  pl.BlockSpec((D_MODEL, PROJ_TN), lambda i, j: (0, j)),
                  pl.BlockSpec((3, PROJ_TM, LANES), lambda i, j: (0, i, 0)),
                  pl.BlockSpec((3, PROJ_TM, LANES), lambda i, j: (0, i, 0))],
        out_specs=pl.BlockSpec((heads_per_tile, PROJ_TM, HEAD_DIM), lambda i, j: (j, i, 0)),
        out_shape=jax.ShapeDtypeStruct((N_PROJ_HEADS, m, HEAD_DIM), jnp.float32),
        scratch_shapes=[pltpu.VMEM((PROJ_TM, D_MODEL), jnp.bfloat16)],
        compiler_params=pltpu.CompilerParams(
            dimension_semantics=("arbitrary", "arbitrary"), vmem_limit_bytes=VMEM_LIMIT),
        name="in_proj",
    )(x2, w_in_b, cos_t, sin_t)


def _band_bias(q0, k0, rows, cols, halo):
    qpos = q0 + lax.broadcasted_iota(jnp.int32, (rows, cols), 0)
    kpos = k0 + lax.broadcasted_iota(jnp.int32, (rows, cols), 1)
    return jnp.where(jnp.abs(qpos - kpos) <= halo, 0.0, NEG_INF).astype(jnp.float32)


def _softmax_pv(s, v, sink=None):
    m = jnp.max(s, axis=-1, keepdims=True)
    if sink is not None:
        m = jnp.maximum(m, sink)
    e = jnp.exp(s - m)
    denom = jnp.sum(e, axis=-1, keepdims=True)
    if sink is not None:
        denom = denom + jnp.exp(sink - m)
    o = jnp.dot(e.astype(v.dtype), v, preferred_element_type=jnp.float32) / denom
    return o, m, denom


def _qk(q, k):
    return lax.dot_general(q, k, (((1,), (1,)), ((), ())), preferred_element_type=jnp.float32)


def _window_attn_kernel(q_ref, k_ref, v_ref, sink_ref, o_ref, *, seq):
    grp = A_HEADS // A_KV_HEADS
    window = Q_BLOCK + 2 * A_WINDOW
    bf = jnp.bfloat16

    def q_block(j, carry):
        q0 = pl.multiple_of(j * Q_BLOCK, Q_BLOCK)
        w0 = pl.multiple_of(jnp.clip(q0 - A_WINDOW, 0, seq - window), Q_BLOCK)
        bias = _band_bias(q0, w0, Q_BLOCK, window, A_WINDOW)
        bias = jnp.concatenate([bias] * grp, axis=0)
        for g in range(A_KV_HEADS):
            q = jnp.concatenate([q_ref[g * grp + i, pl.ds(q0, Q_BLOCK), :] for i in range(grp)], axis=0)
            k = k_ref[g, pl.ds(w0, window), :].astype(bf)
            v = v_ref[g, pl.ds(w0, window), :].astype(bf)
            sink = jnp.concatenate(
                [jnp.full((Q_BLOCK, 1), sink_ref[g * grp + i], jnp.float32) for i in range(grp)], axis=0)
            o, _, _ = _softmax_pv(_qk(q.astype(bf), k) + bias, v, sink)
            for i in range(grp):
                h = g * grp + i
                o_ref[pl.ds(q0, Q_BLOCK), h * HEAD_DIM:(h + 1) * HEAD_DIM] = (
                    o[i * Q_BLOCK:(i + 1) * Q_BLOCK].astype(o_ref.dtype))
        return carry

    lax.fori_loop(0, seq // Q_BLOCK, q_block, 0)


def _window_attn(proj, sink, *, batch, seq):
    m = batch * seq
    return pl.pallas_call(
        functools.partial(_window_attn_kernel, seq=seq),
        grid=(batch,),
        in_specs=[pl.BlockSpec((A_HEADS, seq, HEAD_DIM), lambda b: (QA_HEAD0 // A_HEADS, b, 0)),
                  pl.BlockSpec((A_KV_HEADS, seq, HEAD_DIM), lambda b: (KA_HEAD0 // A_KV_HEADS, b, 0)),
                  pl.BlockSpec((A_KV_HEADS, seq, HEAD_DIM), lambda b: (VA_HEAD0 // A_KV_HEADS, b, 0)),
                  pl.BlockSpec(memory_space=pltpu.SMEM)],
        out_specs=pl.BlockSpec((seq, A_Q), lambda b: (b, 0)),
        out_shape=jax.ShapeDtypeStruct((m, A_Q), jnp.bfloat16),
        compiler_params=pltpu.CompilerParams(
            dimension_semantics=("arbitrary",), vmem_limit_bytes=VMEM_LIMIT),
        name="window_attn",
    )(proj, proj, proj, sink)


def _dilated_attn_kernel(q_ref, k_ref, v_ref, o_ref, qb_ref, kb_ref, vb_ref, *pat_refs, seq):
    bf = jnp.bfloat16
    strided = [(w, r) for (w, r) in DILATED_PATTERNS if r > 1]
    natural = [(w, r) for (w, r) in DILATED_PATTERNS if r == 1]
    assert len(natural) == 1 and len(pat_refs) == 2 * len(strided)

    def band_blocks(q, k, v, length, halo):
        window = min(Q_BLOCK + 2 * halo, length)
        outs, lses = [], []
        for j in range(length // Q_BLOCK):
            q0 = j * Q_BLOCK
            w0 = min(max(q0 - halo, 0), length - window)
            s = _qk(q[q0:q0 + Q_BLOCK], k[w0:w0 + window]) + _band_bias(q0, w0, Q_BLOCK, window, halo)
            o, m, denom = _softmax_pv(s, v[w0:w0 + window])
            outs.append(o)
            lses.append(jnp.broadcast_to(m + jnp.log(denom), (Q_BLOCK, LANES)))
        return jnp.concatenate(outs, axis=0), jnp.concatenate(lses, axis=0)

    for p, (win, rate) in enumerate(strided):
        length = seq // rate
        halo = win // (2 * rate)
        op_ref, lp_ref = pat_refs[2 * p], pat_refs[2 * p + 1]

        def one_class(c, carry, rate=rate, length=length, halo=halo, op_ref=op_ref, lp_ref=lp_ref):
            rows = pl.ds(c, length, stride=rate)
            o, lse = band_blocks(q_ref[rows, :].astype(bf), k_ref[rows, :].astype(bf),
                                 v_ref[rows, :].astype(bf), length, halo)
            op_ref[rows, :] = o
            lp_ref[rows, :] = lse
            return carry

        lax.fori_loop(0, rate, one_class, 0)

    qb_ref[...] = q_ref[...].astype(bf)
    kb_ref[...] = k_ref[...].astype(bf)
    vb_ref[...] = v_ref[...].astype(bf)
    halo = natural[0][0] // 2
    window = min(Q_BLOCK + 2 * halo, seq)

    def q_block(j, carry):
        q0 = pl.multiple_of(j * Q_BLOCK, Q_BLOCK)
        w0 = pl.multiple_of(jnp.clip(q0 - halo, 0, seq - window), 64)
        s = _qk(qb_ref[pl.ds(q0, Q_BLOCK), :], kb_ref[pl.ds(w0, window), :]) + _band_bias(
            q0, w0, Q_BLOCK, window, halo)
        o, m, denom = _softmax_pv(s, vb_ref[pl.ds(w0, window), :])
        outs = [o] + [pat_refs[2 * p][pl.ds(q0, Q_BLOCK), :] for p in range(len(strided))]
        lses = [jnp.broadcast_to(m + jnp.log(denom), (Q_BLOCK, LANES))] + [
            pat_refs[2 * p + 1][pl.ds(q0, Q_BLOCK), :] for p in range(len(strided))]
        lmax = functools.reduce(jnp.maximum, lses)
        es = [jnp.exp(l - lmax) for l in lses]
        tot = functools.reduce(lambda a, b: a + b, es)
        mixed = functools.reduce(lambda a, b: a + b, [(e / tot) * o_p for e, o_p in zip(es, outs)])
        o_ref[pl.ds(q0, Q_BLOCK), :] = mixed.astype(o_ref.dtype)
        return carry

    lax.fori_loop(0, seq // Q_BLOCK, q_block, 0)


def _dilated_attn(proj, *, batch, seq):
    m = batch * seq
    n_strided = sum(1 for _, r in DILATED_PATTERNS if r > 1)
    head_spec = lambda h0: pl.BlockSpec((None, seq, HEAD_DIM), lambda b, h: (h0 + h, b, 0))
    return pl.pallas_call(
        functools.partial(_dilated_attn_kernel, seq=seq),
        grid=(batch, B_HEADS),
        in_specs=[head_spec(QB_HEAD0), head_spec(KB_HEAD0), head_spec(VB_HEAD0)],
        out_specs=pl.BlockSpec((seq, HEAD_DIM), lambda b, h: (b, h)),
        out_shape=jax.ShapeDtypeStruct((m, B_QKV), jnp.bfloat16),
        scratch_shapes=[pltpu.VMEM((seq, HEAD_DIM), jnp.bfloat16)] * 3
                       + [pltpu.VMEM((seq, HEAD_DIM), jnp.float32)] * (2 * n_strided),
        compiler_params=pltpu.CompilerParams(
            dimension_semantics=("arbitrary", "arbitrary"), vmem_limit_bytes=VMEM_LIMIT),
        name="dilated_attn",
    )(proj, proj, proj)


def _layer_norm(y, g, b):
    mu = jnp.mean(y, axis=-1, keepdims=True)
    d = y - mu
    var = jnp.mean(d * d, axis=-1, keepdims=True)
    return d * lax.rsqrt(var + LN_EPS) * g + b


def _rms_norm(y, g):
    return y * lax.rsqrt(jnp.mean(y * y, axis=-1, keepdims=True) + RMS_EPS) * g


def _mix_ln1_kernel(oa_ref, ob_ref, x_ref, wo_ref, gna_ref, gnb_ref, g_ref, b_ref, h_ref):
    ya = _rms_norm(oa_ref[...].astype(jnp.float32), gna_ref[...])
    yb = _rms_norm(ob_ref[...].astype(jnp.float32), gnb_ref[...])
    mix = (jnp.dot(ya.astype(jnp.bfloat16), wo_ref[:A_Q, :], preferred_element_type=jnp.float32)
           + jnp.dot(yb.astype(jnp.bfloat16), wo_ref[A_Q:, :], preferred_element_type=jnp.float32))
    h_ref[...] = _layer_norm(ALPHA * x_ref[...] + mix, g_ref[...], b_ref[...])


def _mix_ln1(oa, ob, x2, wo_b, gn_a, gn_b, ln_g, ln_b):
    m = x2.shape[0]
    row = lambda w: pl.BlockSpec((MIX_TM, w), lambda i: (i, 0))
    full = lambda r, w: pl.BlockSpec((r, w), lambda i: (0, 0))
    return pl.pallas_call(
        _mix_ln1_kernel,
        grid=(m // MIX_TM,),
        in_specs=[row(A_Q), row(B_QKV), row(D_MODEL), full(D_MODEL, D_MODEL), full(1, A_Q),
                  full(1, B_QKV), full(1, D_MODEL), full(1, D_MODEL)],
        out_specs=row(D_MODEL),
        out_shape=jax.ShapeDtypeStruct((m, D_MODEL), jnp.float32),
        compiler_params=pltpu.CompilerParams(
            dimension_semantics=("arbitrary",), vmem_limit_bytes=VMEM_LIMIT),
        name="mix_ln1",
    )(oa, ob, x2, wo_b, gn_a, gn_b, ln_g, ln_b)


def _ffn_ln2_kernel(h_ref, w1g_ref, w2_ref, p_ref, wple_ref, g_ref, b_ref, o_ref, hb_ref, gate_ref):
    j = pl.program_id(1)
    n_ff = D_FF // FFN_TF
    n_all = (D_FF + D_MODEL) // FFN_TF
    n_sub = FFN_TF // FFN_SUB

    @pl.when(j == 0)
    def _():
        hb_ref[...] = h_ref[...].astype(jnp.bfloat16)
        o_ref[...] = jnp.zeros_like(o_ref)

    def up(c):
        return jnp.dot(hb_ref[...], w1g_ref[:, c * FFN_SUB:(c + 1) * FFN_SUB],
                       preferred_element_type=jnp.float32)

    @pl.when(j < n_ff)
    def _():
        acts = []
        for c in range(n_sub):
            r = jnp.maximum(up(c), 0.0)
            acts.append((r * r).astype(jnp.bfloat16))
        for n in range(D_MODEL // FFN_SUB):
            cols = slice(n * FFN_SUB, (n + 1) * FFN_SUB)
            part = sum(jnp.dot(acts[c], w2_ref[c * FFN_SUB:(c + 1) * FFN_SUB, cols],
                               preferred_element_type=jnp.float32) for c in range(n_sub))
            o_ref[:, cols] += part

    for t in range(n_all - n_ff):
        @pl.when(j == n_ff + t)
        def _():
            for c in range(n_sub):
                lo = t * FFN_TF + c * FFN_SUB
                gate_ref[:, lo:lo + FFN_SUB] = 1.0 / (1.0 + jnp.exp(-up(c)))

    @pl.when(j == n_all - 1)
    def _():
        ple = jnp.dot(p_ref[...].astype(jnp.bfloat16), wple_ref[...], preferred_element_type=jnp.float32)
        y = ALPHA * h_ref[...] + o_ref[...] + ple * gate_ref[...]
        o_ref[...] = _layer_norm(y, g_ref[...], b_ref[...])


def _ffn_ln2(h1, w1g_b, w2_b, p2, wple_b, ln_g, ln_b):
    m = h1.shape[0]
    n_ff = D_FF // FFN_TF
    n_all = (D_FF + D_MODEL) // FFN_TF
    return pl.pallas_call(
        _ffn_ln2_kernel,
        grid=(m // FFN_TM, n_all),
        in_specs=[pl.BlockSpec((FFN_TM, D_MODEL), lambda i, j: (i, 0)),
                  pl.BlockSpec((D_MODEL, FFN_TF), lambda i, j: (0, j)),
                  pl.BlockSpec((FFN_TF, D_MODEL), lambda i, j: (jnp.minimum(j, n_ff - 1), 0)),
                  pl.BlockSpec((FFN_TM, PLE_DIM), lambda i, j: (i, 0)),
                  pl.BlockSpec((PLE_DIM, D_MODEL), lambda i, j: (0, 0)),
                  pl.BlockSpec((1, D_MODEL), lambda i, j: (0, 0)),
                  pl.BlockSpec((1, D_MODEL), lambda i, j: (0, 0))],
        out_specs=pl.BlockSpec((FFN_TM, D_MODEL), lambda i, j: (i, 0)),
        out_shape=jax.ShapeDtypeStruct((m, D_MODEL), jnp.float32),
        scratch_shapes=[pltpu.VMEM((FFN_TM, D_MODEL), jnp.bfloat16),
                        pltpu.VMEM((FFN_TM, D_MODEL), jnp.float32)],
        compiler_params=pltpu.CompilerParams(
            dimension_semantics=("arbitrary", "arbitrary"), vmem_limit_bytes=VMEM_LIMIT),
        name="ffn_ln2",
    )(h1, w1g_b, w2_b, p2, wple_b, ln_g, ln_b)


def kernel(x, p, positions, w_in, sink_a, gn_a, gn_b, w_o, ln1_g, ln1_b, w1, w2, w_ple, w_ple_gate,
           ln2_g, ln2_b):
    batch, seq, d = x.shape
    m = batch * seq
    bf = jnp.bfloat16
    cos_t, sin_t = _rope_tables(positions)
    h = x.reshape(m, d)
    for i in range(DEPTH):
        proj = _in_proj(h, w_in[i].astype(bf), cos_t, sin_t)
        oa = _window_attn(proj, sink_a[i], batch=batch, seq=seq)
        ob = _dilated_attn(proj, batch=batch, seq=seq)
        h1 = _mix_ln1(oa, ob, h, w_o[i].astype(bf), gn_a[i].reshape(1, -1), gn_b[i].reshape(1, -1),
                      ln1_g[i].reshape(1, -1), ln1_b[i].reshape(1, -1))
        w1g = jnp.concatenate([w1[i], w_ple_gate[i]], axis=1).astype(bf)
        h = _ffn_ln2(h1, w1g, w2[i].astype(bf), p[i].reshape(m, PLE_DIM), w_ple[i].astype(bf),
                     ln2_g[i].reshape(1, -1), ln2_b[i].reshape(1, -1))
    return h.reshape(batch, seq, d)
```

```python
import functools

import jax
import jax.numpy as jnp
from jax import lax
from jax.experimental import pallas as pl
from jax.experimental.pallas import tpu as pltpu

D_MODEL = 2048
HEAD_DIM = 128
A_HEADS = 8
A_KV_HEADS = 2
A_WINDOW = 128
B_HEADS = 8
DILATED_PATTERNS = ((128, 1), (512, 4), (2048, 16))
ROT_DIM = HEAD_DIM // 4
ROPE_THETA = 500000.0
D_FF = 4 * D_MODEL
PLE_DIM = 256
DEPTH = 1
ALPHA = (2.0 * DEPTH) ** 0.25
LN_EPS = 1e-5
RMS_EPS = 1e-6
NEG_INF = -1e30

A_Q = A_HEADS * HEAD_DIM
A_KV = A_KV_HEADS * HEAD_DIM
B_QKV = B_HEADS * HEAD_DIM
D_IN = A_Q + 2 * A_KV + 3 * B_QKV
N_PROJ_HEADS = D_IN // HEAD_DIM
QA_HEAD0 = 0
KA_HEAD0 = A_HEADS
VA_HEAD0 = A_HEADS + A_KV_HEADS
QB_HEAD0 = A_HEADS + 2 * A_KV_HEADS
KB_HEAD0 = QB_HEAD0 + B_HEADS
VB_HEAD0 = KB_HEAD0 + B_HEADS

LANES = 128
MXU_COLS = 256
Q_BLOCK = 128
VMEM_LIMIT = 56 * 1024 * 1024

PROJ_TM = 1024
PROJ_TN = 1536
MIX_TM = 512
FFN_TM = 512
FFN_TF = 1024
FFN_SUB = 512
ROPE_TM = 2048
KIND_Q, KIND_K, KIND_V = 0, 1, 2


def _rope_table_kernel(pos_ref, freq_ref, cos_ref, sin_ref):
    ang = pos_ref[...].astype(jnp.float32) * freq_ref[...]
    lane = lax.broadcasted_iota(jnp.int32, ang.shape, 1)
    half = ROT_DIM // 2
    c = jnp.where(lane < ROT_DIM, jnp.cos(ang), 1.0)
    s = jnp.sin(ang)
    s = jnp.where(lane < half, -s, jnp.where(lane < ROT_DIM, s, 0.0))
    q_scale = HEAD_DIM ** -0.5
    cos_ref[KIND_Q] = c * q_scale
    sin_ref[KIND_Q] = s * q_scale
    cos_ref[KIND_K] = c
    sin_ref[KIND_K] = s
    cos_ref[KIND_V] = jnp.ones_like(c)
    sin_ref[KIND_V] = jnp.zeros_like(s)


def _rope_tables(positions):
    m = positions.size
    half = ROT_DIM // 2
    inv_freq = ROPE_THETA ** (-jnp.arange(0, ROT_DIM, 2, dtype=jnp.float32) / ROT_DIM)
    freq = jnp.tile(inv_freq, LANES // half).reshape(1, LANES)
    pos = positions.reshape(m, 1)
    out = jax.ShapeDtypeStruct((3, m, LANES), jnp.float32)
    return pl.pallas_call(
        _rope_table_kernel,
        grid=(m // ROPE_TM,),
        in_specs=[pl.BlockSpec((ROPE_TM, 1), lambda i: (i, 0)),
                  pl.BlockSpec((1, LANES), lambda i: (0, 0))],
        out_specs=[pl.BlockSpec((3, ROPE_TM, LANES), lambda i: (0, i, 0))] * 2,
        out_shape=[out, out],
        name="rope_table",
    )(pos, freq)


def _in_proj_kernel(x_ref, w_ref, cos_ref, sin_ref, o_ref, xb_ref):
    j = pl.program_id(1)

    @pl.when(j == 0)
    def _():
        xb_ref[...] = x_ref[...].astype(jnp.bfloat16)

    half = ROT_DIM // 2
    lane = lax.broadcasted_iota(jnp.int32, (PROJ_TM, LANES), 1)
    heads_per_tile = PROJ_TN // HEAD_DIM
    heads_per_chunk = MXU_COLS // HEAD_DIM
    for c in range(PROJ_TN // MXU_COLS):
        acc = jnp.dot(xb_ref[...], w_ref[:, c * MXU_COLS:(c + 1) * MXU_COLS],
                      preferred_element_type=jnp.float32)
        for g in range(heads_per_chunk):
            local = c * heads_per_chunk + g
            head = j * heads_per_tile + local
            is_q = (head < KA_HEAD0) | ((head >= QB_HEAD0) & (head < KB_HEAD0))
            is_k = ((head >= KA_HEAD0) & (head < VA_HEAD0)) | ((head >= KB_HEAD0) & (head < VB_HEAD0))
            kind = jnp.where(is_q, KIND_Q, jnp.where(is_k, KIND_K, KIND_V))
            t = acc[:, g * HEAD_DIM:(g + 1) * HEAD_DIM]
            swapped = jnp.where(lane < half,
                                pltpu.roll(t, LANES - half, 1),
                                pltpu.roll(t, half, 1))
            o_ref[local] = t * cos_ref[kind] + swapped * sin_ref[kind]


def _in_proj(x2, w_in_b, cos_t, sin_t):
    m = x2.shape[0]
    heads_per_tile = PROJ_TN // HEAD_DIM
    return pl.pallas_call(
        _in_proj_kernel,
        grid=(m // PROJ_TM, D_IN // PROJ_TN),
        in_specs=[pl.BlockSpec((PROJ_TM, D_MODEL), lambda i, j: (i, 0)),
                  pl.BlockSpec((D_MODEL, PROJ_TN), lambda i, j: (0, j)),
                  pl.BlockSpec((3, PROJ_TM, LANES), lambda i, j: (0, i, 0)),
                  pl.BlockSpec((3, PROJ_TM, LANES), lambda i, j: (0, i, 0))],
        out_specs=pl.BlockSpec((heads_per_tile, PROJ_TM, HEAD_DIM), lambda i, j: (j, i, 0)),
        out_shape=jax.ShapeDtypeStruct((N_PROJ_HEADS, m, HEAD_DIM), jnp.float32),
        scratch_shapes=[pltpu.VMEM((PROJ_TM, D_MODEL), jnp.bfloat16)],
        compiler_params=pltpu.CompilerParams(
            dimension_semantics=("arbitrary", "arbitrary"), vmem_limit_bytes=VMEM_LIMIT),
        name="in_proj",
    )(x2, w_in_b, cos_t, sin_t)


def _band_bias(delta, rows, cols, halo):
    d = delta + lax.broadcasted_iota(jnp.int32, (rows, cols), 0) - lax.broadcasted_iota(jnp.int32, (rows, cols), 1)
    return jnp.where(jnp.abs(d) <= halo, 0.0, NEG_INF).astype(jnp.float32)


def _softmax_pv(s, v, sink=None):
    m = jnp.max(s, axis=-1, keepdims=True)
    if sink is not None:
        m = jnp.maximum(m, sink)
    e = jnp.exp(s - m)
    denom = jnp.sum(e, axis=-1, keepdims=True)
    if sink is not None:
        denom = denom + jnp.exp(sink - m)
    o = jnp.dot(e.astype(v.dtype), v, preferred_element_type=jnp.float32) / denom
    return o, m, denom


def _qk(q, k):
    return lax.dot_general(q, k, (((1,), (1,)), ((), ())), preferred_element_type=jnp.float32)


def _window_start(q0, halo, window, length):
    return min(max(q0 - halo, 0), length - window)


def _window_attn_kernel(q_ref, k_ref, v_ref, sink_ref, o_ref, *, seq):
    grp = A_HEADS // A_KV_HEADS
    window = Q_BLOCK + 2 * A_WINDOW
    bf = jnp.bfloat16

    def q_block(j, carry):
        q0 = pl.multiple_of(j * Q_BLOCK, Q_BLOCK)
        w0 = pl.multiple_of(jnp.clip(q0 - A_WINDOW, 0, seq - window), Q_BLOCK)
        bias = _band_bias(q0 - w0, Q_BLOCK, window, A_WINDOW)
        for g in range(A_KV_HEADS):
            k = k_ref[g, pl.ds(w0, window), :].astype(bf)
            v = v_ref[g, pl.ds(w0, window), :].astype(bf)
            for i in range(grp):
                h = g * grp + i
                q = q_ref[h, pl.ds(q0, Q_BLOCK), :].astype(bf)
                o, _, _ = _softmax_pv(_qk(q, k) + bias, v, sink_ref[h])
                o_ref[pl.ds(q0, Q_BLOCK), h * HEAD_DIM:(h + 1) * HEAD_DIM] = o.astype(o_ref.dtype)
        return carry

    lax.fori_loop(0, seq // Q_BLOCK, q_block, 0, unroll=2)


def _window_attn(proj, sink, *, batch, seq):
    m = batch * seq
    return pl.pallas_call(
        functools.partial(_window_attn_kernel, seq=seq),
        grid=(batch,),
        in_specs=[pl.BlockSpec((A_HEADS, seq, HEAD_DIM), lambda b: (QA_HEAD0 // A_HEADS, b, 0)),
                  pl.BlockSpec((A_KV_HEADS, seq, HEAD_DIM), lambda b: (KA_HEAD0 // A_KV_HEADS, b, 0)),
                  pl.BlockSpec((A_KV_HEADS, seq, HEAD_DIM), lambda b: (VA_HEAD0 // A_KV_HEADS, b, 0)),
                  pl.BlockSpec(memory_space=pltpu.SMEM)],
        out_specs=pl.BlockSpec((seq, A_Q), lambda b: (b, 0)),
        out_shape=jax.ShapeDtypeStruct((m, A_Q), jnp.bfloat16),
        compiler_params=pltpu.CompilerParams(
            dimension_semantics=("arbitrary",), vmem_limit_bytes=VMEM_LIMIT),
        name="window_attn",
    )(proj, proj, proj, sink)


def _mix2(oa, la, ob, lb):
    lmax = jnp.maximum(la, lb)
    ea = jnp.exp(la - lmax)
    eb = jnp.exp(lb - lmax)
    tot = ea + eb
    return (ea * oa + eb * ob) / tot, lmax + jnp.log(tot)


def _dilated_attn_kernel(q_ref, k_ref, v_ref, o_ref, nat_ref, cls_ref, oc_ref, lc_ref, on_ref, ln_ref, *, seq):
    bf = jnp.bfloat16
    (w1, r1), (wa, ra), (wb, rb) = sorted(DILATED_PATTERNS, key=lambda pat: pat[1])
    assert r1 == 1 and rb % ra == 0
    sub = rb // ra
    len_a, len_b = seq // ra, seq // rb
    halo1, halo_a, halo_b = w1 // 2, wa // (2 * ra), wb // (2 * rb)
    biases = {}

    def unit(q, k, v, delta, halo):
        key = (delta, q.shape[0], k.shape[0], halo)
        if key not in biases:
            biases[key] = _band_bias(*key)
        o, m, denom = _softmax_pv(_qk(q, k) + biases[key], v)
        return o, jnp.broadcast_to(m + jnp.log(denom), (q.shape[0], LANES))

    def blocks(length, halo):
        window = min(Q_BLOCK + 2 * halo, length)
        for j in range(length // Q_BLOCK):
            q0 = j * Q_BLOCK
            yield q0, _window_start(q0, halo, window, length), window

    for t, src in enumerate((q_ref, k_ref, v_ref)):
        nat_ref[t] = src[...].astype(bf)
        for c in range(ra):
            cls_ref[t, c * len_a:(c + 1) * len_a, :] = src[pl.ds(c, len_a, stride=ra), :]

    for c in range(ra):
        for u in range(sub):
            rows = pl.ds(c * len_a + u, len_b, stride=sub)
            q, k, v = (cls_ref[t, rows, :].astype(bf) for t in range(3))
            outs, lses = [], []
            for q0, w0, window in blocks(len_b, halo_b):
                o, l = unit(q[q0:q0 + Q_BLOCK], k[w0:w0 + window], v[w0:w0 + window], q0 - w0, halo_b)
                outs.append(o)
                lses.append(l)
            oc_ref[rows, :] = jnp.concatenate(outs, axis=0)
            lc_ref[rows, :] = jnp.concatenate(lses, axis=0)

    for c in range(ra):
        base = c * len_a
        for q0, w0, window in blocks(len_a, halo_a):
            q = cls_ref[0, base + q0:base + q0 + Q_BLOCK, :].astype(bf)
            k = cls_ref[1, base + w0:base + w0 + window, :].astype(bf)
            v = cls_ref[2, base + w0:base + w0 + window, :].astype(bf)
            o, l = unit(q, k, v, q0 - w0, halo_a)
            o, l = _mix2(o, l, oc_ref[base + q0:base + q0 + Q_BLOCK, :], lc_ref[base + q0:base + q0 + Q_BLOCK, :])
            rows = pl.ds(q0 * ra + c, Q_BLOCK, stride=ra)
            on_ref[rows, :] = o
            ln_ref[rows, :] = l

    for q0, w0, window in blocks(seq, halo1):
        o, l = unit(nat_ref[0, q0:q0 + Q_BLOCK, :], nat_ref[1, w0:w0 + window, :],
                    nat_ref[2, w0:w0 + window, :], q0 - w0, halo1)
        o, _ = _mix2(o, l, on_ref[q0:q0 + Q_BLOCK, :], ln_ref[q0:q0 + Q_BLOCK, :])
        o_ref[q0:q0 + Q_BLOCK, :] = o.astype(o_ref.dtype)


def _dilated_attn(proj, *, batch, seq):
    m = batch * seq
    head_spec = lambda h0: pl.BlockSpec((None, seq, HEAD_DIM), lambda b, h: (h0 + h, b, 0))
    return pl.pallas_call(
        functools.partial(_dilated_attn_kernel, seq=seq),
        grid=(batch, B_HEADS),
        in_specs=[head_spec(QB_HEAD0), head_spec(KB_HEAD0), head_spec(VB_HEAD0)],
        out_specs=pl.BlockSpec((seq, HEAD_DIM), lambda b, h: (b, h)),
        out_shape=jax.ShapeDtypeStruct((m, B_QKV), jnp.bfloat16),
        scratch_shapes=[pltpu.VMEM((3, seq, HEAD_DIM), jnp.bfloat16),
                        pltpu.VMEM((3, seq, HEAD_DIM), jnp.float32)]
                       + [pltpu.VMEM((seq, HEAD_DIM), jnp.float32)] * 4,
        compiler_params=pltpu.CompilerParams(
            dimension_semantics=("arbitrary", "arbitrary"), vmem_limit_bytes=VMEM_LIMIT),
        name="dilated_attn",
    )(proj, proj, proj)


def _layer_norm(y, g, b):
    mu = jnp.mean(y, axis=-1, keepdims=True)
    d = y - mu
    var = jnp.mean(d * d, axis=-1, keepdims=True)
    return d * lax.rsqrt(var + LN_EPS) * g + b


def _rms_norm(y, g):
    return y * lax.rsqrt(jnp.mean(y * y, axis=-1, keepdims=True) + RMS_EPS) * g


def _mix_ln1_kernel(oa_ref, ob_ref, x_ref, wo_ref, gna_ref, gnb_ref, g_ref, b_ref, h_ref):
    ya = _rms_norm(oa_ref[...].astype(jnp.float32), gna_ref[...])
    yb = _rms_norm(ob_ref[...].astype(jnp.float32), gnb_ref[...])
    mix = (jnp.dot(ya.astype(jnp.bfloat16), wo_ref[:A_Q, :], preferred_element_type=jnp.float32)
           + jnp.dot(yb.astype(jnp.bfloat16), wo_ref[A_Q:, :], preferred_element_type=jnp.float32))
    h_ref[...] = _layer_norm(ALPHA * x_ref[...] + mix, g_ref[...], b_ref[...])


def _mix_ln1(oa, ob, x2, wo_b, gn_a, gn_b, ln_g, ln_b):
    m = x2.shape[0]
    row = lambda w: pl.BlockSpec((MIX_TM, w), lambda i: (i, 0))
    full = lambda r, w: pl.BlockSpec((r, w), lambda i: (0, 0))
    return pl.pallas_call(
        _mix_ln1_kernel,
        grid=(m // MIX_TM,),
        in_specs=[row(A_Q), row(B_QKV), row(D_MODEL), full(D_MODEL, D_MODEL), full(1, A_Q),
                  full(1, B_QKV), full(1, D_MODEL), full(1, D_MODEL)],
        out_specs=row(D_MODEL),
        out_shape=jax.ShapeDtypeStruct((m, D_MODEL), jnp.float32),
        compiler_params=pltpu.CompilerParams(
            dimension_semantics=("arbitrary",), vmem_limit_bytes=VMEM_LIMIT),
        name="mix_ln1",
    )(oa, ob, x2, wo_b, gn_a, gn_b, ln_g, ln_b)


def _ffn_ln2_kernel(h_ref, w1g_ref, w2_ref, p_ref, wple_ref, g_ref, b_ref, o_ref, hb_ref, gate_ref):
    j = pl.program_id(1)
    n_ff = D_FF // FFN_TF
    n_all = (D_FF + D_MODEL) // FFN_TF
    n_sub = FFN_TF // FFN_SUB

    @pl.when(j == 0)
    def _():
        hb_ref[...] = h_ref[...].astype(jnp.bfloat16)
        o_ref[...] = jnp.zeros_like(o_ref)

    def up(c):
        return jnp.dot(hb_ref[...], w1g_ref[:, c * FFN_SUB:(c + 1) * FFN_SUB],
                       preferred_element_type=jnp.float32)

    @pl.when(j < n_ff)
    def _():
        acts = []
        for c in range(n_sub):
            r = jnp.maximum(up(c), 0.0)
            acts.append((r * r).astype(jnp.bfloat16))
        for n in range(D_MODEL // FFN_SUB):
            cols = slice(n * FFN_SUB, (n + 1) * FFN_SUB)
            part = sum(jnp.dot(acts[c], w2_ref[c * FFN_SUB:(c + 1) * FFN_SUB, cols],
                               preferred_element_type=jnp.float32) for c in range(n_sub))
            o_ref[:, cols] += part

    for t in range(n_all - n_ff):
        @pl.when(j == n_ff + t)
        def _():
            for c in range(n_sub):
                lo = t * FFN_TF + c * FFN_SUB
                gate_ref[:, lo:lo + FFN_SUB] = 1.0 / (1.0 + jnp.exp(-up(c)))

    @pl.when(j == n_all - 1)
    def _():
        ple = jnp.dot(p_ref[...].astype(jnp.bfloat16), wple_ref[...], preferred_element_type=jnp.float32)
        y = ALPHA * h_ref[...] + o_ref[...] + ple * gate_ref[...]
        o_ref[...] = _layer_norm(y, g_ref[...], b_ref[...])


def _ffn_ln2(h1, w1g_b, w2_b, p2, wple_b, ln_g, ln_b):
    m = h1.shape[0]
    n_ff = D_FF // FFN_TF
    n_all = (D_FF + D_MODEL) // FFN_TF
    return pl.pallas_call(
        _ffn_ln2_kernel,
        grid=(m // FFN_TM, n_all),
        in_specs=[pl.BlockSpec((FFN_TM, D_MODEL), lambda i, j: (i, 0)),
                  pl.BlockSpec((D_MODEL, FFN_TF), lambda i, j: (0, j)),
                  pl.BlockSpec((FFN_TF, D_MODEL), lambda i, j: (jnp.minimum(j, n_ff - 1), 0)),
                  pl.BlockSpec((FFN_TM, PLE_DIM), lambda i, j: (i, 0)),
                  pl.BlockSpec((PLE_DIM, D_MODEL), lambda i, j: (0, 0)),
                  pl.BlockSpec((1, D_MODEL), lambda i, j: (0, 0)),
                  pl.BlockSpec((1, D_MODEL), lambda i, j: (0, 0))],
        out_specs=pl.BlockSpec((FFN_TM, D_MODEL), lambda i, j: (i, 0)),
        out_shape=jax.ShapeDtypeStruct((m, D_MODEL), jnp.float32),
        scratch_shapes=[pltpu.VMEM((FFN_TM, D_MODEL), jnp.bfloat16),
                        pltpu.VMEM((FFN_TM, D_MODEL), jnp.float32)],
        compiler_params=pltpu.CompilerParams(
            dimension_semantics=("arbitrary", "arbitrary"), vmem_limit_bytes=VMEM_LIMIT),
        name="ffn_ln2",
    )(h1, w1g_b, w2_b, p2, wple_b, ln_g, ln_b)


def kernel(x, p, positions, w_in, sink_a, gn_a, gn_b, w_o, ln1_g, ln1_b, w1, w2, w_ple, w_ple_gate,
           ln2_g, ln2_b):
    batch, seq, d = x.shape
    m = batch * seq
    bf = jnp.bfloat16
    cos_t, sin_t = _rope_tables(positions)
    h = x.reshape(m, d)
    for i in range(DEPTH):
        proj = _in_proj(h, w_in[i].astype(bf), cos_t, sin_t)
        oa = _window_attn(proj, sink_a[i], batch=batch, seq=seq)
        ob = _dilated_attn(proj, batch=batch, seq=seq)
        h1 = _mix_ln1(oa, ob, h, w_o[i].astype(bf), gn_a[i].reshape(1, -1), gn_b[i].reshape(1, -1),
                      ln1_g[i].reshape(1, -1), ln1_b[i].reshape(1, -1))
        w1g = jnp.concatenate([w1[i], w_ple_gate[i]], axis=1).astype(bf)
        h = _ffn_ln2(h1, w1g, w2[i].astype(bf), p[i].reshape(m, PLE_DIM), w_ple[i].astype(bf),
                     ln2_g[i].reshape(1, -1), ln2_b[i].reshape(1, -1))
    return h.reshape(batch, seq, d)
```

```python
import functools

import jax
import jax.numpy as jnp
from jax import lax
from jax.experimental import pallas as pl
from jax.experimental.pallas import tpu as pltpu

D_MODEL = 2048
HEAD_DIM = 128
A_HEADS = 8
A_KV_HEADS = 2
A_WINDOW = 128
B_HEADS = 8
DILATED_PATTERNS = ((128, 1), (512, 4), (2048, 16))
ROT_DIM = HEAD_DIM // 4
ROPE_THETA = 500000.0
D_FF = 4 * D_MODEL
PLE_DIM = 256
DEPTH = 1
ALPHA = (2.0 * DEPTH) ** 0.25
LN_EPS = 1e-5
RMS_EPS = 1e-6
NEG_INF = -1e30

A_Q = A_HEADS * HEAD_DIM
A_KV = A_KV_HEADS * HEAD_DIM
B_QKV = B_HEADS * HEAD_DIM
D_IN = A_Q + 2 * A_KV + 3 * B_QKV
N_PROJ_HEADS = D_IN // HEAD_DIM
QA_HEAD0 = 0
KA_HEAD0 = A_HEADS
VA_HEAD0 = A_HEADS + A_KV_HEADS
QB_HEAD0 = A_HEADS + 2 * A_KV_HEADS
KB_HEAD0 = QB_HEAD0 + B_HEADS
VB_HEAD0 = KB_HEAD0 + B_HEADS

LANES = 128
MXU_COLS = 256
Q_BLOCK = 128
SCORE_LOOKAHEAD = 3
VMEM_LIMIT = 56 * 1024 * 1024

PROJ_TM = 1024
PROJ_TN = 1536
MIX_TM = 512
FFN_TM = 512
FFN_TF = 1024
FFN_SUB = 512
ROPE_TM = 2048
KIND_Q, KIND_K, KIND_V = 0, 1, 2


def _rope_table_kernel(pos_ref, freq_ref, cos_ref, sin_ref):
    ang = pos_ref[...].astype(jnp.float32) * freq_ref[...]
    lane = lax.broadcasted_iota(jnp.int32, ang.shape, 1)
    half = ROT_DIM // 2
    c = jnp.where(lane < ROT_DIM, jnp.cos(ang), 1.0)
    s = jnp.sin(ang)
    s = jnp.where(lane < half, -s, jnp.where(lane < ROT_DIM, s, 0.0))
    q_scale = HEAD_DIM ** -0.5
    cos_ref[KIND_Q] = c * q_scale
    sin_ref[KIND_Q] = s * q_scale
    cos_ref[KIND_K] = c
    sin_ref[KIND_K] = s
    cos_ref[KIND_V] = jnp.ones_like(c)
    sin_ref[KIND_V] = jnp.zeros_like(s)


def _rope_tables(positions):
    m = positions.size
    half = ROT_DIM // 2
    inv_freq = ROPE_THETA ** (-jnp.arange(0, ROT_DIM, 2, dtype=jnp.float32) / ROT_DIM)
    freq = jnp.tile(inv_freq, LANES // half).reshape(1, LANES)
    pos = positions.reshape(m, 1)
    out = jax.ShapeDtypeStruct((3, m, LANES), jnp.float32)
    return pl.pallas_call(
        _rope_table_kernel,
        grid=(m // ROPE_TM,),
        in_specs=[pl.BlockSpec((ROPE_TM, 1), lambda i: (i, 0)),
                  pl.BlockSpec((1, LANES), lambda i: (0, 0))],
        out_specs=[pl.BlockSpec((3, ROPE_TM, LANES), lambda i: (0, i, 0))] * 2,
        out_shape=[out, out],
        name="rope_table",
    )(pos, freq)


def _in_proj_kernel(x_ref, w_ref, cos_ref, sin_ref, o_ref, xb_ref):
    j = pl.program_id(1)

    @pl.when(j == 0)
    def _():
        xb_ref[...] = x_ref[...].astype(jnp.bfloat16)

    half = ROT_DIM // 2
    lane = lax.broadcasted_iota(jnp.int32, (PROJ_TM, LANES), 1)
    heads_per_tile = PROJ_TN // HEAD_DIM
    heads_per_chunk = MXU_COLS // HEAD_DIM
    for c in range(PROJ_TN // MXU_COLS):
        acc = jnp.dot(xb_ref[...], w_ref[:, c * MXU_COLS:(c + 1) * MXU_COLS],
                      preferred_element_type=jnp.float32)
        for g in range(heads_per_chunk):
            local = c * heads_per_chunk + g
            head = j * heads_per_tile + local
            is_q = (head < KA_HEAD0) | ((head >= QB_HEAD0) & (head < KB_HEAD0))
            is_k = ((head >= KA_HEAD0) & (head < VA_HEAD0)) | ((head >= KB_HEAD0) & (head < VB_HEAD0))
            kind = jnp.where(is_q, KIND_Q, jnp.where(is_k, KIND_K, KIND_V))
            t = acc[:, g * HEAD_DIM:(g + 1) * HEAD_DIM]
            swapped = jnp.where(lane < half,
                                pltpu.roll(t, LANES - half, 1),
                                pltpu.roll(t, half, 1))
            o_ref[local] = t * cos_ref[kind] + swapped * sin_ref[kind]


def _in_proj(x2, w_in_b, cos_t, sin_t):
    m = x2.shape[0]
    heads_per_tile = PROJ_TN // HEAD_DIM
    return pl.pallas_call(
        _in_proj_kernel,
        grid=(m // PROJ_TM, D_IN // PROJ_TN),
        in_specs=[pl.BlockSpec((PROJ_TM, D_MODEL), lambda i, j: (i, 0)),
                  pl.BlockSpec((D_MODEL, PROJ_TN), lambda i, j: (0, j)),
                  pl.BlockSpec((3, PROJ_TM, LANES), lambda i, j: (0, i, 0)),
                  pl.BlockSpec((3, PROJ_TM, LANES), lambda i, j: (0, i, 0))],
        out_specs=pl.BlockSpec((heads_per_tile, PROJ_TM, HEAD_DIM), lambda i, j: (j, i, 0)),
        out_shape=jax.ShapeDtypeStruct((N_PROJ_HEADS, m, HEAD_DIM), jnp.float32),
        scratch_shapes=[pltpu.VMEM((PROJ_TM, D_MODEL), jnp.bfloat16)],
        compiler_params=pltpu.CompilerParams(
            dimension_semantics=("arbitrary", "arbitrary"), vmem_limit_bytes=VMEM_LIMIT),
        name="in_proj",
    )(x2, w_in_b, cos_t, sin_t)


def _band_bias(delta, rows, cols, halo):
    d = delta + lax.broadcasted_iota(jnp.int32, (rows, cols), 0) - lax.broadcasted_iota(jnp.int32, (rows, cols), 1)
    return jnp.where(jnp.abs(d) <= halo, 0.0, NEG_INF).astype(jnp.float32)


def _softmax_pv(s, v, sink=None):
    m = jnp.max(s, axis=-1, keepdims=True)
    if sink is not None:
        m = jnp.maximum(m, sink)
    e = jnp.exp(s - m)
    denom = jnp.sum(e, axis=-1, keepdims=True)
    if sink is not None:
        denom = denom + jnp.exp(sink - m)
    o = jnp.dot(e.astype(v.dtype), v, preferred_element_type=jnp.float32) / denom
    return o, m, denom


def _qk(q, k):
    return lax.dot_general(q, k, (((1,), (1,)), ((), ())), preferred_element_type=jnp.float32)


def _window_start(q0, halo, window, length):
    return min(max(q0 - halo, 0), length - window)


def _run_pipelined(n, scores, finish, depth=SCORE_LOOKAHEAD):
    pending = {}
    for j in range(n + depth):
        if j < n:
            pending[j] = scores(j)
        if j >= depth:
            finish(j - depth, pending.pop(j - depth))


def _window_attn_kernel(q_ref, k_ref, v_ref, sink_ref, o_ref, qkv_ref, *, seq):
    window = Q_BLOCK + 2 * A_WINDOW
    head = pl.program_id(1)
    sink = sink_ref[head]
    qkv_ref[0] = q_ref[...].astype(jnp.bfloat16)

    @pl.when(head % (A_HEADS // A_KV_HEADS) == 0)
    def _():
        qkv_ref[1] = k_ref[...].astype(jnp.bfloat16)
        qkv_ref[2] = v_ref[...].astype(jnp.bfloat16)

    biases = {}

    def scores(j):
        q0 = j * Q_BLOCK
        w0 = _window_start(q0, A_WINDOW, window, seq)
        if q0 - w0 not in biases:
            biases[q0 - w0] = _band_bias(q0 - w0, Q_BLOCK, window, A_WINDOW)
        return _qk(qkv_ref[0, q0:q0 + Q_BLOCK, :], qkv_ref[1, w0:w0 + window, :]) + biases[q0 - w0]

    def finish(j, s):
        q0 = j * Q_BLOCK
        w0 = _window_start(q0, A_WINDOW, window, seq)
        o, _, _ = _softmax_pv(s, qkv_ref[2, w0:w0 + window, :], sink)
        o_ref[q0:q0 + Q_BLOCK, :] = o.astype(o_ref.dtype)

    _run_pipelined(seq // Q_BLOCK, scores, finish)


def _window_attn(proj, sink, *, batch, seq):
    m = batch * seq
    grp = A_HEADS // A_KV_HEADS
    head_spec = lambda h0, div: pl.BlockSpec((None, seq, HEAD_DIM), lambda b, h: (h0 + h // div, b, 0))
    return pl.pallas_call(
        functools.partial(_window_attn_kernel, seq=seq),
        grid=(batch, A_HEADS),
        in_specs=[head_spec(QA_HEAD0, 1), head_spec(KA_HEAD0, grp), head_spec(VA_HEAD0, grp),
                  pl.BlockSpec(memory_space=pltpu.SMEM)],
        out_specs=pl.BlockSpec((seq, HEAD_DIM), lambda b, h: (b, h)),
        out_shape=jax.ShapeDtypeStruct((m, A_Q), jnp.bfloat16),
        scratch_shapes=[pltpu.VMEM((3, seq, HEAD_DIM), jnp.bfloat16)],
        compiler_params=pltpu.CompilerParams(
            dimension_semantics=("arbitrary", "arbitrary"), vmem_limit_bytes=VMEM_LIMIT),
        name="window_attn",
    )(proj, proj, proj, sink)


def _mix2(oa, la, ob, lb):
    lmax = jnp.maximum(la, lb)
    ea = jnp.exp(la - lmax)
    eb = jnp.exp(lb - lmax)
    tot = ea + eb
    return (ea * oa + eb * ob) / tot, lmax + jnp.log(tot)


def _dilated_attn_kernel(q_ref, k_ref, v_ref, o_ref, nat_ref, ca_ref, cab_ref, cbb_ref,
                         pb_ref, pc_ref, pa_ref, pn_ref, *, seq):
    bf = jnp.bfloat16
    (w1, r1), (wa, ra), (wb, rb) = sorted(DILATED_PATTERNS, key=lambda pat: pat[1])
    assert r1 == 1 and rb % ra == 0
    sub = rb // ra
    len_a, len_b = seq // ra, seq // rb
    halo1, halo_a, halo_b = w1 // 2, wa // (2 * ra), wb // (2 * rb)
    biases = {}

    def band_scores(src_ref, base, q0, w0, window, halo):
        key = (q0 - w0, Q_BLOCK, window, halo)
        if key not in biases:
            biases[key] = _band_bias(*key)
        return _qk(src_ref[0, base + q0:base + q0 + Q_BLOCK, :],
                   src_ref[1, base + w0:base + w0 + window, :]) + biases[key]

    def band_output(s, src_ref, base, w0, window):
        o, m, denom = _softmax_pv(s, src_ref[2, base + w0:base + w0 + window, :])
        return o, jnp.broadcast_to(m + jnp.log(denom), (Q_BLOCK, LANES))

    def blocks(length, halo):
        window = min(Q_BLOCK + 2 * halo, length)
        for j in range(length // Q_BLOCK):
            q0 = j * Q_BLOCK
            yield q0, _window_start(q0, halo, window, length), window

    for t, src in enumerate((q_ref, k_ref, v_ref)):
        nat_ref[t] = src[...].astype(bf)
        for c in range(ra):
            x = src[pl.ds(c, len_a, stride=ra), :]
            ca_ref[t, c * len_a:(c + 1) * len_a, :] = x
            cab_ref[t, c * len_a:(c + 1) * len_a, :] = x.astype(bf)
    for t in range(3):
        for c in range(ra):
            for u in range(sub):
                cu = c * sub + u
                cbb_ref[t, cu * len_b:(cu + 1) * len_b, :] = (
                    ca_ref[t, pl.ds(c * len_a + u, len_b, stride=sub), :].astype(bf))

    units = []

    def add_unit(src_ref, base, q0, w0, window, halo, finish):
        units.append((lambda: band_scores(src_ref, base, q0, w0, window, halo),
                      lambda s: finish(*band_output(s, src_ref, base, w0, window))))

    for c in range(ra):
        for u in range(sub):
            cu = c * sub + u
            todo = list(blocks(len_b, halo_b))
            for q0, w0, window in todo:
                def finish_b(o, l, cu=cu, c=c, q0=q0, last=(u == sub - 1 and q0 == todo[-1][0])):
                    pb_ref[0, cu * len_b + q0:cu * len_b + q0 + Q_BLOCK, :] = o
                    pb_ref[1, cu * len_b + q0:cu * len_b + q0 + Q_BLOCK, :] = l
                    if last:
                        for uu in range(sub):
                            for t in range(2):
                                pc_ref[t, pl.ds(c * len_a + uu, len_b, stride=sub), :] = (
                                    pb_ref[t, (c * sub + uu) * len_b:(c * sub + uu + 1) * len_b, :])
                add_unit(cbb_ref, cu * len_b, q0, w0, window, halo_b, finish_b)

    for c in range(ra):
        todo = list(blocks(len_a, halo_a))
        for q0, w0, window in todo:
            def finish_a(o, l, c=c, q0=q0, last=(q0 == todo[-1][0])):
                rows = slice(c * len_a + q0, c * len_a + q0 + Q_BLOCK)
                o, l = _mix2(o, l, pc_ref[0, rows, :], pc_ref[1, rows, :])
                pa_ref[0, rows, :] = o
                pa_ref[1, rows, :] = l
                if last:
                    for t in range(2):
                        pn_ref[t, pl.ds(c, len_a, stride=ra), :] = pa_ref[t, c * len_a:(c + 1) * len_a, :]
            add_unit(cab_ref, c * len_a, q0, w0, window, halo_a, finish_a)

    for q0, w0, window in blocks(seq, halo1):
        def finish_1(o, l, q0=q0):
            o, _ = _mix2(o, l, pn_ref[0, q0:q0 + Q_BLOCK, :], pn_ref[1, q0:q0 + Q_BLOCK, :])
            o_ref[q0:q0 + Q_BLOCK, :] = o.astype(o_ref.dtype)
        add_unit(nat_ref, 0, q0, w0, window, halo1, finish_1)

    _run_pipelined(len(units), lambda j: units[j][0](), lambda j, s: units[j][1](s))


def _dilated_attn(proj, *, batch, seq):
    m = batch * seq
    head_spec = lambda h0: pl.BlockSpec((None, seq, HEAD_DIM), lambda b, h: (h0 + h, b, 0))
    return pl.pallas_call(
        functools.partial(_dilated_attn_kernel, seq=seq),
        grid=(batch, B_HEADS),
        in_specs=[head_spec(QB_HEAD0), head_spec(KB_HEAD0), head_spec(VB_HEAD0)],
        out_specs=pl.BlockSpec((seq, HEAD_DIM), lambda b, h: (b, h)),
        out_shape=jax.ShapeDtypeStruct((m, B_QKV), jnp.bfloat16),
        scratch_shapes=[pltpu.VMEM((3, seq, HEAD_DIM), jnp.bfloat16),
                        pltpu.VMEM((3, seq, HEAD_DIM), jnp.float32),
                        pltpu.VMEM((3, seq, HEAD_DIM), jnp.bfloat16),
                        pltpu.VMEM((3, seq, HEAD_DIM), jnp.bfloat16)]
                       + [pltpu.VMEM((2, seq, HEAD_DIM), jnp.float32)] * 4,
        compiler_params=pltpu.CompilerParams(
            dimension_semantics=("arbitrary", "arbitrary"), vmem_limit_bytes=VMEM_LIMIT),
        name="dilated_attn",
    )(proj, proj, proj)


def _layer_norm(y, g, b):
    mu = jnp.mean(y, axis=-1, keepdims=True)
    d = y - mu
    var = jnp.mean(d * d, axis=-1, keepdims=True)
    return d * lax.rsqrt(var + LN_EPS) * g + b


def _rms_norm(y, g):
    return y * lax.rsqrt(jnp.mean(y * y, axis=-1, keepdims=True) + RMS_EPS) * g


def _mix_ln1_kernel(oa_ref, ob_ref, x_ref, wo_ref, gna_ref, gnb_ref, g_ref, b_ref, h_ref):
    ya = _rms_norm(oa_ref[...].astype(jnp.float32), gna_ref[...])
    yb = _rms_norm(ob_ref[...].astype(jnp.float32), gnb_ref[...])
    mix = (jnp.dot(ya.astype(jnp.bfloat16), wo_ref[:A_Q, :], preferred_element_type=jnp.float32)
           + jnp.dot(yb.astype(jnp.bfloat16), wo_ref[A_Q:, :], preferred_element_type=jnp.float32))
    h_ref[...] = _layer_norm(ALPHA * x_ref[...] + mix, g_ref[...], b_ref[...])


def _mix_ln1(oa, ob, x2, wo_b, gn_a, gn_b, ln_g, ln_b):
    m = x2.shape[0]
    row = lambda w: pl.BlockSpec((MIX_TM, w), lambda i: (i, 0))
    full = lambda r, w: pl.BlockSpec((r, w), lambda i: (0, 0))
    return pl.pallas_call(
        _mix_ln1_kernel,
        grid=(m // MIX_TM,),
        in_specs=[row(A_Q), row(B_QKV), row(D_MODEL), full(D_MODEL, D_MODEL), full(1, A_Q),
                  full(1, B_QKV), full(1, D_MODEL), full(1, D_MODEL)],
        out_specs=row(D_MODEL),
        out_shape=jax.ShapeDtypeStruct((m, D_MODEL), jnp.float32),
        compiler_params=pltpu.CompilerParams(
            dimension_semantics=("arbitrary",), vmem_limit_bytes=VMEM_LIMIT),
        name="mix_ln1",
    )(oa, ob, x2, wo_b, gn_a, gn_b, ln_g, ln_b)


def _ffn_ln2_kernel(h_ref, w1g_ref, w2_ref, p_ref, wple_ref, g_ref, b_ref, o_ref, hb_ref, gate_ref):
    j = pl.program_id(1)
    n_ff = D_FF // FFN_TF
    n_all = (D_FF + D_MODEL) // FFN_TF
    n_sub = FFN_TF // FFN_SUB

    @pl.when(j == 0)
    def _():
        hb_ref[...] = h_ref[...].astype(jnp.bfloat16)
        o_ref[...] = jnp.zeros_like(o_ref)

    def up(c):
        return jnp.dot(hb_ref[...], w1g_ref[:, c * FFN_SUB:(c + 1) * FFN_SUB],
                       preferred_element_type=jnp.float32)

    @pl.when(j < n_ff)
    def _():
        acts = []
        for c in range(n_sub):
            r = jnp.maximum(up(c), 0.0)
            acts.append((r * r).astype(jnp.bfloat16))
        for n in range(D_MODEL // FFN_SUB):
            cols = slice(n * FFN_SUB, (n + 1) * FFN_SUB)
            part = sum(jnp.dot(acts[c], w2_ref[c * FFN_SUB:(c + 1) * FFN_SUB, cols],
                               preferred_element_type=jnp.float32) for c in range(n_sub))
            o_ref[:, cols] += part

    for t in range(n_all - n_ff):
        @pl.when(j == n_ff + t)
        def _():
            for c in range(n_sub):
                lo = t * FFN_TF + c * FFN_SUB
                gate_ref[:, lo:lo + FFN_SUB] = 1.0 / (1.0 + jnp.exp(-up(c)))

    @pl.when(j == n_all - 1)
    def _():
        ple = jnp.dot(p_ref[...].astype(jnp.bfloat16), wple_ref[...], preferred_element_type=jnp.float32)
        y = ALPHA * h_ref[...] + o_ref[...] + ple * gate_ref[...]
        o_ref[...] = _layer_norm(y, g_ref[...], b_ref[...])


def _ffn_ln2(h1, w1g_b, w2_b, p2, wple_b, ln_g, ln_b):
    m = h1.shape[0]
    n_ff = D_FF // FFN_TF
    n_all = (D_FF + D_MODEL) // FFN_TF
    return pl.pallas_call(
        _ffn_ln2_kernel,
        grid=(m // FFN_TM, n_all),
        in_specs=[pl.BlockSpec((FFN_TM, D_MODEL), lambda i, j: (i, 0)),
                  pl.BlockSpec((D_MODEL, FFN_TF), lambda i, j: (0, j)),
                  pl.BlockSpec((FFN_TF, D_MODEL), lambda i, j: (jnp.minimum(j, n_ff - 1), 0)),
                  pl.BlockSpec((FFN_TM, PLE_DIM), lambda i, j: (i, 0)),
                  pl.BlockSpec((PLE_DIM, D_MODEL), lambda i, j: (0, 0)),
                  pl.BlockSpec((1, D_MODEL), lambda i, j: (0, 0)),
                  pl.BlockSpec((1, D_MODEL), lambda i, j: (0, 0))],
        out_specs=pl.BlockSpec((FFN_TM, D_MODEL), lambda i, j: (i, 0)),
        out_shape=jax.ShapeDtypeStruct((m, D_MODEL), jnp.float32),
        scratch_shapes=[pltpu.VMEM((FFN_TM, D_MODEL), jnp.bfloat16),
                        pltpu.VMEM((FFN_TM, D_MODEL), jnp.float32)],
        compiler_params=pltpu.CompilerParams(
            dimension_semantics=("arbitrary", "arbitrary"), vmem_limit_bytes=VMEM_LIMIT),
        name="ffn_ln2",
    )(h1, w1g_b, w2_b, p2, wple_b, ln_g, ln_b)


def kernel(x, p, positions, w_in, sink_a, gn_a, gn_b, w_o, ln1_g, ln1_b, w1, w2, w_ple, w_ple_gate,
           ln2_g, ln2_b):
    batch, seq, d = x.shape
    m = batch * seq
    bf = jnp.bfloat16
    cos_t, sin_t = _rope_tables(positions)
    h = x.reshape(m, d)
    for i in range(DEPTH):
        proj = _in_proj(h, w_in[i].astype(bf), cos_t, sin_t)
        oa = _window_attn(proj, sink_a[i], batch=batch, seq=seq)
        ob = _dilated_attn(proj, batch=batch, seq=seq)
        h1 = _mix_ln1(oa, ob, h, w_o[i].astype(bf), gn_a[i].reshape(1, -1), gn_b[i].reshape(1, -1),
                      ln1_g[i].reshape(1, -1), ln1_b[i].reshape(1, -1))
        w1g = jnp.concatenate([w1[i], w_ple_gate[i]], axis=1).astype(bf)
        h = _ffn_ln2(h1, w1g, w2[i].astype(bf), p[i].reshape(m, PLE_DIM), w_ple[i].astype(bf),
                     ln2_g[i].reshape(1, -1), ln2_b[i].reshape(1, -1))
    return h.reshape(batch, seq, d)
```

```python
import functools

import jax
import jax.numpy as jnp
from jax import lax
from jax.experimental import pallas as pl
from jax.experimental.pallas import tpu as pltpu

D_MODEL = 2048
HEAD_DIM = 128
A_HEADS = 8
A_KV_HEADS = 2
A_WINDOW = 128
B_HEADS = 8
DILATED_PATTERNS = ((128, 1), (512, 4), (2048, 16))
ROT_DIM = HEAD_DIM // 4
ROPE_THETA = 500000.0
D_FF = 4 * D_MODEL
PLE_DIM = 256
DEPTH = 1
ALPHA = (2.0 * DEPTH) ** 0.25
LN_EPS = 1e-5
RMS_EPS = 1e-6
NEG_INF = -1e30
LOG2_E = 1.4426950408889634

A_Q = A_HEADS * HEAD_DIM
A_KV = A_KV_HEADS * HEAD_DIM
B_QKV = B_HEADS * HEAD_DIM
D_IN = A_Q + 2 * A_KV + 3 * B_QKV
N_PROJ_HEADS = D_IN // HEAD_DIM
QA_HEAD0 = 0
KA_HEAD0 = A_HEADS
VA_HEAD0 = A_HEADS + A_KV_HEADS
QB_HEAD0 = A_HEADS + 2 * A_KV_HEADS
KB_HEAD0 = QB_HEAD0 + B_HEADS
VB_HEAD0 = KB_HEAD0 + B_HEADS

LANES = 128
MXU_COLS = 256
Q_BLOCK = 128
SCORE_LOOKAHEAD = 5
VMEM_LIMIT = 56 * 1024 * 1024

PROJ_TM = 1024
PROJ_TN = 1536
MIX_TM = 512
FFN_TM = 512
FFN_TF = 1024
FFN_SUB = 512
ROPE_TM = 2048
KIND_Q, KIND_K, KIND_V = 0, 1, 2


def _rope_table_kernel(pos_ref, freq_ref, cos_ref, sin_ref):
    ang = pos_ref[...].astype(jnp.float32) * freq_ref[...]
    lane = lax.broadcasted_iota(jnp.int32, ang.shape, 1)
    half = ROT_DIM // 2
    c = jnp.where(lane < ROT_DIM, jnp.cos(ang), 1.0)
    s = jnp.sin(ang)
    s = jnp.where(lane < half, -s, jnp.where(lane < ROT_DIM, s, 0.0))
    q_scale = LOG2_E * HEAD_DIM ** -0.5
    cos_ref[KIND_Q] = c * q_scale
    sin_ref[KIND_Q] = s * q_scale
    cos_ref[KIND_K] = c
    sin_ref[KIND_K] = s
    cos_ref[KIND_V] = jnp.ones_like(c)
    sin_ref[KIND_V] = jnp.zeros_like(s)


def _rope_tables(positions):
    m = positions.size
    half = ROT_DIM // 2
    inv_freq = ROPE_THETA ** (-jnp.arange(0, ROT_DIM, 2, dtype=jnp.float32) / ROT_DIM)
    freq = jnp.tile(inv_freq, LANES // half).reshape(1, LANES)
    pos = positions.reshape(m, 1)
    out = jax.ShapeDtypeStruct((3, m, LANES), jnp.float32)
    return pl.pallas_call(
        _rope_table_kernel,
        grid=(m // ROPE_TM,),
        in_specs=[pl.BlockSpec((ROPE_TM, 1), lambda i: (i, 0)),
                  pl.BlockSpec((1, LANES), lambda i: (0, 0))],
        out_specs=[pl.BlockSpec((3, ROPE_TM, LANES), lambda i: (0, i, 0))] * 2,
        out_shape=[out, out],
        name="rope_table",
    )(pos, freq)


def _in_proj_kernel(x_ref, w_ref, cos_ref, sin_ref, o_ref, xb_ref):
    j = pl.program_id(1)

    @pl.when(j == 0)
    def _():
        xb_ref[...] = x_ref[...].astype(jnp.bfloat16)

    half = ROT_DIM // 2
    lane = lax.broadcasted_iota(jnp.int32, (PROJ_TM, LANES), 1)
    partner = jnp.where(lane < half, lane + half, jnp.where(lane < ROT_DIM, lane - half, lane))
    heads_per_tile = PROJ_TN // HEAD_DIM
    heads_per_chunk = MXU_COLS // HEAD_DIM
    for c in range(PROJ_TN // MXU_COLS):
        acc = jnp.dot(xb_ref[...], w_ref[:, c * MXU_COLS:(c + 1) * MXU_COLS],
                      preferred_element_type=jnp.float32)
        for g in range(heads_per_chunk):
            local = c * heads_per_chunk + g
            head = j * heads_per_tile + local
            is_q = (head < KA_HEAD0) | ((head >= QB_HEAD0) & (head < KB_HEAD0))
            is_k = ((head >= KA_HEAD0) & (head < VA_HEAD0)) | ((head >= KB_HEAD0) & (head < VB_HEAD0))
            kind = jnp.where(is_q, KIND_Q, jnp.where(is_k, KIND_K, KIND_V))
            t = acc[:, g * HEAD_DIM:(g + 1) * HEAD_DIM]
            swapped = jnp.take_along_axis(t, partner, axis=1)
            o_ref[local] = t * cos_ref[kind] + swapped * sin_ref[kind]


def _in_proj(x2, w_in_b, cos_t, sin_t):
    m = x2.shape[0]
    heads_per_tile = PROJ_TN // HEAD_DIM
    return pl.pallas_call(
        _in_proj_kernel,
        grid=(m // PROJ_TM, D_IN // PROJ_TN),
        in_specs=[pl.BlockSpec((PROJ_TM, D_MODEL), lambda i, j: (i, 0)),
                  pl.BlockSpec((D_MODEL, PROJ_TN), lambda i, j: (0, j)),
                  pl.BlockSpec((3, PROJ_TM, LANES), lambda i, j: (0, i, 0)),
                  pl.BlockSpec((3, PROJ_TM, LANES), lambda i, j: (0, i, 0))],
        out_specs=pl.BlockSpec((heads_per_tile, PROJ_TM, HEAD_DIM), lambda i, j: (j, i, 0)),
        out_shape=jax.ShapeDtypeStruct((N_PROJ_HEADS, m, HEAD_DIM), jnp.float32),
        scratch_shapes=[pltpu.VMEM((PROJ_TM, D_MODEL), jnp.bfloat16)],
        compiler_params=pltpu.CompilerParams(
            dimension_semantics=("arbitrary", "arbitrary"), vmem_limit_bytes=VMEM_LIMIT),
        name="in_proj",
    )(x2, w_in_b, cos_t, sin_t)


def _band_bias(delta, rows, cols, halo):
    d = delta + lax.broadcasted_iota(jnp.int32, (rows, cols), 0) - lax.broadcasted_iota(jnp.int32, (rows, cols), 1)
    return jnp.where(jnp.abs(d) <= halo, 0.0, NEG_INF).astype(jnp.float32)


def _softmax_pv(s, v, sink=None):
    m = jnp.max(s, axis=-1, keepdims=True)
    if sink is not None:
        m = jnp.maximum(m, sink)
    e = jnp.exp2(s - m)
    denom = jnp.sum(e, axis=-1, keepdims=True)
    if sink is not None:
        denom = denom + jnp.exp2(sink - m)
    o = jnp.dot(e.astype(v.dtype), v, preferred_element_type=jnp.float32) / denom
    return o, m, denom


def _qk(q, k):
    return lax.dot_general(q, k, (((1,), (1,)), ((), ())), preferred_element_type=jnp.float32)


def _window_start(q0, halo, window, length):
    return min(max(q0 - halo, 0), length - window)


def _run_pipelined(n, scores, finish, depth=SCORE_LOOKAHEAD):
    pending = {}
    for j in range(n + depth):
        if j < n:
            pending[j] = scores(j)
        if j >= depth:
            finish(j - depth, pending.pop(j - depth))


def _window_attn_kernel(q_ref, k_ref, v_ref, sink_ref, o_ref, qkv_ref, *, seq):
    window = Q_BLOCK + 2 * A_WINDOW
    head = pl.program_id(1)
    sink = sink_ref[head] * LOG2_E
    qkv_ref[0] = q_ref[...].astype(jnp.bfloat16)

    @pl.when(head % (A_HEADS // A_KV_HEADS) == 0)
    def _():
        qkv_ref[1] = k_ref[...].astype(jnp.bfloat16)
        qkv_ref[2] = v_ref[...].astype(jnp.bfloat16)

    biases = {}

    def scores(j):
        q0 = j * Q_BLOCK
        w0 = _window_start(q0, A_WINDOW, window, seq)
        if q0 - w0 not in biases:
            biases[q0 - w0] = _band_bias(q0 - w0, Q_BLOCK, window, A_WINDOW)
        return _qk(qkv_ref[0, q0:q0 + Q_BLOCK, :], qkv_ref[1, w0:w0 + window, :]) + biases[q0 - w0]

    def finish(j, s):
        q0 = j * Q_BLOCK
        w0 = _window_start(q0, A_WINDOW, window, seq)
        o, _, _ = _softmax_pv(s, qkv_ref[2, w0:w0 + window, :], sink)
        o_ref[q0:q0 + Q_BLOCK, :] = o.astype(o_ref.dtype)

    _run_pipelined(seq // Q_BLOCK, scores, finish)


def _window_attn(proj, sink, *, batch, seq):
    m = batch * seq
    grp = A_HEADS // A_KV_HEADS
    head_spec = lambda h0, div: pl.BlockSpec((None, seq, HEAD_DIM), lambda b, h: (h0 + h // div, b, 0))
    return pl.pallas_call(
        functools.partial(_window_attn_kernel, seq=seq),
        grid=(batch, A_HEADS),
        in_specs=[head_spec(QA_HEAD0, 1), head_spec(KA_HEAD0, grp), head_spec(VA_HEAD0, grp),
                  pl.BlockSpec(memory_space=pltpu.SMEM)],
        out_specs=pl.BlockSpec((seq, HEAD_DIM), lambda b, h: (b, h)),
        out_shape=jax.ShapeDtypeStruct((m, A_Q), jnp.bfloat16),
        scratch_shapes=[pltpu.VMEM((3, seq, HEAD_DIM), jnp.bfloat16)],
        compiler_params=pltpu.CompilerParams(
            dimension_semantics=("arbitrary", "arbitrary"), vmem_limit_bytes=VMEM_LIMIT),
        name="window_attn",
    )(proj, proj, proj, sink)


def _mix2(oa, la, ob, lb):
    lmax = jnp.maximum(la, lb)
    ea = jnp.exp2(la - lmax)
    eb = jnp.exp2(lb - lmax)
    tot = ea + eb
    return (ea * oa + eb * ob) / tot, lmax + jnp.log2(tot)


def _dilated_attn_kernel(q_ref, k_ref, v_ref, o_ref, nat_ref, ca_ref, cab_ref, cbb_ref,
                         pb_ref, pc_ref, pa_ref, pn_ref, *, seq):
    bf = jnp.bfloat16
    (w1, r1), (wa, ra), (wb, rb) = sorted(DILATED_PATTERNS, key=lambda pat: pat[1])
    assert r1 == 1 and rb % ra == 0
    sub = rb // ra
    len_a, len_b = seq // ra, seq // rb
    halo1, halo_a, halo_b = w1 // 2, wa // (2 * ra), wb // (2 * rb)
    biases = {}

    def band_scores(src_ref, base, q0, w0, window, halo):
        key = (q0 - w0, Q_BLOCK, window, halo)
        if key not in biases:
            biases[key] = _band_bias(*key)
        return _qk(src_ref[0, base + q0:base + q0 + Q_BLOCK, :],
                   src_ref[1, base + w0:base + w0 + window, :]) + biases[key]

    def band_output(s, src_ref, base, w0, window):
        o, m, denom = _softmax_pv(s, src_ref[2, base + w0:base + w0 + window, :])
        return o, jnp.broadcast_to(m + jnp.log2(denom), (Q_BLOCK, LANES))

    def blocks(length, halo):
        window = min(Q_BLOCK + 2 * halo, length)
        for j in range(length // Q_BLOCK):
            q0 = j * Q_BLOCK
            yield q0, _window_start(q0, halo, window, length), window

    for t, src in enumerate((q_ref, k_ref, v_ref)):
        nat_ref[t] = src[...].astype(bf)
        for c in range(ra):
            x = src[pl.ds(c, len_a, stride=ra), :]
            ca_ref[t, c * len_a:(c + 1) * len_a, :] = x
            cab_ref[t, c * len_a:(c + 1) * len_a, :] = x.astype(bf)
    for t in range(3):
        for c in range(ra):
            for u in range(sub):
                cu = c * sub + u
                cbb_ref[t, cu * len_b:(cu + 1) * len_b, :] = (
                    ca_ref[t, pl.ds(c * len_a + u, len_b, stride=sub), :].astype(bf))

    units = []

    def add_unit(src_ref, base, q0, w0, window, halo, finish):
        units.append((lambda: band_scores(src_ref, base, q0, w0, window, halo),
                      lambda s: finish(*band_output(s, src_ref, base, w0, window))))

    for c in range(ra):
        for u in range(sub):
            cu = c * sub + u
            todo = list(blocks(len_b, halo_b))
            for q0, w0, window in todo:
                def finish_b(o, l, cu=cu, c=c, q0=q0, last=(u == sub - 1 and q0 == todo[-1][0])):
                    pb_ref[0, cu * len_b + q0:cu * len_b + q0 + Q_BLOCK, :] = o
                    pb_ref[1, cu * len_b + q0:cu * len_b + q0 + Q_BLOCK, :] = l
                    if last:
                        for uu in range(sub):
                            for t in range(2):
                                pc_ref[t, pl.ds(c * len_a + uu, len_b, stride=sub), :] = (
                                    pb_ref[t, (c * sub + uu) * len_b:(c * sub + uu + 1) * len_b, :])
                add_unit(cbb_ref, cu * len_b, q0, w0, window, halo_b, finish_b)

    for c in range(ra):
        todo = list(blocks(len_a, halo_a))
        for q0, w0, window in todo:
            def finish_a(o, l, c=c, q0=q0, last=(q0 == todo[-1][0])):
                rows = slice(c * len_a + q0, c * len_a + q0 + Q_BLOCK)
                o, l = _mix2(o, l, pc_ref[0, rows, :], pc_ref[1, rows, :])
                pa_ref[0, rows, :] = o
                pa_ref[1, rows, :] = l
                if last:
                    for t in range(2):
                        pn_ref[t, pl.ds(c, len_a, stride=ra), :] = pa_ref[t, c * len_a:(c + 1) * len_a, :]
            add_unit(cab_ref, c * len_a, q0, w0, window, halo_a, finish_a)

    for q0, w0, window in blocks(seq, halo1):
        def finish_1(o, l, q0=q0):
            o, _ = _mix2(o, l, pn_ref[0, q0:q0 + Q_BLOCK, :], pn_ref[1, q0:q0 + Q_BLOCK, :])
            o_ref[q0:q0 + Q_BLOCK, :] = o.astype(o_ref.dtype)
        add_unit(nat_ref, 0, q0, w0, window, halo1, finish_1)

    _run_pipelined(len(units), lambda j: units[j][0](), lambda j, s: units[j][1](s))


def _dilated_attn(proj, *, batch, seq):
    m = batch * seq
    head_spec = lambda h0: pl.BlockSpec((None, seq, HEAD_DIM), lambda b, h: (h0 + h, b, 0))
    return pl.pallas_call(
        functools.partial(_dilated_attn_kernel, seq=seq),
        grid=(batch, B_HEADS),
        in_specs=[head_spec(QB_HEAD0), head_spec(KB_HEAD0), head_spec(VB_HEAD0)],
        out_specs=pl.BlockSpec((seq, HEAD_DIM), lambda b, h: (b, h)),
        out_shape=jax.ShapeDtypeStruct((m, B_QKV), jnp.bfloat16),
        scratch_shapes=[pltpu.VMEM((3, seq, HEAD_DIM), jnp.bfloat16),
                        pltpu.VMEM((3, seq, HEAD_DIM), jnp.float32),
                        pltpu.VMEM((3, seq, HEAD_DIM), jnp.bfloat16),
                        pltpu.VMEM((3, seq, HEAD_DIM), jnp.bfloat16)]
                       + [pltpu.VMEM((2, seq, HEAD_DIM), jnp.float32)] * 4,
        compiler_params=pltpu.CompilerParams(
            dimension_semantics=("arbitrary", "arbitrary"), vmem_limit_bytes=VMEM_LIMIT),
        name="dilated_attn",
    )(proj, proj, proj)


def _layer_norm(y, g, b):
    mu = jnp.mean(y, axis=-1, keepdims=True)
    d = y - mu
    var = jnp.mean(d * d, axis=-1, keepdims=True)
    return d * lax.rsqrt(var + LN_EPS) * g + b


def _rms_norm(y, g):
    return y * lax.rsqrt(jnp.mean(y * y, axis=-1, keepdims=True) + RMS_EPS) * g


def _mix_ln1_kernel(oa_ref, ob_ref, x_ref, wo_ref, gna_ref, gnb_ref, g_ref, b_ref, h_ref):
    ya = _rms_norm(oa_ref[...].astype(jnp.float32), gna_ref[...])
    yb = _rms_norm(ob_ref[...].astype(jnp.float32), gnb_ref[...])
    mix = (jnp.dot(ya.astype(jnp.bfloat16), wo_ref[:A_Q, :], preferred_element_type=jnp.float32)
           + jnp.dot(yb.astype(jnp.bfloat16), wo_ref[A_Q:, :], preferred_element_type=jnp.float32))
    h_ref[...] = _layer_norm(ALPHA * x_ref[...] + mix, g_ref[...], b_ref[...])


def _mix_ln1(oa, ob, x2, wo_b, gn_a, gn_b, ln_g, ln_b):
    m = x2.shape[0]
    row = lambda w: pl.BlockSpec((MIX_TM, w), lambda i: (i, 0))
    full = lambda r, w: pl.BlockSpec((r, w), lambda i: (0, 0))
    return pl.pallas_call(
        _mix_ln1_kernel,
        grid=(m // MIX_TM,),
        in_specs=[row(A_Q), row(B_QKV), row(D_MODEL), full(D_MODEL, D_MODEL), full(1, A_Q),
                  full(1, B_QKV), full(1, D_MODEL), full(1, D_MODEL)],
        out_specs=row(D_MODEL),
        out_shape=jax.ShapeDtypeStruct((m, D_MODEL), jnp.float32),
        compiler_params=pltpu.CompilerParams(
            dimension_semantics=("arbitrary",), vmem_limit_bytes=VMEM_LIMIT),
        name="mix_ln1",
    )(oa, ob, x2, wo_b, gn_a, gn_b, ln_g, ln_b)


def _ffn_ln2_kernel(h_ref, w1g_ref, w2_ref, p_ref, wple_ref, g_ref, b_ref, o_ref, hb_ref, gate_ref):
    j = pl.program_id(1)
    n_ff = D_FF // FFN_TF
    n_all = (D_FF + D_MODEL) // FFN_TF
    n_sub = FFN_TF // FFN_SUB

    @pl.when(j == 0)
    def _():
        hb_ref[...] = h_ref[...].astype(jnp.bfloat16)
        o_ref[...] = jnp.zeros_like(o_ref)

    def up(c):
        return jnp.dot(hb_ref[...], w1g_ref[:, c * FFN_SUB:(c + 1) * FFN_SUB],
                       preferred_element_type=jnp.float32)

    @pl.when(j < n_ff)
    def _():
        acts = []
        for c in range(n_sub):
            r = jnp.maximum(up(c), 0.0)
            acts.append((r * r).astype(jnp.bfloat16))
        for n in range(D_MODEL // FFN_SUB):
            cols = slice(n * FFN_SUB, (n + 1) * FFN_SUB)
            part = sum(jnp.dot(acts[c], w2_ref[c * FFN_SUB:(c + 1) * FFN_SUB, cols],
                               preferred_element_type=jnp.float32) for c in range(n_sub))
            o_ref[:, cols] += part

    for t in range(n_all - n_ff):
        @pl.when(j == n_ff + t)
        def _():
            for c in range(n_sub):
                lo = t * FFN_TF + c * FFN_SUB
                gate_ref[:, lo:lo + FFN_SUB] = 1.0 / (1.0 + jnp.exp(-up(c)))

    @pl.when(j == n_all - 1)
    def _():
        ple = jnp.dot(p_ref[...].astype(jnp.bfloat16), wple_ref[...], preferred_element_type=jnp.float32)
        y = ALPHA * h_ref[...] + o_ref[...] + ple * gate_ref[...]
        o_ref[...] = _layer_norm(y, g_ref[...], b_ref[...])


def _ffn_ln2(h1, w1g_b, w2_b, p2, wple_b, ln_g, ln_b):
    m = h1.shape[0]
    n_ff = D_FF // FFN_TF
    n_all = (D_FF + D_MODEL) // FFN_TF
    return pl.pallas_call(
        _ffn_ln2_kernel,
        grid=(m // FFN_TM, n_all),
        in_specs=[pl.BlockSpec((FFN_TM, D_MODEL), lambda i, j: (i, 0)),
                  pl.BlockSpec((D_MODEL, FFN_TF), lambda i, j: (0, j)),
                  pl.BlockSpec((FFN_TF, D_MODEL), lambda i, j: (jnp.minimum(j, n_ff - 1), 0)),
                  pl.BlockSpec((FFN_TM, PLE_DIM), lambda i, j: (i, 0)),
                  pl.BlockSpec((PLE_DIM, D_MODEL), lambda i, j: (0, 0)),
                  pl.BlockSpec((1, D_MODEL), lambda i, j: (0, 0)),
                  pl.BlockSpec((1, D_MODEL), lambda i, j: (0, 0))],
        out_specs=pl.BlockSpec((FFN_TM, D_MODEL), lambda i, j: (i, 0)),
        out_shape=jax.ShapeDtypeStruct((m, D_MODEL), jnp.float32),
        scratch_shapes=[pltpu.VMEM((FFN_TM, D_MODEL), jnp.bfloat16),
                        pltpu.VMEM((FFN_TM, D_MODEL), jnp.float32)],
        compiler_params=pltpu.CompilerParams(
            dimension_semantics=("arbitrary", "arbitrary"), vmem_limit_bytes=VMEM_LIMIT),
        name="ffn_ln2",
    )(h1, w1g_b, w2_b, p2, wple_b, ln_g, ln_b)


def kernel(x, p, positions, w_in, sink_a, gn_a, gn_b, w_o, ln1_g, ln1_b, w1, w2, w_ple, w_ple_gate,
           ln2_g, ln2_b):
    batch, seq, d = x.shape
    m = batch * seq
    bf = jnp.bfloat16
    cos_t, sin_t = _rope_tables(positions)
    h = x.reshape(m, d)
    for i in range(DEPTH):
        proj = _in_proj(h, w_in[i].astype(bf), cos_t, sin_t)
        oa = _window_attn(proj, sink_a[i], batch=batch, seq=seq)
        ob = _dilated_attn(proj, batch=batch, seq=seq)
        h1 = _mix_ln1(oa, ob, h, w_o[i].astype(bf), gn_a[i].reshape(1, -1), gn_b[i].reshape(1, -1),
                      ln1_g[i].reshape(1, -1), ln1_b[i].reshape(1, -1))
        w1g = jnp.concatenate([w1[i], w_ple_gate[i]], axis=1).astype(bf)
        h = _ffn_ln2(h1, w1g, w2[i].astype(bf), p[i].reshape(m, PLE_DIM), w_ple[i].astype(bf),
                     ln2_g[i].reshape(1, -1), ln2_b[i].reshape(1, -1))
    return h.reshape(batch, seq, d)
```

```python
import functools

import jax
import jax.numpy as jnp
from jax import lax
from jax.experimental import pallas as pl
from jax.experimental.pallas import tpu as pltpu

D_MODEL = 2048
HEAD_DIM = 128
A_HEADS = 8
A_KV_HEADS = 2
A_WINDOW = 128
B_HEADS = 8
DILATED_PATTERNS = ((128, 1), (512, 4), (2048, 16))
ROT_DIM = HEAD_DIM // 4
ROPE_THETA = 500000.0
D_FF = 4 * D_MODEL
PLE_DIM = 256
DEPTH = 1
ALPHA = (2.0 * DEPTH) ** 0.25
LN_EPS = 1e-5
RMS_EPS = 1e-6
NEG_INF = -1e30
LOG2_E = 1.4426950408889634

A_Q = A_HEADS * HEAD_DIM
A_KV = A_KV_HEADS * HEAD_DIM
B_QKV = B_HEADS * HEAD_DIM
D_IN = A_Q + 2 * A_KV + 3 * B_QKV
N_PROJ_HEADS = D_IN // HEAD_DIM
QA_HEAD0 = 0
KA_HEAD0 = A_HEADS
VA_HEAD0 = A_HEADS + A_KV_HEADS
QB_HEAD0 = A_HEADS + 2 * A_KV_HEADS
KB_HEAD0 = QB_HEAD0 + B_HEADS
VB_HEAD0 = KB_HEAD0 + B_HEADS

LANES = 128
MXU_COLS = 256
Q_BLOCK = 128
SCORE_LOOKAHEAD = 5
VMEM_LIMIT = 56 * 1024 * 1024

PROJ_TM = 1024
PROJ_TN = 1536
MIX_TM = 512
FFN_TM = 512
FFN_TF = 1024
FFN_SUB = 512
ROPE_TM = 2048
KIND_Q, KIND_K, KIND_V = 0, 1, 2


def _rope_table_kernel(pos_ref, freq_ref, cos_ref, sin_ref):
    ang = pos_ref[...].astype(jnp.float32) * freq_ref[...]
    lane = lax.broadcasted_iota(jnp.int32, ang.shape, 1)
    half = ROT_DIM // 2
    c = jnp.where(lane < ROT_DIM, jnp.cos(ang), 1.0)
    s = jnp.sin(ang)
    s = jnp.where(lane < half, -s, jnp.where(lane < ROT_DIM, s, 0.0))
    q_scale = LOG2_E * HEAD_DIM ** -0.5
    cos_ref[KIND_Q] = c * q_scale
    sin_ref[KIND_Q] = s * q_scale
    cos_ref[KIND_K] = c
    sin_ref[KIND_K] = s
    cos_ref[KIND_V] = jnp.ones_like(c)
    sin_ref[KIND_V] = jnp.zeros_like(s)


def _rope_tables(positions):
    m = positions.size
    half = ROT_DIM // 2
    inv_freq = ROPE_THETA ** (-jnp.arange(0, ROT_DIM, 2, dtype=jnp.float32) / ROT_DIM)
    freq = jnp.tile(inv_freq, LANES // half).reshape(1, LANES)
    pos = positions.reshape(m, 1)
    out = jax.ShapeDtypeStruct((3, m, LANES), jnp.float32)
    return pl.pallas_call(
        _rope_table_kernel,
        grid=(m // ROPE_TM,),
        in_specs=[pl.BlockSpec((ROPE_TM, 1), lambda i: (i, 0)),
                  pl.BlockSpec((1, LANES), lambda i: (0, 0))],
        out_specs=[pl.BlockSpec((3, ROPE_TM, LANES), lambda i: (0, i, 0))] * 2,
        out_shape=[out, out],
        name="rope_table",
    )(pos, freq)


def _in_proj_kernel(x_ref, w_ref, cos_ref, sin_ref, o_ref, xb_ref):
    j = pl.program_id(1)

    @pl.when(j == 0)
    def _():
        xb_ref[...] = x_ref[...].astype(jnp.bfloat16)

    half = ROT_DIM // 2
    lane = lax.broadcasted_iota(jnp.int32, (PROJ_TM, LANES), 1)
    partner = jnp.where(lane < half, lane + half, jnp.where(lane < ROT_DIM, lane - half, lane))
    heads_per_tile = PROJ_TN // HEAD_DIM
    heads_per_chunk = MXU_COLS // HEAD_DIM
    for c in range(PROJ_TN // MXU_COLS):
        acc = jnp.dot(xb_ref[...], w_ref[:, c * MXU_COLS:(c + 1) * MXU_COLS],
                      preferred_element_type=jnp.float32)
        for g in range(heads_per_chunk):
            local = c * heads_per_chunk + g
            head = j * heads_per_tile + local
            is_q = (head < KA_HEAD0) | ((head >= QB_HEAD0) & (head < KB_HEAD0))
            is_k = ((head >= KA_HEAD0) & (head < VA_HEAD0)) | ((head >= KB_HEAD0) & (head < VB_HEAD0))
            kind = jnp.where(is_q, KIND_Q, jnp.where(is_k, KIND_K, KIND_V))
            t = acc[:, g * HEAD_DIM:(g + 1) * HEAD_DIM]
            swapped = jnp.take_along_axis(t, partner, axis=1)
            o_ref[local] = t * cos_ref[kind] + swapped * sin_ref[kind]


def _in_proj(x2, w_in_b, cos_t, sin_t):
    m = x2.shape[0]
    heads_per_tile = PROJ_TN // HEAD_DIM
    return pl.pallas_call(
        _in_proj_kernel,
        grid=(m // PROJ_TM, D_IN // PROJ_TN),
        in_specs=[pl.BlockSpec((PROJ_TM, D_MODEL), lambda i, j: (i, 0)),
                  pl.BlockSpec((D_MODEL, PROJ_TN), lambda i, j: (0, j)),
                  pl.BlockSpec((3, PROJ_TM, LANES), lambda i, j: (0, i, 0)),
                  pl.BlockSpec((3, PROJ_TM, LANES), lambda i, j: (0, i, 0))],
        out_specs=pl.BlockSpec((heads_per_tile, PROJ_TM, HEAD_DIM), lambda i, j: (j, i, 0)),
        out_shape=jax.ShapeDtypeStruct((N_PROJ_HEADS, m, HEAD_DIM), jnp.float32),
        scratch_shapes=[pltpu.VMEM((PROJ_TM, D_MODEL), jnp.bfloat16)],
        compiler_params=pltpu.CompilerParams(
            dimension_semantics=("arbitrary", "arbitrary"), vmem_limit_bytes=VMEM_LIMIT),
        name="in_proj",
    )(x2, w_in_b, cos_t, sin_t)


def _band_bias(delta, rows, cols, halo):
    d = delta + lax.broadcasted_iota(jnp.int32, (rows, cols), 0) - lax.broadcasted_iota(jnp.int32, (rows, cols), 1)
    return jnp.where(jnp.abs(d) <= halo, 0.0, NEG_INF).astype(jnp.float32)


def _softmax_pv(s, v, sink=None):
    m = jnp.max(s, axis=-1, keepdims=True)
    if sink is not None:
        m = jnp.maximum(m, sink)
    e = jnp.exp2(s - m)
    acc = jnp.dot(e.astype(v.dtype), v, preferred_element_type=jnp.float32)
    if v.shape[1] == HEAD_DIM:
        o, denom = acc, jnp.sum(e, axis=-1, keepdims=True)
    else:
        o, denom = acc[:, :HEAD_DIM], acc[:, HEAD_DIM:]
    if sink is not None:
        denom = denom + jnp.exp2(sink - m)
    return o, m, denom


def _store_v_ones(vo_ref, v):
    vo_ref[:, :HEAD_DIM] = v.astype(vo_ref.dtype)
    vo_ref[:, HEAD_DIM:] = jnp.ones(v.shape, vo_ref.dtype)


def _qk(q, k):
    return lax.dot_general(q, k, (((1,), (1,)), ((), ())), preferred_element_type=jnp.float32)


def _window_start(q0, halo, window, length):
    return min(max(q0 - halo, 0), length - window)


def _run_pipelined(n, scores, finish, depth=SCORE_LOOKAHEAD):
    pending = {}
    for j in range(n + depth):
        if j < n:
            pending[j] = scores(j)
        if j >= depth:
            finish(j - depth, pending.pop(j - depth))


def _window_attn_kernel(q_ref, k_ref, v_ref, sink_ref, o_ref, qkv_ref, *, seq):
    window = Q_BLOCK + 2 * A_WINDOW
    head = pl.program_id(1)
    sink = sink_ref[head] * LOG2_E
    qkv_ref[0] = q_ref[...].astype(jnp.bfloat16)

    @pl.when(head % (A_HEADS // A_KV_HEADS) == 0)
    def _():
        qkv_ref[1] = k_ref[...].astype(jnp.bfloat16)
        qkv_ref[2] = v_ref[...].astype(jnp.bfloat16)

    biases = {}

    def scores(j):
        q0 = j * Q_BLOCK
        w0 = _window_start(q0, A_WINDOW, window, seq)
        if q0 - w0 not in biases:
            biases[q0 - w0] = _band_bias(q0 - w0, Q_BLOCK, window, A_WINDOW)
        return _qk(qkv_ref[0, q0:q0 + Q_BLOCK, :], qkv_ref[1, w0:w0 + window, :]) + biases[q0 - w0]

    def finish(j, s):
        q0 = j * Q_BLOCK
        w0 = _window_start(q0, A_WINDOW, window, seq)
        o, _, denom = _softmax_pv(s, qkv_ref[2, w0:w0 + window, :], sink)
        o_ref[q0:q0 + Q_BLOCK, :] = (o / denom).astype(o_ref.dtype)

    _run_pipelined(seq // Q_BLOCK, scores, finish)


def _window_attn(proj, sink, *, batch, seq):
    m = batch * seq
    grp = A_HEADS // A_KV_HEADS
    head_spec = lambda h0, div: pl.BlockSpec((None, seq, HEAD_DIM), lambda b, h: (h0 + h // div, b, 0))
    return pl.pallas_call(
        functools.partial(_window_attn_kernel, seq=seq),
        grid=(batch, A_HEADS),
        in_specs=[head_spec(QA_HEAD0, 1), head_spec(KA_HEAD0, grp), head_spec(VA_HEAD0, grp),
                  pl.BlockSpec(memory_space=pltpu.SMEM)],
        out_specs=pl.BlockSpec((seq, HEAD_DIM), lambda b, h: (b, h)),
        out_shape=jax.ShapeDtypeStruct((m, A_Q), jnp.bfloat16),
        scratch_shapes=[pltpu.VMEM((3, seq, HEAD_DIM), jnp.bfloat16)],
        compiler_params=pltpu.CompilerParams(
            dimension_semantics=("arbitrary", "arbitrary"), vmem_limit_bytes=VMEM_LIMIT),
        name="window_attn",
    )(proj, proj, proj, sink)


def _mix2(a, b):
    m = jnp.maximum(a[1], b[1])
    fa = jnp.exp2(a[1] - m)
    fb = jnp.exp2(b[1] - m)
    return fa * a[0] + fb * b[0], m, fa * a[2] + fb * b[2]


def _dilated_attn_kernel(q_ref, k_ref, v_ref, o_ref, nat_ref, ca_ref, cab_ref, cbb_ref, vn_ref, va_ref, vb_ref,
                         pb_ref, pc_ref, pa_ref, pn_ref, *, seq):
    bf = jnp.bfloat16
    (w1, r1), (wa, ra), (wb, rb) = sorted(DILATED_PATTERNS, key=lambda pat: pat[1])
    assert r1 == 1 and rb % ra == 0
    sub = rb // ra
    len_a, len_b = seq // ra, seq // rb
    halo1, halo_a, halo_b = w1 // 2, wa // (2 * ra), wb // (2 * rb)
    biases = {}

    def band_scores(qk_src, base, q0, w0, window, halo):
        key = (q0 - w0, Q_BLOCK, window, halo)
        if key not in biases:
            biases[key] = _band_bias(*key)
        return _qk(qk_src[0, base + q0:base + q0 + Q_BLOCK, :],
                   qk_src[1, base + w0:base + w0 + window, :]) + biases[key]

    def band_output(s, v_src, base, w0, window):
        o, m, denom = _softmax_pv(s, v_src[base + w0:base + w0 + window, :])
        return o, jnp.broadcast_to(m, (Q_BLOCK, LANES)), denom

    def blocks(length, halo):
        window = min(Q_BLOCK + 2 * halo, length)
        for j in range(length // Q_BLOCK):
            q0 = j * Q_BLOCK
            yield q0, _window_start(q0, halo, window, length), window

    def load3(ref, rows):
        return tuple(ref[t, rows, :] for t in range(3))

    def store3(ref, rows, triple):
        for t in range(3):
            ref[t, rows, :] = triple[t]

    for t, src in enumerate((q_ref, k_ref, v_ref)):
        if t < 2:
            nat_ref[t] = src[...].astype(bf)
        else:
            _store_v_ones(vn_ref, src[...])
        for c in range(ra):
            x = src[pl.ds(c, len_a, stride=ra), :]
            ca_ref[t, c * len_a:(c + 1) * len_a, :] = x
            if t < 2:
                cab_ref[t, c * len_a:(c + 1) * len_a, :] = x.astype(bf)
            else:
                _store_v_ones(va_ref.at[c * len_a:(c + 1) * len_a, :], x)
    for t in range(3):
        for c in range(ra):
            for u in range(sub):
                rows = slice((c * sub + u) * len_b, (c * sub + u + 1) * len_b)
                x = ca_ref[t, pl.ds(c * len_a + u, len_b, stride=sub), :]
                if t < 2:
                    cbb_ref[t, rows, :] = x.astype(bf)
                else:
                    _store_v_ones(vb_ref.at[rows, :], x)

    units = []

    def add_unit(qk_src, v_src, base, q0, w0, window, halo, finish):
        units.append((lambda: band_scores(qk_src, base, q0, w0, window, halo),
                      lambda s: finish(band_output(s, v_src, base, w0, window))))

    for c in range(ra):
        for u in range(sub):
            cu = c * sub + u
            todo = list(blocks(len_b, halo_b))
            for q0, w0, window in todo:
                def finish_b(triple, cu=cu, c=c, q0=q0, last=(u == sub - 1 and q0 == todo[-1][0])):
                    store3(pb_ref, slice(cu * len_b + q0, cu * len_b + q0 + Q_BLOCK), triple)
                    if last:
                        for uu in range(sub):
                            src_rows = slice((c * sub + uu) * len_b, (c * sub + uu + 1) * len_b)
                            store3(pc_ref, pl.ds(c * len_a + uu, len_b, stride=sub), load3(pb_ref, src_rows))
                add_unit(cbb_ref, vb_ref, cu * len_b, q0, w0, window, halo_b, finish_b)

    for c in range(ra):
        todo = list(blocks(len_a, halo_a))
        for q0, w0, window in todo:
            def finish_a(triple, c=c, q0=q0, last=(q0 == todo[-1][0])):
                rows = slice(c * len_a + q0, c * len_a + q0 + Q_BLOCK)
                store3(pa_ref, rows, _mix2(triple, load3(pc_ref, rows)))
                if last:
                    store3(pn_ref, pl.ds(c, len_a, stride=ra), load3(pa_ref, slice(c * len_a, (c + 1) * len_a)))
            add_unit(cab_ref, va_ref, c * len_a, q0, w0, window, halo_a, finish_a)

    for q0, w0, window in blocks(seq, halo1):
        def finish_1(triple, q0=q0):
            o, _, denom = _mix2(triple, load3(pn_ref, slice(q0, q0 + Q_BLOCK)))
            o_ref[q0:q0 + Q_BLOCK, :] = (o / denom).astype(o_ref.dtype)
        add_unit(nat_ref, vn_ref, 0, q0, w0, window, halo1, finish_1)

    _run_pipelined(len(units), lambda j: units[j][0](), lambda j, s: units[j][1](s))


def _dilated_attn(proj, *, batch, seq):
    m = batch * seq
    head_spec = lambda h0: pl.BlockSpec((None, seq, HEAD_DIM), lambda b, h: (h0 + h, b, 0))
    qk_buf = pltpu.VMEM((2, seq, HEAD_DIM), jnp.bfloat16)
    v_ones_buf = pltpu.VMEM((seq, 2 * HEAD_DIM), jnp.bfloat16)
    return pl.pallas_call(
        functools.partial(_dilated_attn_kernel, seq=seq),
        grid=(batch, B_HEADS),
        in_specs=[head_spec(QB_HEAD0), head_spec(KB_HEAD0), head_spec(VB_HEAD0)],
        out_specs=pl.BlockSpec((seq, HEAD_DIM), lambda b, h: (b, h)),
        out_shape=jax.ShapeDtypeStruct((m, B_QKV), jnp.bfloat16),
        scratch_shapes=[qk_buf,
                        pltpu.VMEM((3, seq, HEAD_DIM), jnp.float32),
                        qk_buf, qk_buf,
                        v_ones_buf, v_ones_buf, v_ones_buf]
                       + [pltpu.VMEM((3, seq, HEAD_DIM), jnp.float32)] * 4,
        compiler_params=pltpu.CompilerParams(
            dimension_semantics=("arbitrary", "arbitrary"), vmem_limit_bytes=VMEM_LIMIT),
        name="dilated_attn",
    )(proj, proj, proj)


def _layer_norm(y, g, b):
    mu = jnp.mean(y, axis=-1, keepdims=True)
    d = y - mu
    var = jnp.mean(d * d, axis=-1, keepdims=True)
    return d * lax.rsqrt(var + LN_EPS) * g + b


def _rms_norm(y, g):
    return y * lax.rsqrt(jnp.mean(y * y, axis=-1, keepdims=True) + RMS_EPS) * g


def _mix_ln1_kernel(oa_ref, ob_ref, x_ref, wo_ref, gna_ref, gnb_ref, g_ref, b_ref, h_ref):
    ya = _rms_norm(oa_ref[...].astype(jnp.float32), gna_ref[...])
    yb = _rms_norm(ob_ref[...].astype(jnp.float32), gnb_ref[...])
    mix = (jnp.dot(ya.astype(jnp.bfloat16), wo_ref[:A_Q, :], preferred_element_type=jnp.float32)
           + jnp.dot(yb.astype(jnp.bfloat16), wo_ref[A_Q:, :], preferred_element_type=jnp.float32))
    h_ref[...] = _layer_norm(ALPHA * x_ref[...] + mix, g_ref[...], b_ref[...])


def _mix_ln1(oa, ob, x2, wo_b, gn_a, gn_b, ln_g, ln_b):
    m = x2.shape[0]
    row = lambda w: pl.BlockSpec((MIX_TM, w), lambda i: (i, 0))
    full = lambda r, w: pl.BlockSpec((r, w), lambda i: (0, 0))
    return pl.pallas_call(
        _mix_ln1_kernel,
        grid=(m // MIX_TM,),
        in_specs=[row(A_Q), row(B_QKV), row(D_MODEL), full(D_MODEL, D_MODEL), full(1, A_Q),
                  full(1, B_QKV), full(1, D_MODEL), full(1, D_MODEL)],
        out_specs=row(D_MODEL),
        out_shape=jax.ShapeDtypeStruct((m, D_MODEL), jnp.float32),
        compiler_params=pltpu.CompilerParams(
            dimension_semantics=("arbitrary",), vmem_limit_bytes=VMEM_LIMIT),
        name="mix_ln1",
    )(oa, ob, x2, wo_b, gn_a, gn_b, ln_g, ln_b)


def _ffn_ln2_kernel(h_ref, w1g_ref, w2_ref, p_ref, wple_ref, g_ref, b_ref, o_ref, hb_ref, gate_ref):
    j = pl.program_id(1)
    n_ff = D_FF // FFN_TF
    n_all = (D_FF + D_MODEL) // FFN_TF
    n_sub = FFN_TF // FFN_SUB

    @pl.when(j == 0)
    def _():
        h = h_ref[...]
        hb_ref[...] = h.astype(jnp.bfloat16)
        o_ref[...] = ALPHA * h

    def up(c):
        return jnp.dot(hb_ref[...], w1g_ref[:, c * FFN_SUB:(c + 1) * FFN_SUB],
                       preferred_element_type=jnp.float32)

    @pl.when(j < n_ff)
    def _():
        acts = []
        for c in range(n_sub):
            r = jnp.maximum(up(c), 0.0)
            acts.append((r * r).astype(jnp.bfloat16))
        for n in range(D_MODEL // FFN_SUB):
            cols = slice(n * FFN_SUB, (n + 1) * FFN_SUB)
            part = sum(jnp.dot(acts[c], w2_ref[c * FFN_SUB:(c + 1) * FFN_SUB, cols],
                               preferred_element_type=jnp.float32) for c in range(n_sub))
            o_ref[:, cols] += part

    for t in range(n_all - n_ff):
        @pl.when(j == n_ff + t)
        def _():
            for c in range(n_sub):
                lo = t * FFN_TF + c * FFN_SUB
                gate_ref[:, lo:lo + FFN_SUB] = 0.5 * jnp.tanh(0.5 * up(c)) + 0.5

    @pl.when(j == n_all - 1)
    def _():
        ple = jnp.dot(p_ref[...].astype(jnp.bfloat16), wple_ref[...], preferred_element_type=jnp.float32)
        o_ref[...] = _layer_norm(o_ref[...] + ple * gate_ref[...], g_ref[...], b_ref[...])


def _ffn_ln2(h1, w1g_b, w2_b, p2, wple_b, ln_g, ln_b):
    m = h1.shape[0]
    n_ff = D_FF // FFN_TF
    n_all = (D_FF + D_MODEL) // FFN_TF
    return pl.pallas_call(
        _ffn_ln2_kernel,
        grid=(m // FFN_TM, n_all),
        in_specs=[pl.BlockSpec((FFN_TM, D_MODEL), lambda i, j: (i, 0)),
                  pl.BlockSpec((D_MODEL, FFN_TF), lambda i, j: (0, j)),
                  pl.BlockSpec((FFN_TF, D_MODEL), lambda i, j: (jnp.minimum(j, n_ff - 1), 0)),
                  pl.BlockSpec((FFN_TM, PLE_DIM), lambda i, j: (i, 0)),
                  pl.BlockSpec((PLE_DIM, D_MODEL), lambda i, j: (0, 0)),
                  pl.BlockSpec((1, D_MODEL), lambda i, j: (0, 0)),
                  pl.BlockSpec((1, D_MODEL), lambda i, j: (0, 0))],
        out_specs=pl.BlockSpec((FFN_TM, D_MODEL), lambda i, j: (i, 0)),
        out_shape=jax.ShapeDtypeStruct((m, D_MODEL), jnp.float32),
        scratch_shapes=[pltpu.VMEM((FFN_TM, D_MODEL), jnp.bfloat16),
                        pltpu.VMEM((FFN_TM, D_MODEL), jnp.float32)],
        compiler_params=pltpu.CompilerParams(
            dimension_semantics=("arbitrary", "arbitrary"), vmem_limit_bytes=VMEM_LIMIT),
        name="ffn_ln2",
    )(h1, w1g_b, w2_b, p2, wple_b, ln_g, ln_b)


def kernel(x, p, positions, w_in, sink_a, gn_a, gn_b, w_o, ln1_g, ln1_b, w1, w2, w_ple, w_ple_gate,
           ln2_g, ln2_b):
    batch, seq, d = x.shape
    m = batch * seq
    bf = jnp.bfloat16
    cos_t, sin_t = _rope_tables(positions)
    h = x.reshape(m, d)
    for i in range(DEPTH):
        proj = _in_proj(h, w_in[i].astype(bf), cos_t, sin_t)
        oa = _window_attn(proj, sink_a[i], batch=batch, seq=seq)
        ob = _dilated_attn(proj, batch=batch, seq=seq)
        h1 = _mix_ln1(oa, ob, h, w_o[i].astype(bf), gn_a[i].reshape(1, -1), gn_b[i].reshape(1, -1),
                      ln1_g[i].reshape(1, -1), ln1_b[i].reshape(1, -1))
        w1g = jnp.concatenate([w1[i], w_ple_gate[i]], axis=1).astype(bf)
        h = _ffn_ln2(h1, w1g, w2[i].astype(bf), p[i].reshape(m, PLE_DIM), w_ple[i].astype(bf),
                     ln2_g[i].reshape(1, -1), ln2_b[i].reshape(1, -1))
    return h.reshape(batch, seq, d)
```

```python
import functools

import jax
import jax.numpy as jnp
from jax import lax
from jax.experimental import pallas as pl
from jax.experimental.pallas import tpu as pltpu

D_MODEL = 2048
HEAD_DIM = 128
A_HEADS = 8
A_KV_HEADS = 2
A_WINDOW = 128
B_HEADS = 8
DILATED_PATTERNS = ((128, 1), (512, 4), (2048, 16))
ROT_DIM = HEAD_DIM // 4
ROPE_THETA = 500000.0
D_FF = 4 * D_MODEL
PLE_DIM = 256
DEPTH = 1
ALPHA = (2.0 * DEPTH) ** 0.25
LN_EPS = 1e-5
RMS_EPS = 1e-6
NEG_INF = -1e30
LOG2_E = 1.4426950408889634

A_Q = A_HEADS * HEAD_DIM
A_KV = A_KV_HEADS * HEAD_DIM
B_QKV = B_HEADS * HEAD_DIM
D_IN = A_Q + 2 * A_KV + 3 * B_QKV
N_PROJ_HEADS = D_IN // HEAD_DIM
QA_HEAD0 = 0
KA_HEAD0 = A_HEADS
VA_HEAD0 = A_HEADS + A_KV_HEADS
QB_HEAD0 = A_HEADS + 2 * A_KV_HEADS
KB_HEAD0 = QB_HEAD0 + B_HEADS
VB_HEAD0 = KB_HEAD0 + B_HEADS

LANES = 128
MXU_COLS = 256
Q_BLOCK = 128
SCORE_LOOKAHEAD = 5
VMEM_LIMIT = 56 * 1024 * 1024

PROJ_TM = 1024
PROJ_TN = 1536
MIX_TM = 512
FFN_TM = 512
FFN_TF = 1024
FFN_SUB = 512
ROPE_TM = 2048
KIND_Q, KIND_K, KIND_V = 0, 1, 2


def _rope_table_kernel(pos_ref, freq_ref, cos_ref, sin_ref):
    ang = pos_ref[...].astype(jnp.float32) * freq_ref[...]
    lane = lax.broadcasted_iota(jnp.int32, ang.shape, 1)
    half = ROT_DIM // 2
    c = jnp.where(lane < ROT_DIM, jnp.cos(ang), 1.0)
    s = jnp.sin(ang)
    s = jnp.where(lane < half, -s, jnp.where(lane < ROT_DIM, s, 0.0))
    q_scale = LOG2_E * HEAD_DIM ** -0.5
    cos_ref[KIND_Q] = c * q_scale
    sin_ref[KIND_Q] = s * q_scale
    cos_ref[KIND_K] = c
    sin_ref[KIND_K] = s
    cos_ref[KIND_V] = jnp.ones_like(c)
    sin_ref[KIND_V] = jnp.zeros_like(s)


def _rope_tables(positions):
    m = positions.size
    half = ROT_DIM // 2
    inv_freq = ROPE_THETA ** (-jnp.arange(0, ROT_DIM, 2, dtype=jnp.float32) / ROT_DIM)
    freq = jnp.tile(inv_freq, LANES // half).reshape(1, LANES)
    pos = positions.reshape(m, 1)
    out = jax.ShapeDtypeStruct((3, m, LANES), jnp.float32)
    return pl.pallas_call(
        _rope_table_kernel,
        grid=(m // ROPE_TM,),
        in_specs=[pl.BlockSpec((ROPE_TM, 1), lambda i: (i, 0)),
                  pl.BlockSpec((1, LANES), lambda i: (0, 0))],
        out_specs=[pl.BlockSpec((3, ROPE_TM, LANES), lambda i: (0, i, 0))] * 2,
        out_shape=[out, out],
        name="rope_table",
    )(pos, freq)


def _in_proj_kernel(x_ref, w_ref, cos_ref, sin_ref, o_ref, xb_ref):
    j = pl.program_id(1)

    @pl.when(j == 0)
    def _():
        xb_ref[...] = x_ref[...].astype(jnp.bfloat16)

    half = ROT_DIM // 2
    lane = lax.broadcasted_iota(jnp.int32, (PROJ_TM, LANES), 1)
    partner = jnp.where(lane < half, lane + half, jnp.where(lane < ROT_DIM, lane - half, lane))
    heads_per_tile = PROJ_TN // HEAD_DIM
    heads_per_chunk = MXU_COLS // HEAD_DIM
    for c in range(PROJ_TN // MXU_COLS):
        acc = jnp.dot(xb_ref[...], w_ref[:, c * MXU_COLS:(c + 1) * MXU_COLS],
                      preferred_element_type=jnp.float32)
        for g in range(heads_per_chunk):
            local = c * heads_per_chunk + g
            head = j * heads_per_tile + local
            is_q = (head < KA_HEAD0) | ((head >= QB_HEAD0) & (head < KB_HEAD0))
            is_k = ((head >= KA_HEAD0) & (head < VA_HEAD0)) | ((head >= KB_HEAD0) & (head < VB_HEAD0))
            kind = jnp.where(is_q, KIND_Q, jnp.where(is_k, KIND_K, KIND_V))
            t = acc[:, g * HEAD_DIM:(g + 1) * HEAD_DIM]
            swapped = jnp.take_along_axis(t, partner, axis=1)
            o_ref[local] = t * cos_ref[kind] + swapped * sin_ref[kind]


def _in_proj(x2, w_in_b, cos_t, sin_t):
    m = x2.shape[0]
    heads_per_tile = PROJ_TN // HEAD_DIM
    return pl.pallas_call(
        _in_proj_kernel,
        grid=(m // PROJ_TM, D_IN // PROJ_TN),
        in_specs=[pl.BlockSpec((PROJ_TM, D_MODEL), lambda i, j: (i, 0)),
                  pl.BlockSpec((D_MODEL, PROJ_TN), lambda i, j: (0, j)),
                  pl.BlockSpec((3, PROJ_TM, LANES), lambda i, j: (0, i, 0)),
                  pl.BlockSpec((3, PROJ_TM, LANES), lambda i, j: (0, i, 0))],
        out_specs=pl.BlockSpec((heads_per_tile, PROJ_TM, HEAD_DIM), lambda i, j: (j, i, 0)),
        out_shape=jax.ShapeDtypeStruct((N_PROJ_HEADS, m, HEAD_DIM), jnp.float32),
        scratch_shapes=[pltpu.VMEM((PROJ_TM, D_MODEL), jnp.bfloat16)],
        compiler_params=pltpu.CompilerParams(
            dimension_semantics=("arbitrary", "arbitrary"), vmem_limit_bytes=VMEM_LIMIT),
        name="in_proj",
    )(x2, w_in_b, cos_t, sin_t)


def _band_bias(delta, rows, cols, halo):
    d = delta + lax.broadcasted_iota(jnp.int32, (rows, cols), 0) - lax.broadcasted_iota(jnp.int32, (rows, cols), 1)
    return jnp.where(jnp.abs(d) <= halo, 0.0, NEG_INF).astype(jnp.float32)


def _softmax_pv(s, v, sink=None):
    m = jnp.max(s, axis=-1, keepdims=True)
    if sink is not None:
        m = jnp.maximum(m, sink)
    e = jnp.exp2(s - m)
    acc = jnp.dot(e.astype(v.dtype), v, preferred_element_type=jnp.float32)
    if v.shape[1] == HEAD_DIM:
        o, denom = acc, jnp.sum(e, axis=-1, keepdims=True)
    else:
        o, denom = acc[:, :HEAD_DIM], acc[:, HEAD_DIM:]
    if sink is not None:
        denom = denom + jnp.exp2(sink - m)
    return o, m, denom


def _store_v_ones(vo_ref, v):
    vo_ref[:, :HEAD_DIM] = v.astype(vo_ref.dtype)
    vo_ref[:, HEAD_DIM:] = jnp.ones(v.shape, vo_ref.dtype)


def _qk(q, k):
    return lax.dot_general(q, k, (((1,), (1,)), ((), ())), preferred_element_type=jnp.float32)


def _window_start(q0, halo, window, length):
    return min(max(q0 - halo, 0), length - window)


def _run_pipelined(n, scores, finish, depth=SCORE_LOOKAHEAD):
    pending = {}
    for j in range(n + depth):
        if j < n:
            pending[j] = scores(j)
        if j >= depth:
            finish(j - depth, pending.pop(j - depth))


def _window_attn_kernel(q_ref, k_ref, v_ref, sink_ref, o_ref, qkv_ref, *, seq):
    window = Q_BLOCK + 2 * A_WINDOW
    head = pl.program_id(1)
    sink = sink_ref[head] * LOG2_E
    qkv_ref[0] = q_ref[...].astype(jnp.bfloat16)

    @pl.when(head % (A_HEADS // A_KV_HEADS) == 0)
    def _():
        qkv_ref[1] = k_ref[...].astype(jnp.bfloat16)
        qkv_ref[2] = v_ref[...].astype(jnp.bfloat16)

    biases = {}

    def scores(j):
        q0 = j * Q_BLOCK
        w0 = _window_start(q0, A_WINDOW, window, seq)
        if q0 - w0 not in biases:
            biases[q0 - w0] = _band_bias(q0 - w0, Q_BLOCK, window, A_WINDOW)
        return _qk(qkv_ref[0, q0:q0 + Q_BLOCK, :], qkv_ref[1, w0:w0 + window, :]) + biases[q0 - w0]

    def finish(j, s):
        q0 = j * Q_BLOCK
        w0 = _window_start(q0, A_WINDOW, window, seq)
        o, _, denom = _softmax_pv(s, qkv_ref[2, w0:w0 + window, :], sink)
        o_ref[q0:q0 + Q_BLOCK, :] = (o / denom).astype(o_ref.dtype)

    _run_pipelined(seq // Q_BLOCK, scores, finish)


def _window_attn(proj, sink, *, batch, seq):
    m = batch * seq
    grp = A_HEADS // A_KV_HEADS
    head_spec = lambda h0, div: pl.BlockSpec((None, seq, HEAD_DIM), lambda b, h: (h0 + h // div, b, 0))
    return pl.pallas_call(
        functools.partial(_window_attn_kernel, seq=seq),
        grid=(batch, A_HEADS),
        in_specs=[head_spec(QA_HEAD0, 1), head_spec(KA_HEAD0, grp), head_spec(VA_HEAD0, grp),
                  pl.BlockSpec(memory_space=pltpu.SMEM)],
        out_specs=pl.BlockSpec((seq, HEAD_DIM), lambda b, h: (b, h)),
        out_shape=jax.ShapeDtypeStruct((m, A_Q), jnp.bfloat16),
        scratch_shapes=[pltpu.VMEM((3, seq, HEAD_DIM), jnp.bfloat16)],
        compiler_params=pltpu.CompilerParams(
            dimension_semantics=("arbitrary", "arbitrary"), vmem_limit_bytes=VMEM_LIMIT),
        name="window_attn",
    )(proj, proj, proj, sink)


def _mix2(a, b):
    m = jnp.maximum(a[1], b[1])
    fa = jnp.exp2(a[1] - m)
    fb = jnp.exp2(b[1] - m)
    return fa * a[0] + fb * b[0], m, fa * a[2] + fb * b[2]


def _dilated_attn_kernel(q_ref, k_ref, v_ref, o_ref, nat_ref, ca_ref, cab_ref, cbb_ref, vn_ref, va_ref, vb_ref,
                         pc_ref, pn_ref, *, seq):
    bf = jnp.bfloat16
    (w1, r1), (wa, ra), (wb, rb) = sorted(DILATED_PATTERNS, key=lambda pat: pat[1])
    assert r1 == 1 and rb % ra == 0
    sub = rb // ra
    len_a, len_b = seq // ra, seq // rb
    halo1, halo_a, halo_b = w1 // 2, wa // (2 * ra), wb // (2 * rb)
    biases = {}

    def band_scores(qk_src, base, q0, w0, window, halo):
        key = (q0 - w0, Q_BLOCK, window, halo)
        if key not in biases:
            biases[key] = _band_bias(*key)
        return _qk(qk_src[0, base + q0:base + q0 + Q_BLOCK, :],
                   qk_src[1, base + w0:base + w0 + window, :]) + biases[key]

    def band_output(s, v_src, base, w0, window):
        o, m, denom = _softmax_pv(s, v_src[base + w0:base + w0 + window, :])
        return o, jnp.broadcast_to(m, (Q_BLOCK, LANES)), denom

    def blocks(length, halo):
        window = min(Q_BLOCK + 2 * halo, length)
        for j in range(length // Q_BLOCK):
            q0 = j * Q_BLOCK
            yield q0, _window_start(q0, halo, window, length), window

    def load3(ref, rows):
        return tuple(ref[t, rows, :] for t in range(3))

    def store3(ref, rows, triple):
        for t in range(3):
            ref[t, rows, :] = triple[t]

    for t, src in enumerate((q_ref, k_ref, v_ref)):
        if t < 2:
            nat_ref[t] = src[...].astype(bf)
        else:
            _store_v_ones(vn_ref, src[...])
        for c in range(ra):
            x = src[pl.ds(c, len_a, stride=ra), :]
            ca_ref[t, c * len_a:(c + 1) * len_a, :] = x
            if t < 2:
                cab_ref[t, c * len_a:(c + 1) * len_a, :] = x.astype(bf)
            else:
                _store_v_ones(va_ref.at[c * len_a:(c + 1) * len_a, :], x)
    for t in range(3):
        for c in range(ra):
            for u in range(sub):
                rows = slice((c * sub + u) * len_b, (c * sub + u + 1) * len_b)
                x = ca_ref[t, pl.ds(c * len_a + u, len_b, stride=sub), :]
                if t < 2:
                    cbb_ref[t, rows, :] = x.astype(bf)
                else:
                    _store_v_ones(vb_ref.at[rows, :], x)

    units = []

    def add_unit(qk_src, v_src, base, q0, w0, window, halo, finish):
        units.append((lambda: band_scores(qk_src, base, q0, w0, window, halo),
                      lambda s: finish(band_output(s, v_src, base, w0, window))))

    for c in range(ra):
        for u in range(sub):
            cu = c * sub + u
            for q0, w0, window in blocks(len_b, halo_b):
                def finish_b(triple, c=c, u=u, q0=q0):
                    store3(pc_ref, pl.ds(c * len_a + q0 * sub + u, Q_BLOCK, stride=sub), triple)
                add_unit(cbb_ref, vb_ref, cu * len_b, q0, w0, window, halo_b, finish_b)

    for c in range(ra):
        for q0, w0, window in blocks(len_a, halo_a):
            def finish_a(triple, c=c, q0=q0):
                merged = _mix2(triple, load3(pc_ref, slice(c * len_a + q0, c * len_a + q0 + Q_BLOCK)))
                store3(pn_ref, pl.ds(q0 * ra + c, Q_BLOCK, stride=ra), merged)
            add_unit(cab_ref, va_ref, c * len_a, q0, w0, window, halo_a, finish_a)

    for q0, w0, window in blocks(seq, halo1):
        def finish_1(triple, q0=q0):
            o, _, denom = _mix2(triple, load3(pn_ref, slice(q0, q0 + Q_BLOCK)))
            o_ref[q0:q0 + Q_BLOCK, :] = (o / denom).astype(o_ref.dtype)
        add_unit(nat_ref, vn_ref, 0, q0, w0, window, halo1, finish_1)

    _run_pipelined(len(units), lambda j: units[j][0](), lambda j, s: units[j][1](s))


def _dilated_attn(proj, *, batch, seq):
    m = batch * seq
    head_spec = lambda h0: pl.BlockSpec((None, seq, HEAD_DIM), lambda b, h: (h0 + h, b, 0))
    qk_buf = pltpu.VMEM((2, seq, HEAD_DIM), jnp.bfloat16)
    v_ones_buf = pltpu.VMEM((seq, 2 * HEAD_DIM), jnp.bfloat16)
    return pl.pallas_call(
        functools.partial(_dilated_attn_kernel, seq=seq),
        grid=(batch, B_HEADS),
        in_specs=[head_spec(QB_HEAD0), head_spec(KB_HEAD0), head_spec(VB_HEAD0)],
        out_specs=pl.BlockSpec((seq, HEAD_DIM), lambda b, h: (b, h)),
        out_shape=jax.ShapeDtypeStruct((m, B_QKV), jnp.bfloat16),
        scratch_shapes=[qk_buf,
                        pltpu.VMEM((3, seq, HEAD_DIM), jnp.float32),
                        qk_buf, qk_buf,
                        v_ones_buf, v_ones_buf, v_ones_buf]
                       + [pltpu.VMEM((3, seq, HEAD_DIM), jnp.float32)] * 2,
        compiler_params=pltpu.CompilerParams(
            dimension_semantics=("arbitrary", "arbitrary"), vmem_limit_bytes=VMEM_LIMIT),
        name="dilated_attn",
    )(proj, proj, proj)


def _layer_norm(y, g, b):
    mu = jnp.mean(y, axis=-1, keepdims=True)
    d = y - mu
    var = jnp.mean(d * d, axis=-1, keepdims=True)
    return d * lax.rsqrt(var + LN_EPS) * g + b


def _rms_norm(y, g):
    return y * lax.rsqrt(jnp.mean(y * y, axis=-1, keepdims=True) + RMS_EPS) * g


def _mix_ln1_kernel(oa_ref, ob_ref, x_ref, wo_ref, gna_ref, gnb_ref, g_ref, b_ref, h_ref, hb_ref):
    ya = _rms_norm(oa_ref[...].astype(jnp.float32), gna_ref[...])
    yb = _rms_norm(ob_ref[...].astype(jnp.float32), gnb_ref[...])
    mix = (jnp.dot(ya.astype(jnp.bfloat16), wo_ref[:A_Q, :], preferred_element_type=jnp.float32)
           + jnp.dot(yb.astype(jnp.bfloat16), wo_ref[A_Q:, :], preferred_element_type=jnp.float32))
    h = _layer_norm(ALPHA * x_ref[...] + mix, g_ref[...], b_ref[...])
    h_ref[...] = h
    hb_ref[...] = h.astype(hb_ref.dtype)


def _mix_ln1(oa, ob, x2, wo_b, gn_a, gn_b, ln_g, ln_b):
    m = x2.shape[0]
    row = lambda w: pl.BlockSpec((MIX_TM, w), lambda i: (i, 0))
    full = lambda r, w: pl.BlockSpec((r, w), lambda i: (0, 0))
    return pl.pallas_call(
        _mix_ln1_kernel,
        grid=(m // MIX_TM,),
        in_specs=[row(A_Q), row(B_QKV), row(D_MODEL), full(D_MODEL, D_MODEL), full(1, A_Q),
                  full(1, B_QKV), full(1, D_MODEL), full(1, D_MODEL)],
        out_specs=[row(D_MODEL), row(D_MODEL)],
        out_shape=[jax.ShapeDtypeStruct((m, D_MODEL), jnp.float32),
                   jax.ShapeDtypeStruct((m, D_MODEL), jnp.bfloat16)],
        compiler_params=pltpu.CompilerParams(
            dimension_semantics=("arbitrary",), vmem_limit_bytes=VMEM_LIMIT),
        name="mix_ln1",
    )(oa, ob, x2, wo_b, gn_a, gn_b, ln_g, ln_b)


def _ffn_ln2_kernel(h_ref, hb_ref, w1_ref, wg_ref, w2_ref, p_ref, wple_ref, g_ref, b_ref, o_ref, gate_ref):
    j = pl.program_id(1)
    n_ff = D_FF // FFN_TF
    n_all = (D_FF + D_MODEL) // FFN_TF
    n_sub = FFN_TF // FFN_SUB

    def up(w_ref, c):
        return jnp.dot(hb_ref[...], w_ref[:, c * FFN_SUB:(c + 1) * FFN_SUB],
                       preferred_element_type=jnp.float32)

    def ffn_step(first):
        acts = []
        for c in range(n_sub):
            r = jnp.maximum(up(w1_ref, c), 0.0)
            acts.append((r * r).astype(jnp.bfloat16))
        for n in range(D_MODEL // FFN_SUB):
            cols = slice(n * FFN_SUB, (n + 1) * FFN_SUB)
            part = sum(jnp.dot(acts[c], w2_ref[c * FFN_SUB:(c + 1) * FFN_SUB, cols],
                               preferred_element_type=jnp.float32) for c in range(n_sub))
            if first:
                o_ref[:, cols] = ALPHA * h_ref[:, cols] + part
            else:
                o_ref[:, cols] += part

    @pl.when(j == 0)
    def _():
        ffn_step(True)

    @pl.when((j > 0) & (j < n_ff))
    def _():
        ffn_step(False)

    for t in range(n_all - n_ff):
        @pl.when(j == n_ff + t)
        def _():
            for c in range(n_sub):
                lo = t * FFN_TF + c * FFN_SUB
                gate_ref[:, lo:lo + FFN_SUB] = 0.5 * jnp.tanh(0.5 * up(wg_ref, c)) + 0.5

    @pl.when(j == n_all - 1)
    def _():
        ple = jnp.dot(p_ref[...].astype(jnp.bfloat16), wple_ref[...], preferred_element_type=jnp.float32)
        o_ref[...] = _layer_norm(o_ref[...] + ple * gate_ref[...], g_ref[...], b_ref[...])


def _ffn_ln2(h1, h1_b, w1_b, wg_b, w2_b, p2, wple_b, ln_g, ln_b):
    m = h1.shape[0]
    n_ff = D_FF // FFN_TF
    n_all = (D_FF + D_MODEL) // FFN_TF
    return pl.pallas_call(
        _ffn_ln2_kernel,
        grid=(m // FFN_TM, n_all),
        in_specs=[pl.BlockSpec((FFN_TM, D_MODEL), lambda i, j: (i, 0)),
                  pl.BlockSpec((FFN_TM, D_MODEL), lambda i, j: (i, 0)),
                  pl.BlockSpec((D_MODEL, FFN_TF), lambda i, j: (0, jnp.minimum(j, n_ff - 1))),
                  pl.BlockSpec((D_MODEL, FFN_TF), lambda i, j: (0, jnp.maximum(j - n_ff, 0))),
                  pl.BlockSpec((FFN_TF, D_MODEL), lambda i, j: (jnp.minimum(j, n_ff - 1), 0)),
                  pl.BlockSpec((FFN_TM, PLE_DIM), lambda i, j: (i, 0)),
                  pl.BlockSpec((PLE_DIM, D_MODEL), lambda i, j: (0, 0)),
                  pl.BlockSpec((1, D_MODEL), lambda i, j: (0, 0)),
                  pl.BlockSpec((1, D_MODEL), lambda i, j: (0, 0))],
        out_specs=pl.BlockSpec((FFN_TM, D_MODEL), lambda i, j: (i, 0)),
        out_shape=jax.ShapeDtypeStruct((m, D_MODEL), jnp.float32),
        scratch_shapes=[pltpu.VMEM((FFN_TM, D_MODEL), jnp.float32)],
        compiler_params=pltpu.CompilerParams(
            dimension_semantics=("arbitrary", "arbitrary"), vmem_limit_bytes=VMEM_LIMIT),
        name="ffn_ln2",
    )(h1, h1_b, w1_b, wg_b, w2_b, p2, wple_b, ln_g, ln_b)


def kernel(x, p, positions, w_in, sink_a, gn_a, gn_b, w_o, ln1_g, ln1_b, w1, w2, w_ple, w_ple_gate,
           ln2_g, ln2_b):
    batch, seq, d = x.shape
    m = batch * seq
    bf = jnp.bfloat16
    cos_t, sin_t = _rope_tables(positions)
    h = x.reshape(m, d)
    for i in range(DEPTH):
        proj = _in_proj(h, w_in[i].astype(bf), cos_t, sin_t)
        oa = _window_attn(proj, sink_a[i], batch=batch, seq=seq)
        ob = _dilated_attn(proj, batch=batch, seq=seq)
        h1, h1_b = _mix_ln1(oa, ob, h, w_o[i].astype(bf), gn_a[i].reshape(1, -1), gn_b[i].reshape(1, -1),
                      ln1_g[i].reshape(1, -1), ln1_b[i].reshape(1, -1))
        h = _ffn_ln2(h1, h1_b, w1[i].astype(bf), w_ple_gate[i].astype(bf), w2[i].astype(bf),
                     p[i].reshape(m, PLE_DIM), w_ple[i].astype(bf),
                     ln2_g[i].reshape(1, -1), ln2_b[i].reshape(1, -1))
    return h.reshape(batch, seq, d)
```

```python
import functools

import jax
import jax.numpy as jnp
from jax import lax
from jax.experimental import pallas as pl
from jax.experimental.pallas import tpu as pltpu

D_MODEL = 2048
HEAD_DIM = 128
A_HEADS = 8
A_KV_HEADS = 2
A_WINDOW = 128
B_HEADS = 8
DILATED_PATTERNS = ((128, 1), (512, 4), (2048, 16))
ROT_DIM = HEAD_DIM // 4
ROPE_THETA = 500000.0
D_FF = 4 * D_MODEL
PLE_DIM = 256
DEPTH = 1
ALPHA = (2.0 * DEPTH) ** 0.25
LN_EPS = 1e-5
RMS_EPS = 1e-6
NEG_INF = -1e30
LOG2_E = 1.4426950408889634

A_Q = A_HEADS * HEAD_DIM
A_KV = A_KV_HEADS * HEAD_DIM
B_QKV = B_HEADS * HEAD_DIM
D_IN = A_Q + 2 * A_KV + 3 * B_QKV
N_PROJ_HEADS = D_IN // HEAD_DIM
QA_HEAD0 = 0
KA_HEAD0 = A_HEADS
VA_HEAD0 = A_HEADS + A_KV_HEADS
QB_HEAD0 = A_HEADS + 2 * A_KV_HEADS
KB_HEAD0 = QB_HEAD0 + B_HEADS
VB_HEAD0 = KB_HEAD0 + B_HEADS

LANES = 128
MXU_COLS = 256
Q_BLOCK = 128
SCORE_LOOKAHEAD = 5
VMEM_LIMIT = 56 * 1024 * 1024

PROJ_TM = 1024
PROJ_TN = 1536
MIX_TM = 512
FFN_TM = 512
FFN_TF = 1024
FFN_SUB = 512
ROPE_TM = 2048
KIND_Q, KIND_K, KIND_V = 0, 1, 2


def _rope_table_kernel(pos_ref, freq_ref, cos_ref, sin_ref):
    ang = pos_ref[...].astype(jnp.float32) * freq_ref[...]
    lane = lax.broadcasted_iota(jnp.int32, ang.shape, 1)
    half = ROT_DIM // 2
    c = jnp.where(lane < ROT_DIM, jnp.cos(ang), 1.0)
    s = jnp.sin(ang)
    s = jnp.where(lane < half, -s, jnp.where(lane < ROT_DIM, s, 0.0))
    q_scale = LOG2_E * HEAD_DIM ** -0.5
    cos_ref[KIND_Q] = c * q_scale
    sin_ref[KIND_Q] = s * q_scale
    cos_ref[KIND_K] = c
    sin_ref[KIND_K] = s
    cos_ref[KIND_V] = jnp.ones_like(c)
    sin_ref[KIND_V] = jnp.zeros_like(s)


def _rope_tables(positions):
    m = positions.size
    half = ROT_DIM // 2
    inv_freq = ROPE_THETA ** (-jnp.arange(0, ROT_DIM, 2, dtype=jnp.float32) / ROT_DIM)
    freq = jnp.tile(inv_freq, LANES // half).reshape(1, LANES)
    pos = positions.reshape(m, 1)
    out = jax.ShapeDtypeStruct((3, m, LANES), jnp.float32)
    return pl.pallas_call(
        _rope_table_kernel,
        grid=(m // ROPE_TM,),
        in_specs=[pl.BlockSpec((ROPE_TM, 1), lambda i: (i, 0)),
                  pl.BlockSpec((1, LANES), lambda i: (0, 0))],
        out_specs=[pl.BlockSpec((3, ROPE_TM, LANES), lambda i: (0, i, 0))] * 2,
        out_shape=[out, out],
        name="rope_table",
    )(pos, freq)


def _in_proj_kernel(x_ref, w_ref, cos_ref, sin_ref, o_ref, xb_ref):
    j = pl.program_id(1)

    @pl.when(j == 0)
    def _():
        xb_ref[...] = x_ref[...].astype(jnp.bfloat16)

    half = ROT_DIM // 2
    lane = lax.broadcasted_iota(jnp.int32, (PROJ_TM, LANES), 1)
    partner = jnp.where(lane < half, lane + half, jnp.where(lane < ROT_DIM, lane - half, lane))
    heads_per_tile = PROJ_TN // HEAD_DIM
    heads_per_chunk = MXU_COLS // HEAD_DIM
    for c in range(PROJ_TN // MXU_COLS):
        acc = jnp.dot(xb_ref[...], w_ref[:, c * MXU_COLS:(c + 1) * MXU_COLS],
                      preferred_element_type=jnp.float32)
        for g in range(heads_per_chunk):
            local = c * heads_per_chunk + g
            head = j * heads_per_tile + local
            is_q = (head < KA_HEAD0) | ((head >= QB_HEAD0) & (head < KB_HEAD0))
            is_k = ((head >= KA_HEAD0) & (head < VA_HEAD0)) | ((head >= KB_HEAD0) & (head < VB_HEAD0))
            kind = jnp.where(is_q, KIND_Q, jnp.where(is_k, KIND_K, KIND_V))
            t = acc[:, g * HEAD_DIM:(g + 1) * HEAD_DIM]
            swapped = jnp.take_along_axis(t, partner, axis=1)
            o_ref[local] = t * cos_ref[kind] + swapped * sin_ref[kind]


def _in_proj(x2, w_in_b, cos_t, sin_t):
    m = x2.shape[0]
    heads_per_tile = PROJ_TN // HEAD_DIM
    return pl.pallas_call(
        _in_proj_kernel,
        grid=(m // PROJ_TM, D_IN // PROJ_TN),
        in_specs=[pl.BlockSpec((PROJ_TM, D_MODEL), lambda i, j: (i, 0)),
                  pl.BlockSpec((D_MODEL, PROJ_TN), lambda i, j: (0, j)),
                  pl.BlockSpec((3, PROJ_TM, LANES), lambda i, j: (0, i, 0)),
                  pl.BlockSpec((3, PROJ_TM, LANES), lambda i, j: (0, i, 0))],
        out_specs=pl.BlockSpec((heads_per_tile, PROJ_TM, HEAD_DIM), lambda i, j: (j, i, 0)),
        out_shape=jax.ShapeDtypeStruct((N_PROJ_HEADS, m, HEAD_DIM), jnp.float32),
        scratch_shapes=[pltpu.VMEM((PROJ_TM, D_MODEL), jnp.bfloat16)],
        compiler_params=pltpu.CompilerParams(
            dimension_semantics=("arbitrary", "arbitrary"), vmem_limit_bytes=VMEM_LIMIT),
        name="in_proj",
    )(x2, w_in_b, cos_t, sin_t)


def _band_bias(delta, rows, cols, halo):
    d = delta + lax.broadcasted_iota(jnp.int32, (rows, cols), 0) - lax.broadcasted_iota(jnp.int32, (rows, cols), 1)
    return jnp.where(jnp.abs(d) <= halo, 0.0, NEG_INF).astype(jnp.float32)


def _softmax_pv(s, v, sink=None):
    m = jnp.max(s, axis=-1, keepdims=True)
    if sink is not None:
        m = jnp.maximum(m, sink)
    e = jnp.exp2(s - m)
    acc = jnp.dot(e.astype(v.dtype), v, preferred_element_type=jnp.float32)
    if v.shape[1] == HEAD_DIM:
        o, denom = acc, jnp.sum(e, axis=-1, keepdims=True)
    else:
        o, denom = acc[:, :HEAD_DIM], acc[:, HEAD_DIM:]
    if sink is not None:
        denom = denom + jnp.exp2(sink - m)
    return o, m, denom


def _store_v_ones(vo_ref, v):
    vo_ref[:, :HEAD_DIM] = v.astype(vo_ref.dtype)
    vo_ref[:, HEAD_DIM:] = jnp.ones(v.shape, vo_ref.dtype)


def _qk(q, k):
    return lax.dot_general(q, k, (((1,), (1,)), ((), ())), preferred_element_type=jnp.float32)


def _window_start(q0, halo, window, length):
    return min(max(q0 - halo, 0), length - window)


def _run_pipelined(n, scores, finish, depth=SCORE_LOOKAHEAD):
    pending = {}
    for j in range(n + depth):
        if j < n:
            pending[j] = scores(j)
        if j >= depth:
            finish(j - depth, pending.pop(j - depth))


def _window_attn_kernel(q_ref, k_ref, v_ref, sink_ref, o_ref, qkv_ref, *, seq):
    window = Q_BLOCK + 2 * A_WINDOW
    head = pl.program_id(1)
    sink = sink_ref[head] * LOG2_E
    qkv_ref[0] = q_ref[...].astype(jnp.bfloat16)

    @pl.when(head % (A_HEADS // A_KV_HEADS) == 0)
    def _():
        qkv_ref[1] = k_ref[...].astype(jnp.bfloat16)
        qkv_ref[2] = v_ref[...].astype(jnp.bfloat16)

    biases = {}

    def scores(j):
        q0 = j * Q_BLOCK
        w0 = _window_start(q0, A_WINDOW, window, seq)
        if q0 - w0 not in biases:
            biases[q0 - w0] = _band_bias(q0 - w0, Q_BLOCK, window, A_WINDOW)
        return _qk(qkv_ref[0, q0:q0 + Q_BLOCK, :], qkv_ref[1, w0:w0 + window, :]) + biases[q0 - w0]

    def finish(j, s):
        q0 = j * Q_BLOCK
        w0 = _window_start(q0, A_WINDOW, window, seq)
        o, _, denom = _softmax_pv(s, qkv_ref[2, w0:w0 + window, :], sink)
        o_ref[q0:q0 + Q_BLOCK, :] = (o / denom).astype(o_ref.dtype)

    _run_pipelined(seq // Q_BLOCK, scores, finish)


def _window_attn(proj, sink, *, batch, seq):
    m = batch * seq
    grp = A_HEADS // A_KV_HEADS
    head_spec = lambda h0, div: pl.BlockSpec((None, seq, HEAD_DIM), lambda b, h: (h0 + h // div, b, 0))
    return pl.pallas_call(
        functools.partial(_window_attn_kernel, seq=seq),
        grid=(batch, A_HEADS),
        in_specs=[head_spec(QA_HEAD0, 1), head_spec(KA_HEAD0, grp), head_spec(VA_HEAD0, grp),
                  pl.BlockSpec(memory_space=pltpu.SMEM)],
        out_specs=pl.BlockSpec((seq, HEAD_DIM), lambda b, h: (b, h)),
        out_shape=jax.ShapeDtypeStruct((m, A_Q), jnp.bfloat16),
        scratch_shapes=[pltpu.VMEM((3, seq, HEAD_DIM), jnp.bfloat16)],
        compiler_params=pltpu.CompilerParams(
            dimension_semantics=("arbitrary", "arbitrary"), vmem_limit_bytes=VMEM_LIMIT),
        name="window_attn",
    )(proj, proj, proj, sink)


def _mix2(a, b):
    m = jnp.maximum(a[1], b[1])
    fa = jnp.exp2(a[1] - m)
    fb = jnp.exp2(b[1] - m)
    return fa * a[0] + fb * b[0], m, fa * a[2] + fb * b[2]


def _dilated_attn_kernel(q_ref, k_ref, v_ref, o_ref, nat_ref, ca_ref, cab_ref, cbb_ref, vn_ref, va_ref, vb_ref,
                         pc_ref, pn_ref, *, seq):
    bf = jnp.bfloat16
    (w1, r1), (wa, ra), (wb, rb) = sorted(DILATED_PATTERNS, key=lambda pat: pat[1])
    assert r1 == 1 and rb % ra == 0
    sub = rb // ra
    len_a, len_b = seq // ra, seq // rb
    halo1, halo_a, halo_b = w1 // 2, wa // (2 * ra), wb // (2 * rb)
    biases = {}

    def band_scores(qk_src, base, q0, w0, window, halo):
        key = (q0 - w0, Q_BLOCK, window, halo)
        if key not in biases:
            biases[key] = _band_bias(*key)
        return _qk(qk_src[0, base + q0:base + q0 + Q_BLOCK, :],
                   qk_src[1, base + w0:base + w0 + window, :]) + biases[key]

    def band_output(s, v_src, base, w0, window):
        o, m, denom = _softmax_pv(s, v_src[base + w0:base + w0 + window, :])
        return o, jnp.broadcast_to(m, (Q_BLOCK, LANES)), denom

    def blocks(length, halo):
        window = min(Q_BLOCK + 2 * halo, length)
        for j in range(length // Q_BLOCK):
            q0 = j * Q_BLOCK
            yield q0, _window_start(q0, halo, window, length), window

    def load3(ref, rows):
        return tuple(ref[t, rows, :] for t in range(3))

    def store3(ref, rows, triple):
        for t in range(3):
            ref[t, rows, :] = triple[t]

    for t, src in enumerate((q_ref, k_ref, v_ref)):
        if t < 2:
            nat_ref[t] = src[...].astype(bf)
        else:
            _store_v_ones(vn_ref, src[...])
        for c in range(ra):
            x = src[pl.ds(c, len_a, stride=ra), :]
            ca_ref[t, c * len_a:(c + 1) * len_a, :] = x
            if t < 2:
                cab_ref[t, c * len_a:(c + 1) * len_a, :] = x.astype(bf)
            else:
                _store_v_ones(va_ref.at[c * len_a:(c + 1) * len_a, :], x)
    for t in range(3):
        for c in range(ra):
            for u in range(sub):
                rows = slice((c * sub + u) * len_b, (c * sub + u + 1) * len_b)
                x = ca_ref[t, pl.ds(c * len_a + u, len_b, stride=sub), :]
                if t < 2:
                    cbb_ref[t, rows, :] = x.astype(bf)
                else:
                    _store_v_ones(vb_ref.at[rows, :], x)

    units = []

    def add_unit(qk_src, v_src, base, q0, w0, window, halo, finish):
        units.append((lambda: band_scores(qk_src, base, q0, w0, window, halo),
                      lambda s: finish(band_output(s, v_src, base, w0, window))))

    for c in range(ra):
        for u in range(sub):
            cu = c * sub + u
            for q0, w0, window in blocks(len_b, halo_b):
                def finish_b(triple, c=c, u=u, q0=q0):
                    store3(pc_ref, pl.ds(c * len_a + q0 * sub + u, Q_BLOCK, stride=sub), triple)
                add_unit(cbb_ref, vb_ref, cu * len_b, q0, w0, window, halo_b, finish_b)

    for c in range(ra):
        for q0, w0, window in blocks(len_a, halo_a):
            def finish_a(triple, c=c, q0=q0):
                merged = _mix2(triple, load3(pc_ref, slice(c * len_a + q0, c * len_a + q0 + Q_BLOCK)))
                store3(pn_ref, pl.ds(q0 * ra + c, Q_BLOCK, stride=ra), merged)
            add_unit(cab_ref, va_ref, c * len_a, q0, w0, window, halo_a, finish_a)

    for q0, w0, window in blocks(seq, halo1):
        def finish_1(triple, q0=q0):
            o, _, denom = _mix2(triple, load3(pn_ref, slice(q0, q0 + Q_BLOCK)))
            o_ref[q0:q0 + Q_BLOCK, :] = (o / denom).astype(o_ref.dtype)
        add_unit(nat_ref, vn_ref, 0, q0, w0, window, halo1, finish_1)

    _run_pipelined(len(units), lambda j: units[j][0](), lambda j, s: units[j][1](s))


def _dilated_attn(proj, *, batch, seq):
    m = batch * seq
    head_spec = lambda h0: pl.BlockSpec((None, seq, HEAD_DIM), lambda b, h: (h0 + h, b, 0))
    qk_buf = pltpu.VMEM((2, seq, HEAD_DIM), jnp.bfloat16)
    v_ones_buf = pltpu.VMEM((seq, 2 * HEAD_DIM), jnp.bfloat16)
    return pl.pallas_call(
        functools.partial(_dilated_attn_kernel, seq=seq),
        grid=(batch, B_HEADS),
        in_specs=[head_spec(QB_HEAD0), head_spec(KB_HEAD0), head_spec(VB_HEAD0)],
        out_specs=pl.BlockSpec((seq, HEAD_DIM), lambda b, h: (b, h)),
        out_shape=jax.ShapeDtypeStruct((m, B_QKV), jnp.bfloat16),
        scratch_shapes=[qk_buf,
                        pltpu.VMEM((3, seq, HEAD_DIM), jnp.float32),
                        qk_buf, qk_buf,
                        v_ones_buf, v_ones_buf, v_ones_buf]
                       + [pltpu.VMEM((3, seq, HEAD_DIM), jnp.float32)] * 2,
        compiler_params=pltpu.CompilerParams(
            dimension_semantics=("arbitrary", "arbitrary"), vmem_limit_bytes=VMEM_LIMIT),
        name="dilated_attn",
    )(proj, proj, proj)


def _layer_norm(y, g, b):
    mu = jnp.mean(y, axis=-1, keepdims=True)
    d = y - mu
    var = jnp.mean(d * d, axis=-1, keepdims=True)
    return d * lax.rsqrt(var + LN_EPS) * g + b


def _rms_norm(y, g):
    return y * lax.rsqrt(jnp.mean(y * y, axis=-1, keepdims=True) + RMS_EPS) * g


def _mix_ln1_kernel(oa_ref, ob_ref, x_ref, wo_ref, gna_ref, gnb_ref, g_ref, b_ref, h_ref):
    ya = _rms_norm(oa_ref[...].astype(jnp.float32), gna_ref[...])
    yb = _rms_norm(ob_ref[...].astype(jnp.float32), gnb_ref[...])
    mix = (jnp.dot(ya.astype(jnp.bfloat16), wo_ref[:A_Q, :], preferred_element_type=jnp.float32)
           + jnp.dot(yb.astype(jnp.bfloat16), wo_ref[A_Q:, :], preferred_element_type=jnp.float32))
    h_ref[...] = _layer_norm(ALPHA * x_ref[...] + mix, g_ref[...], b_ref[...])


def _mix_ln1(oa, ob, x2, wo_b, gn_a, gn_b, ln_g, ln_b):
    m = x2.shape[0]
    row = lambda w: pl.BlockSpec((MIX_TM, w), lambda i: (i, 0))
    full = lambda r, w: pl.BlockSpec((r, w), lambda i: (0, 0))
    return pl.pallas_call(
        _mix_ln1_kernel,
        grid=(m // MIX_TM,),
        in_specs=[row(A_Q), row(B_QKV), row(D_MODEL), full(D_MODEL, D_MODEL), full(1, A_Q),
                  full(1, B_QKV), full(1, D_MODEL), full(1, D_MODEL)],
        out_specs=row(D_MODEL),
        out_shape=jax.ShapeDtypeStruct((m, D_MODEL), jnp.float32),
        compiler_params=pltpu.CompilerParams(
            dimension_semantics=("arbitrary",), vmem_limit_bytes=VMEM_LIMIT),
        name="mix_ln1",
    )(oa, ob, x2, wo_b, gn_a, gn_b, ln_g, ln_b)


def _ffn_ln2_kernel(h_ref, w1_ref, wg_ref, w2_ref, p_ref, wple_ref, g_ref, b_ref, o_ref, hb_ref, gate_ref):
    j = pl.program_id(1)
    n_ff = D_FF // FFN_TF
    n_all = (D_FF + D_MODEL) // FFN_TF
    n_sub = FFN_TF // FFN_SUB

    @pl.when(j == 0)
    def _():
        h = h_ref[...]
        hb_ref[...] = h.astype(jnp.bfloat16)
        o_ref[...] = ALPHA * h

    def up(w_ref, c):
        return jnp.dot(hb_ref[...], w_ref[:, c * FFN_SUB:(c + 1) * FFN_SUB],
                       preferred_element_type=jnp.float32)

    @pl.when(j < n_ff)
    def _():
        acts = []
        for c in range(n_sub):
            r = jnp.maximum(up(w1_ref, c), 0.0)
            acts.append((r * r).astype(jnp.bfloat16))
        for n in range(D_MODEL // FFN_SUB):
            cols = slice(n * FFN_SUB, (n + 1) * FFN_SUB)
            part = sum(jnp.dot(acts[c], w2_ref[c * FFN_SUB:(c + 1) * FFN_SUB, cols],
                               preferred_element_type=jnp.float32) for c in range(n_sub))
            o_ref[:, cols] += part

    for t in range(n_all - n_ff):
        @pl.when(j == n_ff + t)
        def _():
            for c in range(n_sub):
                lo = t * FFN_TF + c * FFN_SUB
                gate_ref[:, lo:lo + FFN_SUB] = 0.5 * jnp.tanh(0.5 * up(wg_ref, c)) + 0.5

    @pl.when(j == n_all - 1)
    def _():
        ple = jnp.dot(p_ref[...].astype(jnp.bfloat16), wple_ref[...], preferred_element_type=jnp.float32)
        o_ref[...] = _layer_norm(o_ref[...] + ple * gate_ref[...], g_ref[...], b_ref[...])


def _ffn_ln2(h1, w1_b, wg_b, w2_b, p2, wple_b, ln_g, ln_b):
    m = h1.shape[0]
    n_ff = D_FF // FFN_TF
    n_all = (D_FF + D_MODEL) // FFN_TF
    return pl.pallas_call(
        _ffn_ln2_kernel,
        grid=(m // FFN_TM, n_all),
        in_specs=[pl.BlockSpec((FFN_TM, D_MODEL), lambda i, j: (i, 0)),
                  pl.BlockSpec((D_MODEL, FFN_TF), lambda i, j: (0, jnp.minimum(j, n_ff - 1))),
                  pl.BlockSpec((D_MODEL, FFN_TF), lambda i, j: (0, jnp.maximum(j - n_ff, 0))),
                  pl.BlockSpec((FFN_TF, D_MODEL), lambda i, j: (jnp.minimum(j, n_ff - 1), 0)),
                  pl.BlockSpec((FFN_TM, PLE_DIM), lambda i, j: (i, 0)),
                  pl.BlockSpec((PLE_DIM, D_MODEL), lambda i, j: (0, 0)),
                  pl.BlockSpec((1, D_MODEL), lambda i, j: (0, 0)),
                  pl.BlockSpec((1, D_MODEL), lambda i, j: (0, 0))],
        out_specs=pl.BlockSpec((FFN_TM, D_MODEL), lambda i, j: (i, 0)),
        out_shape=jax.ShapeDtypeStruct((m, D_MODEL), jnp.float32),
        scratch_shapes=[pltpu.VMEM((FFN_TM, D_MODEL), jnp.bfloat16),
                        pltpu.VMEM((FFN_TM, D_MODEL), jnp.float32)],
        compiler_params=pltpu.CompilerParams(
            dimension_semantics=("arbitrary", "arbitrary"), vmem_limit_bytes=VMEM_LIMIT),
        name="ffn_ln2",
    )(h1, w1_b, wg_b, w2_b, p2, wple_b, ln_g, ln_b)


def kernel(x, p, positions, w_in, sink_a, gn_a, gn_b, w_o, ln1_g, ln1_b, w1, w2, w_ple, w_ple_gate,
           ln2_g, ln2_b):
    batch, seq, d = x.shape
    m = batch * seq
    bf = jnp.bfloat16
    cos_t, sin_t = _rope_tables(positions)
    h = x.reshape(m, d)
    for i in range(DEPTH):
        proj = _in_proj(h, w_in[i].astype(bf), cos_t, sin_t)
        oa = _window_attn(proj, sink_a[i], batch=batch, seq=seq)
        ob = _dilated_attn(proj, batch=batch, seq=seq)
        h1 = _mix_ln1(oa, ob, h, w_o[i].astype(bf), gn_a[i].reshape(1, -1), gn_b[i].reshape(1, -1),
                      ln1_g[i].reshape(1, -1), ln1_b[i].reshape(1, -1))
        h = _ffn_ln2(h1, w1[i].astype(bf), w_ple_gate[i].astype(bf), w2[i].astype(bf),
                     p[i].reshape(m, PLE_DIM), w_ple[i].astype(bf),
                     ln2_g[i].reshape(1, -1), ln2_b[i].reshape(1, -1))
    return h.reshape(batch, seq, d)
```

```python
import functools

import jax
import jax.numpy as jnp
from jax import lax
from jax.experimental import pallas as pl
from jax.experimental.pallas import tpu as pltpu

D_MODEL = 2048
HEAD_DIM = 128
A_HEADS = 8
A_KV_HEADS = 2
A_WINDOW = 128
B_HEADS = 8
DILATED_PATTERNS = ((128, 1), (512, 4), (2048, 16))
ROT_DIM = HEAD_DIM // 4
ROPE_THETA = 500000.0
D_FF = 4 * D_MODEL
PLE_DIM = 256
DEPTH = 1
ALPHA = (2.0 * DEPTH) ** 0.25
LN_EPS = 1e-5
RMS_EPS = 1e-6
NEG_INF = -1e30
LOG2_E = 1.4426950408889634

A_Q = A_HEADS * HEAD_DIM
A_KV = A_KV_HEADS * HEAD_DIM
B_QKV = B_HEADS * HEAD_DIM
D_IN = A_Q + 2 * A_KV + 3 * B_QKV
N_PROJ_HEADS = D_IN // HEAD_DIM
QA_HEAD0 = 0
KA_HEAD0 = A_HEADS
VA_HEAD0 = A_HEADS + A_KV_HEADS
QB_HEAD0 = A_HEADS + 2 * A_KV_HEADS
KB_HEAD0 = QB_HEAD0 + B_HEADS
VB_HEAD0 = KB_HEAD0 + B_HEADS

LANES = 128
MXU_COLS = 256
Q_BLOCK = 128
SCORE_LOOKAHEAD = 5
VMEM_LIMIT = 56 * 1024 * 1024

PROJ_TM = 1024
PROJ_TN = 1536
MIX_TM = 512
MIX_ROWS = 256
FFN_TM = 512
FFN_TF = 1024
FFN_SUB = 512
FFN_LAST_ROWS = 256
ROPE_TM = 2048
KIND_Q, KIND_K, KIND_V = 0, 1, 2


def _rope_table_kernel(pos_ref, freq_ref, cos_ref, sin_ref):
    ang = pos_ref[...].astype(jnp.float32) * freq_ref[...]
    lane = lax.broadcasted_iota(jnp.int32, ang.shape, 1)
    half = ROT_DIM // 2
    c = jnp.where(lane < ROT_DIM, jnp.cos(ang), 1.0)
    s = jnp.sin(ang)
    s = jnp.where(lane < half, -s, jnp.where(lane < ROT_DIM, s, 0.0))
    q_scale = LOG2_E * HEAD_DIM ** -0.5
    cos_ref[KIND_Q] = c * q_scale
    sin_ref[KIND_Q] = s * q_scale
    cos_ref[KIND_K] = c
    sin_ref[KIND_K] = s
    cos_ref[KIND_V] = jnp.ones_like(c)
    sin_ref[KIND_V] = jnp.zeros_like(s)


def _rope_tables(positions):
    m = positions.size
    half = ROT_DIM // 2
    inv_freq = ROPE_THETA ** (-jnp.arange(0, ROT_DIM, 2, dtype=jnp.float32) / ROT_DIM)
    freq = jnp.tile(inv_freq, LANES // half).reshape(1, LANES)
    pos = positions.reshape(m, 1)
    out = jax.ShapeDtypeStruct((3, m, LANES), jnp.float32)
    return pl.pallas_call(
        _rope_table_kernel,
        grid=(m // ROPE_TM,),
        in_specs=[pl.BlockSpec((ROPE_TM, 1), lambda i: (i, 0)),
                  pl.BlockSpec((1, LANES), lambda i: (0, 0))],
        out_specs=[pl.BlockSpec((3, ROPE_TM, LANES), lambda i: (0, i, 0))] * 2,
        out_shape=[out, out],
        name="rope_table",
    )(pos, freq)


def _in_proj_kernel(x_ref, w_ref, cos_ref, sin_ref, o_ref, xb_ref):
    j = pl.program_id(1)

    @pl.when(j == 0)
    def _():
        xb_ref[...] = x_ref[...].astype(jnp.bfloat16)

    half = ROT_DIM // 2
    heads_per_tile = PROJ_TN // HEAD_DIM
    heads_per_chunk = MXU_COLS // HEAD_DIM
    n_chunks = PROJ_TN // MXU_COLS

    def chunk(c, rows, n_rows):
        lane = lax.broadcasted_iota(jnp.int32, (n_rows, LANES), 1)
        partner = jnp.where(lane < half, lane + half, jnp.where(lane < ROT_DIM, lane - half, lane))
        acc = jnp.dot(xb_ref[rows, :], w_ref[:, c * MXU_COLS:(c + 1) * MXU_COLS],
                      preferred_element_type=jnp.float32)
        for g in range(heads_per_chunk):
            local = c * heads_per_chunk + g
            head = j * heads_per_tile + local
            is_q = (head < KA_HEAD0) | ((head >= QB_HEAD0) & (head < KB_HEAD0))
            is_k = ((head >= KA_HEAD0) & (head < VA_HEAD0)) | ((head >= KB_HEAD0) & (head < VB_HEAD0))
            kind = jnp.where(is_q, KIND_Q, jnp.where(is_k, KIND_K, KIND_V))
            t = acc[:, g * HEAD_DIM:(g + 1) * HEAD_DIM]
            swapped = jnp.take_along_axis(t, partner, axis=1)
            o_ref[local, rows, :] = t * cos_ref[kind, rows, :] + swapped * sin_ref[kind, rows, :]

    for c in range(n_chunks - 2):
        chunk(c, slice(None), PROJ_TM)
    for r in range(2):
        for c in range(n_chunks - 2, n_chunks):
            chunk(c, slice(r * (PROJ_TM // 2), (r + 1) * (PROJ_TM // 2)), PROJ_TM // 2)


def _in_proj(x2, w_in_b, cos_t, sin_t):
    m = x2.shape[0]
    heads_per_tile = PROJ_TN // HEAD_DIM
    return pl.pallas_call(
        _in_proj_kernel,
        grid=(m // PROJ_TM, D_IN // PROJ_TN),
        in_specs=[pl.BlockSpec((PROJ_TM, D_MODEL), lambda i, j: (i, 0)),
                  pl.BlockSpec((D_MODEL, PROJ_TN), lambda i, j: (0, j)),
                  pl.BlockSpec((3, PROJ_TM, LANES), lambda i, j: (0, i, 0)),
                  pl.BlockSpec((3, PROJ_TM, LANES), lambda i, j: (0, i, 0))],
        out_specs=pl.BlockSpec((heads_per_tile, PROJ_TM, HEAD_DIM), lambda i, j: (j, i, 0)),
        out_shape=jax.ShapeDtypeStruct((N_PROJ_HEADS, m, HEAD_DIM), jnp.float32),
        scratch_shapes=[pltpu.VMEM((PROJ_TM, D_MODEL), jnp.bfloat16)],
        compiler_params=pltpu.CompilerParams(
            dimension_semantics=("arbitrary", "arbitrary"), vmem_limit_bytes=VMEM_LIMIT),
        name="in_proj",
    )(x2, w_in_b, cos_t, sin_t)


def _band_bias(delta, rows, cols, halo):
    d = delta + lax.broadcasted_iota(jnp.int32, (rows, cols), 0) - lax.broadcasted_iota(jnp.int32, (rows, cols), 1)
    return jnp.where(jnp.abs(d) <= halo, 0.0, NEG_INF).astype(jnp.float32)


def _softmax_pv(s, v, sink=None):
    m = jnp.max(s, axis=-1, keepdims=True)
    if sink is not None:
        m = jnp.maximum(m, sink)
    e = jnp.exp2(s - m)
    acc = jnp.dot(e.astype(v.dtype), v, preferred_element_type=jnp.float32)
    if v.shape[1] == HEAD_DIM:
        o, denom = acc, jnp.sum(e, axis=-1, keepdims=True)
    else:
        o, denom = acc[:, :HEAD_DIM], acc[:, HEAD_DIM:]
    if sink is not None:
        denom = denom + jnp.exp2(sink - m)
    return o, m, denom


def _store_v_ones(vo_ref, v):
    vo_ref[:, :HEAD_DIM] = v.astype(vo_ref.dtype)
    vo_ref[:, HEAD_DIM:] = jnp.ones(v.shape, vo_ref.dtype)


def _qk(q, k):
    return lax.dot_general(q, k, (((1,), (1,)), ((), ())), preferred_element_type=jnp.float32)


def _window_start(q0, halo, window, length):
    return min(max(q0 - halo, 0), length - window)


def _run_pipelined(n, scores, finish, depth=SCORE_LOOKAHEAD):
    pending = {}
    for j in range(n + depth):
        if j < n:
            pending[j] = scores(j)
        if j >= depth:
            finish(j - depth, pending.pop(j - depth))


def _window_attn_kernel(q_ref, k_ref, v_ref, sink_ref, o_ref, qkv_ref, *, seq):
    window = Q_BLOCK + 2 * A_WINDOW
    head = pl.program_id(1)
    sink = sink_ref[head] * LOG2_E
    qkv_ref[0] = q_ref[...].astype(jnp.bfloat16)

    @pl.when(head % (A_HEADS // A_KV_HEADS) == 0)
    def _():
        qkv_ref[1] = k_ref[...].astype(jnp.bfloat16)
        qkv_ref[2] = v_ref[...].astype(jnp.bfloat16)

    biases = {}

    def scores(j):
        q0 = j * Q_BLOCK
        w0 = _window_start(q0, A_WINDOW, window, seq)
        if q0 - w0 not in biases:
            biases[q0 - w0] = _band_bias(q0 - w0, Q_BLOCK, window, A_WINDOW)
        return _qk(qkv_ref[0, q0:q0 + Q_BLOCK, :], qkv_ref[1, w0:w0 + window, :]) + biases[q0 - w0]

    def finish(j, s):
        q0 = j * Q_BLOCK
        w0 = _window_start(q0, A_WINDOW, window, seq)
        o, _, denom = _softmax_pv(s, qkv_ref[2, w0:w0 + window, :], sink)
        o_ref[q0:q0 + Q_BLOCK, :] = (o / denom).astype(o_ref.dtype)

    _run_pipelined(seq // Q_BLOCK, scores, finish)


def _window_attn(proj, sink, *, batch, seq):
    m = batch * seq
    grp = A_HEADS // A_KV_HEADS
    head_spec = lambda h0, div: pl.BlockSpec((None, seq, HEAD_DIM), lambda b, h: (h0 + h // div, b, 0))
    return pl.pallas_call(
        functools.partial(_window_attn_kernel, seq=seq),
        grid=(batch, A_HEADS),
        in_specs=[head_spec(QA_HEAD0, 1), head_spec(KA_HEAD0, grp), head_spec(VA_HEAD0, grp),
                  pl.BlockSpec(memory_space=pltpu.SMEM)],
        out_specs=pl.BlockSpec((seq, HEAD_DIM), lambda b, h: (b, h)),
        out_shape=jax.ShapeDtypeStruct((m, A_Q), jnp.bfloat16),
        scratch_shapes=[pltpu.VMEM((3, seq, HEAD_DIM), jnp.bfloat16)],
        compiler_params=pltpu.CompilerParams(
            dimension_semantics=("arbitrary", "arbitrary"), vmem_limit_bytes=VMEM_LIMIT),
        name="window_attn",
    )(proj, proj, proj, sink)


def _mix2(a, b):
    m = jnp.maximum(a[1], b[1])
    fa = jnp.exp2(a[1] - m)
    fb = jnp.exp2(b[1] - m)
    return fa * a[0] + fb * b[0], m, fa * a[2] + fb * b[2]


def _dilated_attn_kernel(q_ref, k_ref, v_ref, o_ref, nat_ref, ca_ref, cab_ref, cbb_ref, vn_ref, va_ref, vb_ref,
                         pc_ref, pn_ref, *, seq):
    bf = jnp.bfloat16
    (w1, r1), (wa, ra), (wb, rb) = sorted(DILATED_PATTERNS, key=lambda pat: pat[1])
    assert r1 == 1 and rb % ra == 0
    sub = rb // ra
    len_a, len_b = seq // ra, seq // rb
    halo1, halo_a, halo_b = w1 // 2, wa // (2 * ra), wb // (2 * rb)
    biases = {}

    def band_scores(qk_src, base, q0, w0, window, halo):
        key = (q0 - w0, Q_BLOCK, window, halo)
        if key not in biases:
            biases[key] = _band_bias(*key)
        return _qk(qk_src[0, base + q0:base + q0 + Q_BLOCK, :],
                   qk_src[1, base + w0:base + w0 + window, :]) + biases[key]

    def band_output(s, v_src, base, w0, window):
        o, m, denom = _softmax_pv(s, v_src[base + w0:base + w0 + window, :])
        return o, jnp.broadcast_to(m, (Q_BLOCK, LANES)), denom

    def blocks(length, halo):
        window = min(Q_BLOCK + 2 * halo, length)
        for j in range(length // Q_BLOCK):
            q0 = j * Q_BLOCK
            yield q0, _window_start(q0, halo, window, length), window

    def load3(ref, rows):
        return tuple(ref[t, rows, :] for t in range(3))

    def store3(ref, rows, triple):
        for t in range(3):
            ref[t, rows, :] = triple[t]

    for t, src in enumerate((q_ref, k_ref, v_ref)):
        if t < 2:
            nat_ref[t] = src[...].astype(bf)
        else:
            _store_v_ones(vn_ref, src[...])
        for c in range(ra):
            x = src[pl.ds(c, len_a, stride=ra), :]
            ca_ref[t, c * len_a:(c + 1) * len_a, :] = x
            if t < 2:
                cab_ref[t, c * len_a:(c + 1) * len_a, :] = x.astype(bf)
            else:
                _store_v_ones(va_ref.at[c * len_a:(c + 1) * len_a, :], x)
    for t in range(3):
        for c in range(ra):
            for u in range(sub):
                rows = slice((c * sub + u) * len_b, (c * sub + u + 1) * len_b)
                x = ca_ref[t, pl.ds(c * len_a + u, len_b, stride=sub), :]
                if t < 2:
                    cbb_ref[t, rows, :] = x.astype(bf)
                else:
                    _store_v_ones(vb_ref.at[rows, :], x)

    units = []

    def add_unit(qk_src, v_src, base, q0, w0, window, halo, finish):
        units.append((lambda: band_scores(qk_src, base, q0, w0, window, halo),
                      lambda s: finish(band_output(s, v_src, base, w0, window))))

    for c in range(ra):
        for u in range(sub):
            cu = c * sub + u
            for q0, w0, window in blocks(len_b, halo_b):
                def finish_b(triple, c=c, u=u, q0=q0):
                    store3(pc_ref, pl.ds(c * len_a + q0 * sub + u, Q_BLOCK, stride=sub), triple)
                add_unit(cbb_ref, vb_ref, cu * len_b, q0, w0, window, halo_b, finish_b)

    for c in range(ra):
        for q0, w0, window in blocks(len_a, halo_a):
            def finish_a(triple, c=c, q0=q0):
                merged = _mix2(triple, load3(pc_ref, slice(c * len_a + q0, c * len_a + q0 + Q_BLOCK)))
                store3(pn_ref, pl.ds(q0 * ra + c, Q_BLOCK, stride=ra), merged)
            add_unit(cab_ref, va_ref, c * len_a, q0, w0, window, halo_a, finish_a)

    for q0, w0, window in blocks(seq, halo1):
        def finish_1(triple, q0=q0):
            o, _, denom = _mix2(triple, load3(pn_ref, slice(q0, q0 + Q_BLOCK)))
            o_ref[q0:q0 + Q_BLOCK, :] = (o / denom).astype(o_ref.dtype)
        add_unit(nat_ref, vn_ref, 0, q0, w0, window, halo1, finish_1)

    _run_pipelined(len(units), lambda j: units[j][0](), lambda j, s: units[j][1](s))


def _dilated_attn(proj, *, batch, seq):
    m = batch * seq
    head_spec = lambda h0: pl.BlockSpec((None, seq, HEAD_DIM), lambda b, h: (h0 + h, b, 0))
    qk_buf = pltpu.VMEM((2, seq, HEAD_DIM), jnp.bfloat16)
    v_ones_buf = pltpu.VMEM((seq, 2 * HEAD_DIM), jnp.bfloat16)
    return pl.pallas_call(
        functools.partial(_dilated_attn_kernel, seq=seq),
        grid=(batch, B_HEADS),
        in_specs=[head_spec(QB_HEAD0), head_spec(KB_HEAD0), head_spec(VB_HEAD0)],
        out_specs=pl.BlockSpec((seq, HEAD_DIM), lambda b, h: (b, h)),
        out_shape=jax.ShapeDtypeStruct((m, B_QKV), jnp.bfloat16),
        scratch_shapes=[qk_buf,
                        pltpu.VMEM((3, seq, HEAD_DIM), jnp.float32),
                        qk_buf, qk_buf,
                        v_ones_buf, v_ones_buf, v_ones_buf]
                       + [pltpu.VMEM((3, seq, HEAD_DIM), jnp.float32)] * 2,
        compiler_params=pltpu.CompilerParams(
            dimension_semantics=("arbitrary", "arbitrary"), vmem_limit_bytes=VMEM_LIMIT),
        name="dilated_attn",
    )(proj, proj, proj)


def _layer_norm(y, g, b):
    mu = jnp.mean(y, axis=-1, keepdims=True)
    d = y - mu
    var = jnp.mean(d * d, axis=-1, keepdims=True)
    return d * lax.rsqrt(var + LN_EPS) * g + b


def _rms_norm(y, g):
    return y * lax.rsqrt(jnp.mean(y * y, axis=-1, keepdims=True) + RMS_EPS) * g


def _mix_ln1_kernel(oa_ref, ob_ref, x_ref, wo_ref, gna_ref, gnb_ref, g_ref, b_ref, h_ref):
    for r in range(MIX_TM // MIX_ROWS):
        rows = slice(r * MIX_ROWS, (r + 1) * MIX_ROWS)
        ya = _rms_norm(oa_ref[rows, :].astype(jnp.float32), gna_ref[...])
        yb = _rms_norm(ob_ref[rows, :].astype(jnp.float32), gnb_ref[...])
        mix = (jnp.dot(ya.astype(jnp.bfloat16), wo_ref[:A_Q, :], preferred_element_type=jnp.float32)
               + jnp.dot(yb.astype(jnp.bfloat16), wo_ref[A_Q:, :], preferred_element_type=jnp.float32))
        h_ref[rows, :] = _layer_norm(ALPHA * x_ref[rows, :] + mix, g_ref[...], b_ref[...])


def _mix_ln1(oa, ob, x2, wo_b, gn_a, gn_b, ln_g, ln_b):
    m = x2.shape[0]
    row = lambda w: pl.BlockSpec((MIX_TM, w), lambda i: (i, 0))
    full = lambda r, w: pl.BlockSpec((r, w), lambda i: (0, 0))
    return pl.pallas_call(
        _mix_ln1_kernel,
        grid=(m // MIX_TM,),
        in_specs=[row(A_Q), row(B_QKV), row(D_MODEL), full(D_MODEL, D_MODEL), full(1, A_Q),
                  full(1, B_QKV), full(1, D_MODEL), full(1, D_MODEL)],
        out_specs=row(D_MODEL),
        out_shape=jax.ShapeDtypeStruct((m, D_MODEL), jnp.float32),
        compiler_params=pltpu.CompilerParams(
            dimension_semantics=("arbitrary",), vmem_limit_bytes=VMEM_LIMIT),
        name="mix_ln1",
    )(oa, ob, x2, wo_b, gn_a, gn_b, ln_g, ln_b)


def _ffn_ln2_kernel(h_ref, w1g_ref, w2_ref, p_ref, wple_ref, g_ref, b_ref, o_ref, hb_ref, gate_ref):
    j = pl.program_id(1)
    n_ff = D_FF // FFN_TF
    n_all = (D_FF + D_MODEL) // FFN_TF
    n_sub = FFN_TF // FFN_SUB

    @pl.when(j == 0)
    def _():
        h = h_ref[...]
        hb_ref[...] = h.astype(jnp.bfloat16)
        o_ref[...] = ALPHA * h

    def up(c):
        return jnp.dot(hb_ref[...], w1g_ref[:, c * FFN_SUB:(c + 1) * FFN_SUB],
                       preferred_element_type=jnp.float32)

    @pl.when(j < n_ff)
    def _():
        acts = []
        for c in range(n_sub):
            r = jnp.maximum(up(c), 0.0)
            acts.append((r * r).astype(jnp.bfloat16))
        for n in range(D_MODEL // FFN_SUB):
            cols = slice(n * FFN_SUB, (n + 1) * FFN_SUB)
            part = sum(jnp.dot(acts[c], w2_ref[c * FFN_SUB:(c + 1) * FFN_SUB, cols],
                               preferred_element_type=jnp.float32) for c in range(n_sub))
            o_ref[:, cols] += part

    def gate_chunk(t, c, rows):
        lo = t * FFN_TF + c * FFN_SUB
        a = jnp.dot(hb_ref[rows, :], w1g_ref[:, c * FFN_SUB:(c + 1) * FFN_SUB], preferred_element_type=jnp.float32)
        gate_ref[rows, lo:lo + FFN_SUB] = 0.5 * jnp.tanh(0.5 * a) + 0.5

    for t in range(n_all - n_ff - 1):
        @pl.when(j == n_ff + t)
        def _():
            for c in range(n_sub):
                gate_chunk(t, c, slice(None))

    @pl.when(j == n_all - 1)
    def _():
        for r in range(FFN_TM // FFN_LAST_ROWS):
            rows = slice(r * FFN_LAST_ROWS, (r + 1) * FFN_LAST_ROWS)
            for c in range(n_sub):
                gate_chunk(n_all - n_ff - 1, c, rows)
            ple = jnp.dot(p_ref[rows, :].astype(jnp.bfloat16), wple_ref[...], preferred_element_type=jnp.float32)
            o_ref[rows, :] = _layer_norm(o_ref[rows, :] + ple * gate_ref[rows, :], g_ref[...], b_ref[...])


def _ffn_ln2(h1, w1g_b, w2_b, p2, wple_b, ln_g, ln_b):
    m = h1.shape[0]
    n_ff = D_FF // FFN_TF
    n_all = (D_FF + D_MODEL) // FFN_TF
    return pl.pallas_call(
        _ffn_ln2_kernel,
        grid=(m // FFN_TM, n_all),
        in_specs=[pl.BlockSpec((FFN_TM, D_MODEL), lambda i, j: (i, 0)),
                  pl.BlockSpec((D_MODEL, FFN_TF), lambda i, j: (0, j)),
                  pl.BlockSpec((FFN_TF, D_MODEL), lambda i, j: (jnp.minimum(j, n_ff - 1), 0)),
                  pl.BlockSpec((FFN_TM, PLE_DIM), lambda i, j: (i, 0)),
                  pl.BlockSpec((PLE_DIM, D_MODEL), lambda i, j: (0, 0)),
                  pl.BlockSpec((1, D_MODEL), lambda i, j: (0, 0)),
                  pl.BlockSpec((1, D_MODEL), lambda i, j: (0, 0))],
        out_specs=pl.BlockSpec((FFN_TM, D_MODEL), lambda i, j: (i, 0)),
        out_shape=jax.ShapeDtypeStruct((m, D_MODEL), jnp.float32),
        scratch_shapes=[pltpu.VMEM((FFN_TM, D_MODEL), jnp.bfloat16),
                        pltpu.VMEM((FFN_TM, D_MODEL), jnp.float32)],
        compiler_params=pltpu.CompilerParams(
            dimension_semantics=("arbitrary", "arbitrary"), vmem_limit_bytes=VMEM_LIMIT),
        name="ffn_ln2",
    )(h1, w1g_b, w2_b, p2, wple_b, ln_g, ln_b)


def kernel(x, p, positions, w_in, sink_a, gn_a, gn_b, w_o, ln1_g, ln1_b, w1, w2, w_ple, w_ple_gate,
           ln2_g, ln2_b):
    batch, seq, d = x.shape
    m = batch * seq
    bf = jnp.bfloat16
    cos_t, sin_t = _rope_tables(positions)
    h = x.reshape(m, d)
    for i in range(DEPTH):
        proj = _in_proj(h, w_in[i].astype(bf), cos_t, sin_t)
        oa = _window_attn(proj, sink_a[i], batch=batch, seq=seq)
        ob = _dilated_attn(proj, batch=batch, seq=seq)
        h1 = _mix_ln1(oa, ob, h, w_o[i].astype(bf), gn_a[i].reshape(1, -1), gn_b[i].reshape(1, -1),
                      ln1_g[i].reshape(1, -1), ln1_b[i].reshape(1, -1))
        w1g = jnp.concatenate([w1[i], w_ple_gate[i]], axis=1).astype(bf)
        h = _ffn_ln2(h1, w1g, w2[i].astype(bf), p[i].reshape(m, PLE_DIM), w_ple[i].astype(bf),
                     ln2_g[i].reshape(1, -1), ln2_b[i].reshape(1, -1))
    return h.reshape(batch, seq, d)
```

```python
import functools

import jax
import jax.numpy as jnp
from jax import lax
from jax.experimental import pallas as pl
from jax.experimental.pallas import tpu as pltpu

D_MODEL = 2048
HEAD_DIM = 128
A_HEADS = 8
A_KV_HEADS = 2
A_WINDOW = 128
B_HEADS = 8
DILATED_PATTERNS = ((128, 1), (512, 4), (2048, 16))
ROT_DIM = HEAD_DIM // 4
ROPE_THETA = 500000.0
D_FF = 4 * D_MODEL
PLE_DIM = 256
DEPTH = 1
ALPHA = (2.0 * DEPTH) ** 0.25
LN_EPS = 1e-5
RMS_EPS = 1e-6
NEG_INF = -1e30
LOG2_E = 1.4426950408889634

A_Q = A_HEADS * HEAD_DIM
A_KV = A_KV_HEADS * HEAD_DIM
B_QKV = B_HEADS * HEAD_DIM
D_IN = A_Q + 2 * A_KV + 3 * B_QKV
N_PROJ_HEADS = D_IN // HEAD_DIM
QA_HEAD0 = 0
KA_HEAD0 = A_HEADS
VA_HEAD0 = A_HEADS + A_KV_HEADS
QB_HEAD0 = A_HEADS + 2 * A_KV_HEADS
KB_HEAD0 = QB_HEAD0 + B_HEADS
VB_HEAD0 = KB_HEAD0 + B_HEADS

LANES = 128
MXU_COLS = 256
Q_BLOCK = 128
SCORE_LOOKAHEAD = 5
VMEM_LIMIT = 56 * 1024 * 1024

PROJ_TM = 1024
PROJ_TN = 1536
MIX_TM = 512
MIX_ROWS = 256
FFN_TM = 512
FFN_TF = 1024
FFN_SUB = 512
ROPE_TM = 2048
KIND_Q, KIND_K, KIND_V = 0, 1, 2


def _rope_table_kernel(pos_ref, freq_ref, cos_ref, sin_ref):
    ang = pos_ref[...].astype(jnp.float32) * freq_ref[...]
    lane = lax.broadcasted_iota(jnp.int32, ang.shape, 1)
    half = ROT_DIM // 2
    c = jnp.where(lane < ROT_DIM, jnp.cos(ang), 1.0)
    s = jnp.sin(ang)
    s = jnp.where(lane < half, -s, jnp.where(lane < ROT_DIM, s, 0.0))
    q_scale = LOG2_E * HEAD_DIM ** -0.5
    cos_ref[KIND_Q] = c * q_scale
    sin_ref[KIND_Q] = s * q_scale
    cos_ref[KIND_K] = c
    sin_ref[KIND_K] = s
    cos_ref[KIND_V] = jnp.ones_like(c)
    sin_ref[KIND_V] = jnp.zeros_like(s)


def _rope_tables(positions):
    m = positions.size
    half = ROT_DIM // 2
    inv_freq = ROPE_THETA ** (-jnp.arange(0, ROT_DIM, 2, dtype=jnp.float32) / ROT_DIM)
    freq = jnp.tile(inv_freq, LANES // half).reshape(1, LANES)
    pos = positions.reshape(m, 1)
    out = jax.ShapeDtypeStruct((3, m, LANES), jnp.float32)
    return pl.pallas_call(
        _rope_table_kernel,
        grid=(m // ROPE_TM,),
        in_specs=[pl.BlockSpec((ROPE_TM, 1), lambda i: (i, 0)),
                  pl.BlockSpec((1, LANES), lambda i: (0, 0))],
        out_specs=[pl.BlockSpec((3, ROPE_TM, LANES), lambda i: (0, i, 0))] * 2,
        out_shape=[out, out],
        name="rope_table",
    )(pos, freq)


def _in_proj_kernel(x_ref, w_ref, cos_ref, sin_ref, o_ref, xb_ref):
    j = pl.program_id(1)

    @pl.when(j == 0)
    def _():
        xb_ref[...] = x_ref[...].astype(jnp.bfloat16)

    half = ROT_DIM // 2
    heads_per_tile = PROJ_TN // HEAD_DIM
    heads_per_chunk = MXU_COLS // HEAD_DIM
    n_chunks = PROJ_TN // MXU_COLS

    def chunk(c, rows, n_rows):
        lane = lax.broadcasted_iota(jnp.int32, (n_rows, LANES), 1)
        partner = jnp.where(lane < half, lane + half, jnp.where(lane < ROT_DIM, lane - half, lane))
        acc = jnp.dot(xb_ref[rows, :], w_ref[:, c * MXU_COLS:(c + 1) * MXU_COLS],
                      preferred_element_type=jnp.float32)
        for g in range(heads_per_chunk):
            local = c * heads_per_chunk + g
            head = j * heads_per_tile + local
            is_q = (head < KA_HEAD0) | ((head >= QB_HEAD0) & (head < KB_HEAD0))
            is_k = ((head >= KA_HEAD0) & (head < VA_HEAD0)) | ((head >= KB_HEAD0) & (head < VB_HEAD0))
            kind = jnp.where(is_q, KIND_Q, jnp.where(is_k, KIND_K, KIND_V))
            t = acc[:, g * HEAD_DIM:(g + 1) * HEAD_DIM]
            swapped = jnp.take_along_axis(t, partner, axis=1)
            o_ref[local, rows, :] = t * cos_ref[kind, rows, :] + swapped * sin_ref[kind, rows, :]

    for c in range(n_chunks - 2):
        chunk(c, slice(None), PROJ_TM)
    for r in range(2):
        for c in range(n_chunks - 2, n_chunks):
            chunk(c, slice(r * (PROJ_TM // 2), (r + 1) * (PROJ_TM // 2)), PROJ_TM // 2)


def _in_proj(x2, w_in_b, cos_t, sin_t):
    m = x2.shape[0]
    heads_per_tile = PROJ_TN // HEAD_DIM
    return pl.pallas_call(
        _in_proj_kernel,
        grid=(m // PROJ_TM, D_IN // PROJ_TN),
        in_specs=[pl.BlockSpec((PROJ_TM, D_MODEL), lambda i, j: (i, 0)),
                  pl.BlockSpec((D_MODEL, PROJ_TN), lambda i, j: (0, j)),
                  pl.BlockSpec((3, PROJ_TM, LANES), lambda i, j: (0, i, 0)),
                  pl.BlockSpec((3, PROJ_TM, LANES), lambda i, j: (0, i, 0))],
        out_specs=pl.BlockSpec((heads_per_tile, PROJ_TM, HEAD_DIM), lambda i, j: (j, i, 0)),
        out_shape=jax.ShapeDtypeStruct((N_PROJ_HEADS, m, HEAD_DIM), jnp.float32),
        scratch_shapes=[pltpu.VMEM((PROJ_TM, D_MODEL), jnp.bfloat16)],
        compiler_params=pltpu.CompilerParams(
            dimension_semantics=("arbitrary", "arbitrary"), vmem_limit_bytes=VMEM_LIMIT),
        name="in_proj",
    )(x2, w_in_b, cos_t, sin_t)


def _band_bias(delta, rows, cols, halo):
    d = delta + lax.broadcasted_iota(jnp.int32, (rows, cols), 0) - lax.broadcasted_iota(jnp.int32, (rows, cols), 1)
    return jnp.where(jnp.abs(d) <= halo, 0.0, NEG_INF).astype(jnp.float32)


def _softmax_pv(s, v, sink=None):
    m = jnp.max(s, axis=-1, keepdims=True)
    if sink is not None:
        m = jnp.maximum(m, sink)
    e = jnp.exp2(s - m)
    acc = jnp.dot(e.astype(v.dtype), v, preferred_element_type=jnp.float32)
    if v.shape[1] == HEAD_DIM:
        o, denom = acc, jnp.sum(e, axis=-1, keepdims=True)
    else:
        o, denom = acc[:, :HEAD_DIM], acc[:, HEAD_DIM:]
    if sink is not None:
        denom = denom + jnp.exp2(sink - m)
    return o, m, denom


def _store_v_ones(vo_ref, v):
    vo_ref[:, :HEAD_DIM] = v.astype(vo_ref.dtype)
    vo_ref[:, HEAD_DIM:] = jnp.ones(v.shape, vo_ref.dtype)


def _qk(q, k):
    return lax.dot_general(q, k, (((1,), (1,)), ((), ())), preferred_element_type=jnp.float32)


def _window_start(q0, halo, window, length):
    return min(max(q0 - halo, 0), length - window)


def _run_pipelined(n, scores, finish, depth=SCORE_LOOKAHEAD):
    pending = {}
    for j in range(n + depth):
        if j < n:
            pending[j] = scores(j)
        if j >= depth:
            finish(j - depth, pending.pop(j - depth))


def _window_attn_kernel(q_ref, k_ref, v_ref, sink_ref, o_ref, qkv_ref, *, seq):
    window = Q_BLOCK + 2 * A_WINDOW
    head = pl.program_id(1)
    sink = sink_ref[head] * LOG2_E
    qkv_ref[0] = q_ref[...].astype(jnp.bfloat16)

    @pl.when(head % (A_HEADS // A_KV_HEADS) == 0)
    def _():
        qkv_ref[1] = k_ref[...].astype(jnp.bfloat16)
        qkv_ref[2] = v_ref[...].astype(jnp.bfloat16)

    biases = {}

    def scores(j):
        q0 = j * Q_BLOCK
        w0 = _window_start(q0, A_WINDOW, window, seq)
        if q0 - w0 not in biases:
            biases[q0 - w0] = _band_bias(q0 - w0, Q_BLOCK, window, A_WINDOW)
        return _qk(qkv_ref[0, q0:q0 + Q_BLOCK, :], qkv_ref[1, w0:w0 + window, :]) + biases[q0 - w0]

    def finish(j, s):
        q0 = j * Q_BLOCK
        w0 = _window_start(q0, A_WINDOW, window, seq)
        o, _, denom = _softmax_pv(s, qkv_ref[2, w0:w0 + window, :], sink)
        o_ref[q0:q0 + Q_BLOCK, :] = (o / denom).astype(o_ref.dtype)

    _run_pipelined(seq // Q_BLOCK, scores, finish)


def _window_attn(proj, sink, *, batch, seq):
    m = batch * seq
    grp = A_HEADS // A_KV_HEADS
    head_spec = lambda h0, div: pl.BlockSpec((None, seq, HEAD_DIM), lambda b, h: (h0 + h // div, b, 0))
    return pl.pallas_call(
        functools.partial(_window_attn_kernel, seq=seq),
        grid=(batch, A_HEADS),
        in_specs=[head_spec(QA_HEAD0, 1), head_spec(KA_HEAD0, grp), head_spec(VA_HEAD0, grp),
                  pl.BlockSpec(memory_space=pltpu.SMEM)],
        out_specs=pl.BlockSpec((seq, HEAD_DIM), lambda b, h: (b, h)),
        out_shape=jax.ShapeDtypeStruct((m, A_Q), jnp.bfloat16),
        scratch_shapes=[pltpu.VMEM((3, seq, HEAD_DIM), jnp.bfloat16)],
        compiler_params=pltpu.CompilerParams(
            dimension_semantics=("arbitrary", "arbitrary"), vmem_limit_bytes=VMEM_LIMIT),
        name="window_attn",
    )(proj, proj, proj, sink)


def _mix2(a, b):
    m = jnp.maximum(a[1], b[1])
    fa = jnp.exp2(a[1] - m)
    fb = jnp.exp2(b[1] - m)
    return fa * a[0] + fb * b[0], m, fa * a[2] + fb * b[2]


def _dilated_attn_kernel(q_ref, k_ref, v_ref, o_ref, nat_ref, ca_ref, cab_ref, cbb_ref, vn_ref, va_ref, vb_ref,
                         pc_ref, pn_ref, *, seq):
    bf = jnp.bfloat16
    (w1, r1), (wa, ra), (wb, rb) = sorted(DILATED_PATTERNS, key=lambda pat: pat[1])
    assert r1 == 1 and rb % ra == 0
    sub = rb // ra
    len_a, len_b = seq // ra, seq // rb
    halo1, halo_a, halo_b = w1 // 2, wa // (2 * ra), wb // (2 * rb)
    biases = {}

    def band_scores(qk_src, base, q0, w0, window, halo):
        key = (q0 - w0, Q_BLOCK, window, halo)
        if key not in biases:
            biases[key] = _band_bias(*key)
        return _qk(qk_src[0, base + q0:base + q0 + Q_BLOCK, :],
                   qk_src[1, base + w0:base + w0 + window, :]) + biases[key]

    def band_output(s, v_src, base, w0, window):
        o, m, denom = _softmax_pv(s, v_src[base + w0:base + w0 + window, :])
        return o, jnp.broadcast_to(m, (Q_BLOCK, LANES)), denom

    def blocks(length, halo):
        window = min(Q_BLOCK + 2 * halo, length)
        for j in range(length // Q_BLOCK):
            q0 = j * Q_BLOCK
            yield q0, _window_start(q0, halo, window, length), window

    def load3(ref, rows):
        return tuple(ref[t, rows, :] for t in range(3))

    def store3(ref, rows, triple):
        for t in range(3):
            ref[t, rows, :] = triple[t]

    for t, src in enumerate((q_ref, k_ref, v_ref)):
        if t < 2:
            nat_ref[t] = src[...].astype(bf)
        else:
            _store_v_ones(vn_ref, src[...])
        for c in range(ra):
            x = src[pl.ds(c, len_a, stride=ra), :]
            ca_ref[t, c * len_a:(c + 1) * len_a, :] = x
            if t < 2:
                cab_ref[t, c * len_a:(c + 1) * len_a, :] = x.astype(bf)
            else:
                _store_v_ones(va_ref.at[c * len_a:(c + 1) * len_a, :], x)
    for t in range(3):
        for c in range(ra):
            for u in range(sub):
                rows = slice((c * sub + u) * len_b, (c * sub + u + 1) * len_b)
                x = ca_ref[t, pl.ds(c * len_a + u, len_b, stride=sub), :]
                if t < 2:
                    cbb_ref[t, rows, :] = x.astype(bf)
                else:
                    _store_v_ones(vb_ref.at[rows, :], x)

    units = []

    def add_unit(qk_src, v_src, base, q0, w0, window, halo, finish):
        units.append((lambda: band_scores(qk_src, base, q0, w0, window, halo),
                      lambda s: finish(band_output(s, v_src, base, w0, window))))

    for c in range(ra):
        for u in range(sub):
            cu = c * sub + u
            for q0, w0, window in blocks(len_b, halo_b):
                def finish_b(triple, c=c, u=u, q0=q0):
                    store3(pc_ref, pl.ds(c * len_a + q0 * sub + u, Q_BLOCK, stride=sub), triple)
                add_unit(cbb_ref, vb_ref, cu * len_b, q0, w0, window, halo_b, finish_b)

    for c in range(ra):
        for q0, w0, window in blocks(len_a, halo_a):
            def finish_a(triple, c=c, q0=q0):
                merged = _mix2(triple, load3(pc_ref, slice(c * len_a + q0, c * len_a + q0 + Q_BLOCK)))
                store3(pn_ref, pl.ds(q0 * ra + c, Q_BLOCK, stride=ra), merged)
            add_unit(cab_ref, va_ref, c * len_a, q0, w0, window, halo_a, finish_a)

    for q0, w0, window in blocks(seq, halo1):
        def finish_1(triple, q0=q0):
            o, _, denom = _mix2(triple, load3(pn_ref, slice(q0, q0 + Q_BLOCK)))
            o_ref[q0:q0 + Q_BLOCK, :] = (o / denom).astype(o_ref.dtype)
        add_unit(nat_ref, vn_ref, 0, q0, w0, window, halo1, finish_1)

    _run_pipelined(len(units), lambda j: units[j][0](), lambda j, s: units[j][1](s))


def _dilated_attn(proj, *, batch, seq):
    m = batch * seq
    head_spec = lambda h0: pl.BlockSpec((None, seq, HEAD_DIM), lambda b, h: (h0 + h, b, 0))
    qk_buf = pltpu.VMEM((2, seq, HEAD_DIM), jnp.bfloat16)
    v_ones_buf = pltpu.VMEM((seq, 2 * HEAD_DIM), jnp.bfloat16)
    return pl.pallas_call(
        functools.partial(_dilated_attn_kernel, seq=seq),
        grid=(batch, B_HEADS),
        in_specs=[head_spec(QB_HEAD0), head_spec(KB_HEAD0), head_spec(VB_HEAD0)],
        out_specs=pl.BlockSpec((seq, HEAD_DIM), lambda b, h: (b, h)),
        out_shape=jax.ShapeDtypeStruct((m, B_QKV), jnp.bfloat16),
        scratch_shapes=[qk_buf,
                        pltpu.VMEM((3, seq, HEAD_DIM), jnp.float32),
                        qk_buf, qk_buf,
                        v_ones_buf, v_ones_buf, v_ones_buf]
                       + [pltpu.VMEM((3, seq, HEAD_DIM), jnp.float32)] * 2,
        compiler_params=pltpu.CompilerParams(
            dimension_semantics=("arbitrary", "arbitrary"), vmem_limit_bytes=VMEM_LIMIT),
        name="dilated_attn",
    )(proj, proj, proj)


def _layer_norm(y, g, b):
    mu = jnp.mean(y, axis=-1, keepdims=True)
    d = y - mu
    var = jnp.mean(d * d, axis=-1, keepdims=True)
    return d * lax.rsqrt(var + LN_EPS) * g + b


def _rms_norm(y, g):
    return y * lax.rsqrt(jnp.mean(y * y, axis=-1, keepdims=True) + RMS_EPS) * g


def _mix_ln1_kernel(oa_ref, ob_ref, x_ref, wo_ref, gna_ref, gnb_ref, g_ref, b_ref, h_ref):
    for r in range(MIX_TM // MIX_ROWS):
        rows = slice(r * MIX_ROWS, (r + 1) * MIX_ROWS)
        ya = _rms_norm(oa_ref[rows, :].astype(jnp.float32), gna_ref[...])
        yb = _rms_norm(ob_ref[rows, :].astype(jnp.float32), gnb_ref[...])
        mix = (jnp.dot(ya.astype(jnp.bfloat16), wo_ref[:A_Q, :], preferred_element_type=jnp.float32)
               + jnp.dot(yb.astype(jnp.bfloat16), wo_ref[A_Q:, :], preferred_element_type=jnp.float32))
        h_ref[rows, :] = _layer_norm(ALPHA * x_ref[rows, :] + mix, g_ref[...], b_ref[...])


def _mix_ln1(oa, ob, x2, wo_b, gn_a, gn_b, ln_g, ln_b):
    m = x2.shape[0]
    row = lambda w: pl.BlockSpec((MIX_TM, w), lambda i: (i, 0))
    full = lambda r, w: pl.BlockSpec((r, w), lambda i: (0, 0))
    return pl.pallas_call(
        _mix_ln1_kernel,
        grid=(m // MIX_TM,),
        in_specs=[row(A_Q), row(B_QKV), row(D_MODEL), full(D_MODEL, D_MODEL), full(1, A_Q),
                  full(1, B_QKV), full(1, D_MODEL), full(1, D_MODEL)],
        out_specs=row(D_MODEL),
        out_shape=jax.ShapeDtypeStruct((m, D_MODEL), jnp.float32),
        compiler_params=pltpu.CompilerParams(
            dimension_semantics=("arbitrary",), vmem_limit_bytes=VMEM_LIMIT),
        name="mix_ln1",
    )(oa, ob, x2, wo_b, gn_a, gn_b, ln_g, ln_b)


def _ffn_ln2_kernel(h_ref, w1g_ref, w2_ref, p_ref, wple_ref, g_ref, b_ref, o_ref, hb_ref, gate_ref):
    j = pl.program_id(1)
    n_ff = D_FF // FFN_TF
    n_gate = D_MODEL // FFN_TF
    n_sub = FFN_TF // FFN_SUB

    @pl.when(j == 0)
    def _():
        h = h_ref[...]
        hb_ref[...] = h.astype(jnp.bfloat16)
        o_ref[...] = ALPHA * h

    def up(c):
        return jnp.dot(hb_ref[...], w1g_ref[:, c * FFN_SUB:(c + 1) * FFN_SUB],
                       preferred_element_type=jnp.float32)

    for t in range(n_gate):
        @pl.when(j == t)
        def _():
            for c in range(n_sub):
                lo = t * FFN_TF + c * FFN_SUB
                gate_ref[:, lo:lo + FFN_SUB] = 0.5 * jnp.tanh(0.5 * up(c)) + 0.5

    @pl.when(j >= n_gate)
    def _():
        acts = []
        for c in range(n_sub):
            r = jnp.maximum(up(c), 0.0)
            acts.append((r * r).astype(jnp.bfloat16))
        for n in range(D_MODEL // FFN_SUB):
            cols = slice(n * FFN_SUB, (n + 1) * FFN_SUB)
            part = sum(jnp.dot(acts[c], w2_ref[c * FFN_SUB:(c + 1) * FFN_SUB, cols],
                               preferred_element_type=jnp.float32) for c in range(n_sub))
            o_ref[:, cols] += part

    @pl.when(j == n_gate + n_ff - 1)
    def _():
        ple = jnp.dot(p_ref[...].astype(jnp.bfloat16), wple_ref[...], preferred_element_type=jnp.float32)
        o_ref[...] = _layer_norm(o_ref[...] + ple * gate_ref[...], g_ref[...], b_ref[...])


def _ffn_ln2(h1, w1g_b, w2_b, p2, wple_b, ln_g, ln_b):
    m = h1.shape[0]
    n_ff = D_FF // FFN_TF
    n_gate = D_MODEL // FFN_TF
    return pl.pallas_call(
        _ffn_ln2_kernel,
        grid=(m // FFN_TM, n_gate + n_ff),
        in_specs=[pl.BlockSpec((FFN_TM, D_MODEL), lambda i, j: (i, 0)),
                  pl.BlockSpec((D_MODEL, FFN_TF), lambda i, j: (0, jnp.where(j < n_gate, n_ff + j, j - n_gate))),
                  pl.BlockSpec((FFN_TF, D_MODEL), lambda i, j: (jnp.maximum(j - n_gate, 0), 0)),
                  pl.BlockSpec((FFN_TM, PLE_DIM), lambda i, j: (i, 0)),
                  pl.BlockSpec((PLE_DIM, D_MODEL), lambda i, j: (0, 0)),
                  pl.BlockSpec((1, D_MODEL), lambda i, j: (0, 0)),
                  pl.BlockSpec((1, D_MODEL), lambda i, j: (0, 0))],
        out_specs=pl.BlockSpec((FFN_TM, D_MODEL), lambda i, j: (i, 0)),
        out_shape=jax.ShapeDtypeStruct((m, D_MODEL), jnp.float32),
        scratch_shapes=[pltpu.VMEM((FFN_TM, D_MODEL), jnp.bfloat16),
                        pltpu.VMEM((FFN_TM, D_MODEL), jnp.float32)],
        compiler_params=pltpu.CompilerParams(
            dimension_semantics=("arbitrary", "arbitrary"), vmem_limit_bytes=VMEM_LIMIT),
        name="ffn_ln2",
    )(h1, w1g_b, w2_b, p2, wple_b, ln_g, ln_b)


def kernel(x, p, positions, w_in, sink_a, gn_a, gn_b, w_o, ln1_g, ln1_b, w1, w2, w_ple, w_ple_gate,
           ln2_g, ln2_b):
    batch, seq, d = x.shape
    m = batch * seq
    bf = jnp.bfloat16
    cos_t, sin_t = _rope_tables(positions)
    h = x.reshape(m, d)
    for i in range(DEPTH):
        proj = _in_proj(h, w_in[i].astype(bf), cos_t, sin_t)
        oa = _window_attn(proj, sink_a[i], batch=batch, seq=seq)
        ob = _dilated_attn(proj, batch=batch, seq=seq)
        h1 = _mix_ln1(oa, ob, h, w_o[i].astype(bf), gn_a[i].reshape(1, -1), gn_b[i].reshape(1, -1),
                      ln1_g[i].reshape(1, -1), ln1_b[i].reshape(1, -1))
        w1g = jnp.concatenate([w1[i], w_ple_gate[i]], axis=1).astype(bf)
        h = _ffn_ln2(h1, w1g, w2[i].astype(bf), p[i].reshape(m, PLE_DIM), w_ple[i].astype(bf),
                     ln2_g[i].reshape(1, -1), ln2_b[i].reshape(1, -1))
    return h.reshape(batch, seq, d)
```

```python
import functools

import jax
import jax.numpy as jnp
from jax import lax
from jax.experimental import pallas as pl
from jax.experimental.pallas import tpu as pltpu

D_MODEL = 2048
HEAD_DIM = 128
A_HEADS = 8
A_KV_HEADS = 2
A_WINDOW = 128
B_HEADS = 8
DILATED_PATTERNS = ((128, 1), (512, 4), (2048, 16))
ROT_DIM = HEAD_DIM // 4
ROPE_THETA = 500000.0
D_FF = 4 * D_MODEL
PLE_DIM = 256
DEPTH = 1
ALPHA = (2.0 * DEPTH) ** 0.25
LN_EPS = 1e-5
RMS_EPS = 1e-6
NEG_INF = -1e30
LOG2_E = 1.4426950408889634

A_Q = A_HEADS * HEAD_DIM
A_KV = A_KV_HEADS * HEAD_DIM
B_QKV = B_HEADS * HEAD_DIM
D_IN = A_Q + 2 * A_KV + 3 * B_QKV
N_PROJ_HEADS = D_IN // HEAD_DIM
QA_HEAD0 = 0
KA_HEAD0 = A_HEADS
VA_HEAD0 = A_HEADS + A_KV_HEADS
QB_HEAD0 = A_HEADS + 2 * A_KV_HEADS
KB_HEAD0 = QB_HEAD0 + B_HEADS
VB_HEAD0 = KB_HEAD0 + B_HEADS

LANES = 128
MXU_COLS = 256
Q_BLOCK = 128
SCORE_LOOKAHEAD = 5
VMEM_LIMIT = 56 * 1024 * 1024

PROJ_TM = 1024
PROJ_TN = 1536
MIX_TM = 512
MIX_ROWS = 256
FFN_TM = 512
FFN_TF = 1024
FFN_SUB = 512
FFN_LAST_ROWS = 256
ROPE_TM = 2048
KIND_Q, KIND_K, KIND_V = 0, 1, 2


def _rope_table_kernel(pos_ref, freq_ref, cos_ref, sin_ref):
    ang = pos_ref[...].astype(jnp.float32) * freq_ref[...]
    lane = lax.broadcasted_iota(jnp.int32, ang.shape, 1)
    half = ROT_DIM // 2
    c = jnp.where(lane < ROT_DIM, jnp.cos(ang), 1.0)
    s = jnp.sin(ang)
    s = jnp.where(lane < half, -s, jnp.where(lane < ROT_DIM, s, 0.0))
    q_scale = LOG2_E * HEAD_DIM ** -0.5
    cos_ref[KIND_Q] = c * q_scale
    sin_ref[KIND_Q] = s * q_scale
    cos_ref[KIND_K] = c
    sin_ref[KIND_K] = s
    cos_ref[KIND_V] = jnp.ones_like(c)
    sin_ref[KIND_V] = jnp.zeros_like(s)


def _rope_tables(positions):
    m = positions.size
    half = ROT_DIM // 2
    inv_freq = ROPE_THETA ** (-jnp.arange(0, ROT_DIM, 2, dtype=jnp.float32) / ROT_DIM)
    freq = jnp.tile(inv_freq, LANES // half).reshape(1, LANES)
    pos = positions.reshape(m, 1)
    out = jax.ShapeDtypeStruct((3, m, LANES), jnp.float32)
    return pl.pallas_call(
        _rope_table_kernel,
        grid=(m // ROPE_TM,),
        in_specs=[pl.BlockSpec((ROPE_TM, 1), lambda i: (i, 0)),
                  pl.BlockSpec((1, LANES), lambda i: (0, 0))],
        out_specs=[pl.BlockSpec((3, ROPE_TM, LANES), lambda i: (0, i, 0))] * 2,
        out_shape=[out, out],
        name="rope_table",
    )(pos, freq)


def _in_proj_kernel(x_ref, w_ref, cos_ref, sin_ref, o_ref, xb_ref):
    j = pl.program_id(1)

    @pl.when(j == 0)
    def _():
        xb_ref[...] = x_ref[...].astype(jnp.bfloat16)

    half = ROT_DIM // 2
    heads_per_tile = PROJ_TN // HEAD_DIM
    heads_per_chunk = MXU_COLS // HEAD_DIM
    n_chunks = PROJ_TN // MXU_COLS

    def chunk(c, rows, n_rows):
        lane = lax.broadcasted_iota(jnp.int32, (n_rows, LANES), 1)
        partner = jnp.where(lane < half, lane + half, jnp.where(lane < ROT_DIM, lane - half, lane))
        acc = jnp.dot(xb_ref[rows, :], w_ref[:, c * MXU_COLS:(c + 1) * MXU_COLS],
                      preferred_element_type=jnp.float32)
        for g in range(heads_per_chunk):
            local = c * heads_per_chunk + g
            head = j * heads_per_tile + local
            is_q = (head < KA_HEAD0) | ((head >= QB_HEAD0) & (head < KB_HEAD0))
            is_k = ((head >= KA_HEAD0) & (head < VA_HEAD0)) | ((head >= KB_HEAD0) & (head < VB_HEAD0))
            kind = jnp.where(is_q, KIND_Q, jnp.where(is_k, KIND_K, KIND_V))
            t = acc[:, g * HEAD_DIM:(g + 1) * HEAD_DIM]
            swapped = jnp.take_along_axis(t, partner, axis=1)
            o_ref[local, rows, :] = t * cos_ref[kind, rows, :] + swapped * sin_ref[kind, rows, :]

    for c in range(n_chunks - 2):
        chunk(c, slice(None), PROJ_TM)
    for r in range(2):
        for c in range(n_chunks - 2, n_chunks):
            chunk(c, slice(r * (PROJ_TM // 2), (r + 1) * (PROJ_TM // 2)), PROJ_TM // 2)


def _in_proj(x2, w_in_b, cos_t, sin_t):
    m = x2.shape[0]
    heads_per_tile = PROJ_TN // HEAD_DIM
    return pl.pallas_call(
        _in_proj_kernel,
        grid=(m // PROJ_TM, D_IN // PROJ_TN),
        in_specs=[pl.BlockSpec((PROJ_TM, D_MODEL), lambda i, j: (i, 0)),
                  pl.BlockSpec((D_MODEL, PROJ_TN), lambda i, j: (0, j)),
                  pl.BlockSpec((3, PROJ_TM, LANES), lambda i, j: (0, i, 0)),
                  pl.BlockSpec((3, PROJ_TM, LANES), lambda i, j: (0, i, 0))],
        out_specs=pl.BlockSpec((heads_per_tile, PROJ_TM, HEAD_DIM), lambda i, j: (j, i, 0)),
        out_shape=jax.ShapeDtypeStruct((N_PROJ_HEADS, m, HEAD_DIM), jnp.float32),
        scratch_shapes=[pltpu.VMEM((PROJ_TM, D_MODEL), jnp.bfloat16)],
        compiler_params=pltpu.CompilerParams(
            dimension_semantics=("arbitrary", "arbitrary"), vmem_limit_bytes=VMEM_LIMIT),
        name="in_proj",
    )(x2, w_in_b, cos_t, sin_t)


def _band_bias(delta, rows, cols, halo):
    d = delta + lax.broadcasted_iota(jnp.int32, (rows, cols), 0) - lax.broadcasted_iota(jnp.int32, (rows, cols), 1)
    return jnp.where(jnp.abs(d) <= halo, 0.0, NEG_INF).astype(jnp.float32)


def _softmax_pv(s, v, sink=None):
    m = jnp.max(s, axis=-1, keepdims=True)
    if sink is not None:
        m = jnp.maximum(m, sink)
    e = jnp.exp2(s - m)
    acc = jnp.dot(e.astype(v.dtype), v, preferred_element_type=jnp.float32)
    if v.shape[1] == HEAD_DIM:
        o, denom = acc, jnp.sum(e, axis=-1, keepdims=True)
    else:
        o, denom = acc[:, :HEAD_DIM], acc[:, HEAD_DIM:]
    if sink is not None:
        denom = denom + jnp.exp2(sink - m)
    return o, m, denom


def _store_v_ones(vo_ref, v):
    vo_ref[:, :HEAD_DIM] = v.astype(vo_ref.dtype)
    vo_ref[:, HEAD_DIM:] = jnp.ones(v.shape, vo_ref.dtype)


def _qk(q, k):
    return lax.dot_general(q, k, (((1,), (1,)), ((), ())), preferred_element_type=jnp.float32)


def _window_start(q0, halo, window, length):
    return min(max(q0 - halo, 0), length - window)


def _run_pipelined(n, scores, finish, depth=SCORE_LOOKAHEAD):
    pending = {}
    for j in range(n + depth):
        if j < n:
            pending[j] = scores(j)
        if j >= depth:
            finish(j - depth, pending.pop(j - depth))


def _cast_rows_specs(weights, grid_steps, step_of):
    rows = weights[0].shape[0] // grid_steps
    assert all(w.shape[0] == weights[0].shape[0] for w in weights) and rows % 16 == 0
    cols = sum(w.shape[1] for w in weights)
    in_specs = [pl.BlockSpec((rows, w.shape[1]), lambda *g: (step_of(*g), 0)) for w in weights]
    out_spec = pl.BlockSpec((rows, cols), lambda *g: (step_of(*g), 0))
    out_shape = jax.ShapeDtypeStruct((weights[0].shape[0], cols), jnp.bfloat16)
    return in_specs, out_spec, out_shape


def _cast_rows(w_refs, out_ref):
    lo = 0
    for w_ref in w_refs:
        out_ref[:, lo:lo + w_ref.shape[1]] = w_ref[...].astype(out_ref.dtype)
        lo += w_ref.shape[1]


def _window_attn_kernel(q_ref, k_ref, v_ref, sink_ref, w2_ref, wo_ref, o_ref, w2b_ref, wob_ref, qkv_ref, *, seq):
    _cast_rows([w2_ref], w2b_ref)
    _cast_rows([wo_ref], wob_ref)
    window = Q_BLOCK + 2 * A_WINDOW
    head = pl.program_id(1)
    sink = sink_ref[head] * LOG2_E
    qkv_ref[0] = q_ref[...].astype(jnp.bfloat16)

    @pl.when(head % (A_HEADS // A_KV_HEADS) == 0)
    def _():
        qkv_ref[1] = k_ref[...].astype(jnp.bfloat16)
        qkv_ref[2] = v_ref[...].astype(jnp.bfloat16)

    biases = {}

    def scores(j):
        q0 = j * Q_BLOCK
        w0 = _window_start(q0, A_WINDOW, window, seq)
        if q0 - w0 not in biases:
            biases[q0 - w0] = _band_bias(q0 - w0, Q_BLOCK, window, A_WINDOW)
        return _qk(qkv_ref[0, q0:q0 + Q_BLOCK, :], qkv_ref[1, w0:w0 + window, :]) + biases[q0 - w0]

    def finish(j, s):
        q0 = j * Q_BLOCK
        w0 = _window_start(q0, A_WINDOW, window, seq)
        o, _, denom = _softmax_pv(s, qkv_ref[2, w0:w0 + window, :], sink)
        o_ref[q0:q0 + Q_BLOCK, :] = (o / denom).astype(o_ref.dtype)

    _run_pipelined(seq // Q_BLOCK, scores, finish)


def _window_attn(proj, sink, w2, w_o, *, batch, seq):
    m = batch * seq
    grp = A_HEADS // A_KV_HEADS
    head_spec = lambda h0, div: pl.BlockSpec((None, seq, HEAD_DIM), lambda b, h: (h0 + h // div, b, 0))
    step_of = lambda b, h: b * A_HEADS + h
    w2_in, w2_out, w2_shape = _cast_rows_specs([w2], batch * A_HEADS, step_of)
    wo_in, wo_out, wo_shape = _cast_rows_specs([w_o], batch * A_HEADS, step_of)
    return pl.pallas_call(
        functools.partial(_window_attn_kernel, seq=seq),
        grid=(batch, A_HEADS),
        in_specs=[head_spec(QA_HEAD0, 1), head_spec(KA_HEAD0, grp), head_spec(VA_HEAD0, grp),
                  pl.BlockSpec(memory_space=pltpu.SMEM)] + w2_in + wo_in,
        out_specs=[pl.BlockSpec((seq, HEAD_DIM), lambda b, h: (b, h)), w2_out, wo_out],
        out_shape=[jax.ShapeDtypeStruct((m, A_Q), jnp.bfloat16), w2_shape, wo_shape],
        scratch_shapes=[pltpu.VMEM((3, seq, HEAD_DIM), jnp.bfloat16)],
        compiler_params=pltpu.CompilerParams(
            dimension_semantics=("arbitrary", "arbitrary"), vmem_limit_bytes=VMEM_LIMIT),
        name="window_attn",
    )(proj, proj, proj, sink, w2, w_o)


def _mix2(a, b):
    m = jnp.maximum(a[1], b[1])
    fa = jnp.exp2(a[1] - m)
    fb = jnp.exp2(b[1] - m)
    return fa * a[0] + fb * b[0], m, fa * a[2] + fb * b[2]


def _dilated_attn_kernel(q_ref, k_ref, v_ref, w1_ref, wg_ref, o_ref, w1gb_ref,
                         nat_ref, ca_ref, cab_ref, cbb_ref, vn_ref, va_ref, vb_ref, pc_ref, pn_ref, *, seq):
    _cast_rows([w1_ref, wg_ref], w1gb_ref)
    bf = jnp.bfloat16
    (w1, r1), (wa, ra), (wb, rb) = sorted(DILATED_PATTERNS, key=lambda pat: pat[1])
    assert r1 == 1 and rb % ra == 0
    sub = rb // ra
    len_a, len_b = seq // ra, seq // rb
    halo1, halo_a, halo_b = w1 // 2, wa // (2 * ra), wb // (2 * rb)
    biases = {}

    def band_scores(qk_src, base, q0, w0, window, halo):
        key = (q0 - w0, Q_BLOCK, window, halo)
        if key not in biases:
            biases[key] = _band_bias(*key)
        return _qk(qk_src[0, base + q0:base + q0 + Q_BLOCK, :],
                   qk_src[1, base + w0:base + w0 + window, :]) + biases[key]

    def band_output(s, v_src, base, w0, window):
        o, m, denom = _softmax_pv(s, v_src[base + w0:base + w0 + window, :])
        return o, jnp.broadcast_to(m, (Q_BLOCK, LANES)), denom

    def blocks(length, halo):
        window = min(Q_BLOCK + 2 * halo, length)
        for j in range(length // Q_BLOCK):
            q0 = j * Q_BLOCK
            yield q0, _window_start(q0, halo, window, length), window

    def load3(ref, rows):
        return tuple(ref[t, rows, :] for t in range(3))

    def store3(ref, rows, triple):
        for t in range(3):
            ref[t, rows, :] = triple[t]

    for t, src in enumerate((q_ref, k_ref, v_ref)):
        if t < 2:
            nat_ref[t] = src[...].astype(bf)
        else:
            _store_v_ones(vn_ref, src[...])
        for c in range(ra):
            x = src[pl.ds(c, len_a, stride=ra), :]
            ca_ref[t, c * len_a:(c + 1) * len_a, :] = x
            if t < 2:
                cab_ref[t, c * len_a:(c + 1) * len_a, :] = x.astype(bf)
            else:
                _store_v_ones(va_ref.at[c * len_a:(c + 1) * len_a, :], x)
    for t in range(3):
        for c in range(ra):
            for u in range(sub):
                rows = slice((c * sub + u) * len_b, (c * sub + u + 1) * len_b)
                x = ca_ref[t, pl.ds(c * len_a + u, len_b, stride=sub), :]
                if t < 2:
                    cbb_ref[t, rows, :] = x.astype(bf)
                else:
                    _store_v_ones(vb_ref.at[rows, :], x)

    units = []

    def add_unit(qk_src, v_src, base, q0, w0, window, halo, finish):
        units.append((lambda: band_scores(qk_src, base, q0, w0, window, halo),
                      lambda s: finish(band_output(s, v_src, base, w0, window))))

    for c in range(ra):
        for u in range(sub):
            cu = c * sub + u
            for q0, w0, window in blocks(len_b, halo_b):
                def finish_b(triple, c=c, u=u, q0=q0):
                    store3(pc_ref, pl.ds(c * len_a + q0 * sub + u, Q_BLOCK, stride=sub), triple)
                add_unit(cbb_ref, vb_ref, cu * len_b, q0, w0, window, halo_b, finish_b)

    for c in range(ra):
        for q0, w0, window in blocks(len_a, halo_a):
            def finish_a(triple, c=c, q0=q0):
                merged = _mix2(triple, load3(pc_ref, slice(c * len_a + q0, c * len_a + q0 + Q_BLOCK)))
                store3(pn_ref, pl.ds(q0 * ra + c, Q_BLOCK, stride=ra), merged)
            add_unit(cab_ref, va_ref, c * len_a, q0, w0, window, halo_a, finish_a)

    for q0, w0, window in blocks(seq, halo1):
        def finish_1(triple, q0=q0):
            o, _, denom = _mix2(triple, load3(pn_ref, slice(q0, q0 + Q_BLOCK)))
            o_ref[q0:q0 + Q_BLOCK, :] = (o / denom).astype(o_ref.dtype)
        add_unit(nat_ref, vn_ref, 0, q0, w0, window, halo1, finish_1)

    _run_pipelined(len(units), lambda j: units[j][0](), lambda j, s: units[j][1](s))


def _dilated_attn(proj, w1, w_gate, *, batch, seq):
    m = batch * seq
    head_spec = lambda h0: pl.BlockSpec((None, seq, HEAD_DIM), lambda b, h: (h0 + h, b, 0))
    w_in_specs, w_out_spec, w_out_shape = _cast_rows_specs([w1, w_gate], batch * B_HEADS,
                                                           lambda b, h: b * B_HEADS + h)
    qk_buf = pltpu.VMEM((2, seq, HEAD_DIM), jnp.bfloat16)
    v_ones_buf = pltpu.VMEM((seq, 2 * HEAD_DIM), jnp.bfloat16)
    return pl.pallas_call(
        functools.partial(_dilated_attn_kernel, seq=seq),
        grid=(batch, B_HEADS),
        in_specs=[head_spec(QB_HEAD0), head_spec(KB_HEAD0), head_spec(VB_HEAD0)] + w_in_specs,
        out_specs=[pl.BlockSpec((seq, HEAD_DIM), lambda b, h: (b, h)), w_out_spec],
        out_shape=[jax.ShapeDtypeStruct((m, B_QKV), jnp.bfloat16), w_out_shape],
        scratch_shapes=[qk_buf,
                        pltpu.VMEM((3, seq, HEAD_DIM), jnp.float32),
                        qk_buf, qk_buf,
                        v_ones_buf, v_ones_buf, v_ones_buf]
                       + [pltpu.VMEM((3, seq, HEAD_DIM), jnp.float32)] * 2,
        compiler_params=pltpu.CompilerParams(
            dimension_semantics=("arbitrary", "arbitrary"), vmem_limit_bytes=VMEM_LIMIT),
        name="dilated_attn",
    )(proj, proj, proj, w1, w_gate)


def _layer_norm(y, g, b):
    mu = jnp.mean(y, axis=-1, keepdims=True)
    d = y - mu
    var = jnp.mean(d * d, axis=-1, keepdims=True)
    return d * lax.rsqrt(var + LN_EPS) * g + b


def _rms_norm(y, g):
    return y * lax.rsqrt(jnp.mean(y * y, axis=-1, keepdims=True) + RMS_EPS) * g


def _mix_ln1_kernel(oa_ref, ob_ref, x_ref, wo_ref, gna_ref, gnb_ref, g_ref, b_ref, h_ref):
    for r in range(MIX_TM // MIX_ROWS):
        rows = slice(r * MIX_ROWS, (r + 1) * MIX_ROWS)
        ya = _rms_norm(oa_ref[rows, :].astype(jnp.float32), gna_ref[...])
        yb = _rms_norm(ob_ref[rows, :].astype(jnp.float32), gnb_ref[...])
        mix = (jnp.dot(ya.astype(jnp.bfloat16), wo_ref[:A_Q, :], preferred_element_type=jnp.float32)
               + jnp.dot(yb.astype(jnp.bfloat16), wo_ref[A_Q:, :], preferred_element_type=jnp.float32))
        h_ref[rows, :] = _layer_norm(ALPHA * x_ref[rows, :] + mix, g_ref[...], b_ref[...])


def _mix_ln1(oa, ob, x2, wo_b, gn_a, gn_b, ln_g, ln_b):
    m = x2.shape[0]
    row = lambda w: pl.BlockSpec((MIX_TM, w), lambda i: (i, 0))
    full = lambda r, w: pl.BlockSpec((r, w), lambda i: (0, 0))
    return pl.pallas_call(
        _mix_ln1_kernel,
        grid=(m // MIX_TM,),
        in_specs=[row(A_Q), row(B_QKV), row(D_MODEL), full(D_MODEL, D_MODEL), full(1, A_Q),
                  full(1, B_QKV), full(1, D_MODEL), full(1, D_MODEL)],
        out_specs=row(D_MODEL),
        out_shape=jax.ShapeDtypeStruct((m, D_MODEL), jnp.float32),
        compiler_params=pltpu.CompilerParams(
            dimension_semantics=("arbitrary",), vmem_limit_bytes=VMEM_LIMIT),
        name="mix_ln1",
    )(oa, ob, x2, wo_b, gn_a, gn_b, ln_g, ln_b)


def _ffn_ln2_kernel(h_ref, w1g_ref, w2_ref, p_ref, wple_ref, g_ref, b_ref, o_ref, hb_ref, gate_ref):
    j = pl.program_id(1)
    n_ff = D_FF // FFN_TF
    n_all = (D_FF + D_MODEL) // FFN_TF
    n_sub = FFN_TF // FFN_SUB

    @pl.when(j == 0)
    def _():
        h = h_ref[...]
        hb_ref[...] = h.astype(jnp.bfloat16)
        o_ref[...] = ALPHA * h

    def up(c):
        return jnp.dot(hb_ref[...], w1g_ref[:, c * FFN_SUB:(c + 1) * FFN_SUB],
                       preferred_element_type=jnp.float32)

    @pl.when(j < n_ff)
    def _():
        acts = []
        for c in range(n_sub):
            r = jnp.maximum(up(c), 0.0)
            acts.append((r * r).astype(jnp.bfloat16))
        for n in range(D_MODEL // FFN_SUB):
            cols = slice(n * FFN_SUB, (n + 1) * FFN_SUB)
            part = sum(jnp.dot(acts[c], w2_ref[c * FFN_SUB:(c + 1) * FFN_SUB, cols],
                               preferred_element_type=jnp.float32) for c in range(n_sub))
            o_ref[:, cols] += part

    def gate_chunk(t, c, rows):
        lo = t * FFN_TF + c * FFN_SUB
        a = jnp.dot(hb_ref[rows, :], w1g_ref[:, c * FFN_SUB:(c + 1) * FFN_SUB], preferred_element_type=jnp.float32)
        gate_ref[rows, lo:lo + FFN_SUB] = 0.5 * jnp.tanh(0.5 * a) + 0.5

    for t in range(n_all - n_ff - 1):
        @pl.when(j == n_ff + t)
        def _():
            for c in range(n_sub):
                gate_chunk(t, c, slice(None))

    @pl.when(j == n_all - 1)
    def _():
        for r in range(FFN_TM // FFN_LAST_ROWS):
            rows = slice(r * FFN_LAST_ROWS, (r + 1) * FFN_LAST_ROWS)
            for c in range(n_sub):
                gate_chunk(n_all - n_ff - 1, c, rows)
            ple = jnp.dot(p_ref[rows, :].astype(jnp.bfloat16), wple_ref[...], preferred_element_type=jnp.float32)
            o_ref[rows, :] = _layer_norm(o_ref[rows, :] + ple * gate_ref[rows, :], g_ref[...], b_ref[...])


def _ffn_ln2(h1, w1g_b, w2_b, p2, wple_b, ln_g, ln_b):
    m = h1.shape[0]
    n_ff = D_FF // FFN_TF
    n_all = (D_FF + D_MODEL) // FFN_TF
    return pl.pallas_call(
        _ffn_ln2_kernel,
        grid=(m // FFN_TM, n_all),
        in_specs=[pl.BlockSpec((FFN_TM, D_MODEL), lambda i, j: (i, 0)),
                  pl.BlockSpec((D_MODEL, FFN_TF), lambda i, j: (0, j)),
                  pl.BlockSpec((FFN_TF, D_MODEL), lambda i, j: (jnp.minimum(j, n_ff - 1), 0)),
                  pl.BlockSpec((FFN_TM, PLE_DIM), lambda i, j: (i, 0)),
                  pl.BlockSpec((PLE_DIM, D_MODEL), lambda i, j: (0, 0)),
                  pl.BlockSpec((1, D_MODEL), lambda i, j: (0, 0)),
                  pl.BlockSpec((1, D_MODEL), lambda i, j: (0, 0))],
        out_specs=pl.BlockSpec((FFN_TM, D_MODEL), lambda i, j: (i, 0)),
        out_shape=jax.ShapeDtypeStruct((m, D_MODEL), jnp.float32),
        scratch_shapes=[pltpu.VMEM((FFN_TM, D_MODEL), jnp.bfloat16),
                        pltpu.VMEM((FFN_TM, D_MODEL), jnp.float32)],
        compiler_params=pltpu.CompilerParams(
            dimension_semantics=("arbitrary", "arbitrary"), vmem_limit_bytes=VMEM_LIMIT),
        name="ffn_ln2",
    )(h1, w1g_b, w2_b, p2, wple_b, ln_g, ln_b)


def kernel(x, p, positions, w_in, sink_a, gn_a, gn_b, w_o, ln1_g, ln1_b, w1, w2, w_ple, w_ple_gate,
           ln2_g, ln2_b):
    batch, seq, d = x.shape
    m = batch * seq
    bf = jnp.bfloat16
    cos_t, sin_t = _rope_tables(positions)
    h = x.reshape(m, d)
    for i in range(DEPTH):
        proj = _in_proj(h, w_in[i].astype(bf), cos_t, sin_t)
        oa, w2_b, wo_b = _window_attn(proj, sink_a[i], w2[i], w_o[i], batch=batch, seq=seq)
        ob, w1g_b = _dilated_attn(proj, w1[i], w_ple_gate[i], batch=batch, seq=seq)
        h1 = _mix_ln1(oa, ob, h, wo_b, gn_a[i].reshape(1, -1), gn_b[i].reshape(1, -1),
                      ln1_g[i].reshape(1, -1), ln1_b[i].reshape(1, -1))
        h = _ffn_ln2(h1, w1g_b, w2_b, p[i].reshape(m, PLE_DIM), w_ple[i].astype(bf),
                     ln2_g[i].reshape(1, -1), ln2_b[i].reshape(1, -1))
    return h.reshape(batch, seq, d)
```

```python
import functools

import jax
import jax.numpy as jnp
from jax import lax
from jax.experimental import pallas as pl
from jax.experimental.pallas import tpu as pltpu

D_MODEL = 2048
HEAD_DIM = 128
A_HEADS = 8
A_KV_HEADS = 2
A_WINDOW = 128
B_HEADS = 8
DILATED_PATTERNS = ((128, 1), (512, 4), (2048, 16))
ROT_DIM = HEAD_DIM // 4
ROPE_THETA = 500000.0
D_FF = 4 * D_MODEL
PLE_DIM = 256
DEPTH = 1
ALPHA = (2.0 * DEPTH) ** 0.25
LN_EPS = 1e-5
RMS_EPS = 1e-6
NEG_INF = -1e30
LOG2_E = 1.4426950408889634

A_Q = A_HEADS * HEAD_DIM
A_KV = A_KV_HEADS * HEAD_DIM
B_QKV = B_HEADS * HEAD_DIM
D_IN = A_Q + 2 * A_KV + 3 * B_QKV
N_PROJ_HEADS = D_IN // HEAD_DIM
QA_HEAD0 = 0
KA_HEAD0 = A_HEADS
VA_HEAD0 = A_HEADS + A_KV_HEADS
QB_HEAD0 = A_HEADS + 2 * A_KV_HEADS
KB_HEAD0 = QB_HEAD0 + B_HEADS
VB_HEAD0 = KB_HEAD0 + B_HEADS

LANES = 128
MXU_COLS = 256
Q_BLOCK = 128
SCORE_LOOKAHEAD = 5
VMEM_LIMIT = 56 * 1024 * 1024

PROJ_TM = 1024
PROJ_TN = 1536
MIX_TM = 512
MIX_ROWS = 256
FFN_TM = 512
FFN_TF = 1024
FFN_SUB = 512
FFN_LAST_ROWS = 256
ROPE_TM = 2048
KIND_Q, KIND_K, KIND_V = 0, 1, 2


def _rope_table_kernel(pos_ref, freq_ref, w_ref, cos_ref, sin_ref, wb_ref):
    _cast_rows([w_ref], wb_ref)
    ang = pos_ref[...].astype(jnp.float32) * freq_ref[...]
    lane = lax.broadcasted_iota(jnp.int32, ang.shape, 1)
    half = ROT_DIM // 2
    c = jnp.where(lane < ROT_DIM, jnp.cos(ang), 1.0)
    s = jnp.sin(ang)
    s = jnp.where(lane < half, -s, jnp.where(lane < ROT_DIM, s, 0.0))
    q_scale = LOG2_E * HEAD_DIM ** -0.5
    cos_ref[KIND_Q] = c * q_scale
    sin_ref[KIND_Q] = s * q_scale
    cos_ref[KIND_K] = c
    sin_ref[KIND_K] = s
    cos_ref[KIND_V] = jnp.ones_like(c)
    sin_ref[KIND_V] = jnp.zeros_like(s)


def _rope_tables(positions, w_in):
    m = positions.size
    half = ROT_DIM // 2
    inv_freq = ROPE_THETA ** (-jnp.arange(0, ROT_DIM, 2, dtype=jnp.float32) / ROT_DIM)
    freq = jnp.tile(inv_freq, LANES // half).reshape(1, LANES)
    pos = positions.reshape(m, 1)
    out = jax.ShapeDtypeStruct((3, m, LANES), jnp.float32)
    w_flat = w_in.reshape(-1, w_in.shape[-1])
    w_specs, wb_spec, wb_shape = _cast_rows_specs([w_flat], m // ROPE_TM, lambda i: i)
    cos_t, sin_t, w_b = pl.pallas_call(
        _rope_table_kernel,
        grid=(m // ROPE_TM,),
        in_specs=[pl.BlockSpec((ROPE_TM, 1), lambda i: (i, 0)),
                  pl.BlockSpec((1, LANES), lambda i: (0, 0))] + w_specs,
        out_specs=[pl.BlockSpec((3, ROPE_TM, LANES), lambda i: (0, i, 0))] * 2 + [wb_spec],
        out_shape=[out, out, wb_shape],
        name="rope_table",
    )(pos, freq, w_flat)
    return cos_t, sin_t, w_b.reshape(w_in.shape)


def _in_proj_kernel(x_ref, w_ref, cos_ref, sin_ref, o_ref, xb_ref):
    j = pl.program_id(1)

    @pl.when(j == 0)
    def _():
        xb_ref[...] = x_ref[...].astype(jnp.bfloat16)

    half = ROT_DIM // 2
    heads_per_tile = PROJ_TN // HEAD_DIM
    heads_per_chunk = MXU_COLS // HEAD_DIM
    n_chunks = PROJ_TN // MXU_COLS

    def chunk(c, rows, n_rows):
        lane = lax.broadcasted_iota(jnp.int32, (n_rows, LANES), 1)
        partner = jnp.where(lane < half, lane + half, jnp.where(lane < ROT_DIM, lane - half, lane))
        acc = jnp.dot(xb_ref[rows, :], w_ref[:, c * MXU_COLS:(c + 1) * MXU_COLS],
                      preferred_element_type=jnp.float32)
        for g in range(heads_per_chunk):
            local = c * heads_per_chunk + g
            head = j * heads_per_tile + local
            is_q = (head < KA_HEAD0) | ((head >= QB_HEAD0) & (head < KB_HEAD0))
            is_k = ((head >= KA_HEAD0) & (head < VA_HEAD0)) | ((head >= KB_HEAD0) & (head < VB_HEAD0))
            kind = jnp.where(is_q, KIND_Q, jnp.where(is_k, KIND_K, KIND_V))
            t = acc[:, g * HEAD_DIM:(g + 1) * HEAD_DIM]
            swapped = jnp.take_along_axis(t, partner, axis=1)
            o_ref[local, rows, :] = t * cos_ref[kind, rows, :] + swapped * sin_ref[kind, rows, :]

    for c in range(n_chunks - 2):
        chunk(c, slice(None), PROJ_TM)
    for r in range(2):
        for c in range(n_chunks - 2, n_chunks):
            chunk(c, slice(r * (PROJ_TM // 2), (r + 1) * (PROJ_TM // 2)), PROJ_TM // 2)


def _in_proj(x2, w_in_b, layer, cos_t, sin_t):
    m = x2.shape[0]
    heads_per_tile = PROJ_TN // HEAD_DIM
    return pl.pallas_call(
        _in_proj_kernel,
        grid=(m // PROJ_TM, D_IN // PROJ_TN),
        in_specs=[pl.BlockSpec((PROJ_TM, D_MODEL), lambda i, j: (i, 0)),
                  pl.BlockSpec((None, D_MODEL, PROJ_TN), lambda i, j: (layer, 0, j)),
                  pl.BlockSpec((3, PROJ_TM, LANES), lambda i, j: (0, i, 0)),
                  pl.BlockSpec((3, PROJ_TM, LANES), lambda i, j: (0, i, 0))],
        out_specs=pl.BlockSpec((heads_per_tile, PROJ_TM, HEAD_DIM), lambda i, j: (j, i, 0)),
        out_shape=jax.ShapeDtypeStruct((N_PROJ_HEADS, m, HEAD_DIM), jnp.float32),
        scratch_shapes=[pltpu.VMEM((PROJ_TM, D_MODEL), jnp.bfloat16)],
        compiler_params=pltpu.CompilerParams(
            dimension_semantics=("arbitrary", "arbitrary"), vmem_limit_bytes=VMEM_LIMIT),
        name="in_proj",
    )(x2, w_in_b, cos_t, sin_t)


def _band_bias(delta, rows, cols, halo):
    d = delta + lax.broadcasted_iota(jnp.int32, (rows, cols), 0) - lax.broadcasted_iota(jnp.int32, (rows, cols), 1)
    return jnp.where(jnp.abs(d) <= halo, 0.0, NEG_INF).astype(jnp.float32)


def _softmax_pv(s, v, sink=None):
    m = jnp.max(s, axis=-1, keepdims=True)
    if sink is not None:
        m = jnp.maximum(m, sink)
    e = jnp.exp2(s - m)
    acc = jnp.dot(e.astype(v.dtype), v, preferred_element_type=jnp.float32)
    if v.shape[1] == HEAD_DIM:
        o, denom = acc, jnp.sum(e, axis=-1, keepdims=True)
    else:
        o, denom = acc[:, :HEAD_DIM], acc[:, HEAD_DIM:]
    if sink is not None:
        denom = denom + jnp.exp2(sink - m)
    return o, m, denom


def _store_v_ones(vo_ref, v):
    vo_ref[:, :HEAD_DIM] = v.astype(vo_ref.dtype)
    vo_ref[:, HEAD_DIM:] = jnp.ones(v.shape, vo_ref.dtype)


def _qk(q, k):
    return lax.dot_general(q, k, (((1,), (1,)), ((), ())), preferred_element_type=jnp.float32)


def _window_start(q0, halo, window, length):
    return min(max(q0 - halo, 0), length - window)


def _run_pipelined(n, scores, finish, depth=SCORE_LOOKAHEAD):
    pending = {}
    for j in range(n + depth):
        if j < n:
            pending[j] = scores(j)
        if j >= depth:
            finish(j - depth, pending.pop(j - depth))


def _cast_rows_specs(weights, grid_steps, step_of):
    rows = weights[0].shape[0] // grid_steps
    assert all(w.shape[0] == weights[0].shape[0] for w in weights) and rows % 16 == 0
    cols = sum(w.shape[1] for w in weights)
    in_specs = [pl.BlockSpec((rows, w.shape[1]), lambda *g: (step_of(*g), 0)) for w in weights]
    out_spec = pl.BlockSpec((rows, cols), lambda *g: (step_of(*g), 0))
    out_shape = jax.ShapeDtypeStruct((weights[0].shape[0], cols), jnp.bfloat16)
    return in_specs, out_spec, out_shape


def _cast_rows(w_refs, out_ref):
    lo = 0
    for w_ref in w_refs:
        out_ref[:, lo:lo + w_ref.shape[1]] = w_ref[...].astype(out_ref.dtype)
        lo += w_ref.shape[1]


def _window_attn_kernel(q_ref, k_ref, v_ref, sink_ref, w2_ref, wo_ref, o_ref, w2b_ref, wob_ref, qkv_ref, *, seq):
    _cast_rows([w2_ref], w2b_ref)
    _cast_rows([wo_ref], wob_ref)
    window = Q_BLOCK + 2 * A_WINDOW
    head = pl.program_id(1)
    sink = sink_ref[head] * LOG2_E
    qkv_ref[0] = q_ref[...].astype(jnp.bfloat16)

    @pl.when(head % (A_HEADS // A_KV_HEADS) == 0)
    def _():
        qkv_ref[1] = k_ref[...].astype(jnp.bfloat16)
        qkv_ref[2] = v_ref[...].astype(jnp.bfloat16)

    biases = {}

    def scores(j):
        q0 = j * Q_BLOCK
        w0 = _window_start(q0, A_WINDOW, window, seq)
        if q0 - w0 not in biases:
            biases[q0 - w0] = _band_bias(q0 - w0, Q_BLOCK, window, A_WINDOW)
        return _qk(qkv_ref[0, q0:q0 + Q_BLOCK, :], qkv_ref[1, w0:w0 + window, :]) + biases[q0 - w0]

    def finish(j, s):
        q0 = j * Q_BLOCK
        w0 = _window_start(q0, A_WINDOW, window, seq)
        o, _, denom = _softmax_pv(s, qkv_ref[2, w0:w0 + window, :], sink)
        o_ref[q0:q0 + Q_BLOCK, :] = (o / denom).astype(o_ref.dtype)

    _run_pipelined(seq // Q_BLOCK, scores, finish)


def _window_attn(proj, sink, w2, w_o, *, batch, seq):
    m = batch * seq
    grp = A_HEADS // A_KV_HEADS
    head_spec = lambda h0, div: pl.BlockSpec((None, seq, HEAD_DIM), lambda b, h: (h0 + h // div, b, 0))
    step_of = lambda b, h: b * A_HEADS + h
    w2_in, w2_out, w2_shape = _cast_rows_specs([w2], batch * A_HEADS, step_of)
    wo_in, wo_out, wo_shape = _cast_rows_specs([w_o], batch * A_HEADS, step_of)
    return pl.pallas_call(
        functools.partial(_window_attn_kernel, seq=seq),
        grid=(batch, A_HEADS),
        in_specs=[head_spec(QA_HEAD0, 1), head_spec(KA_HEAD0, grp), head_spec(VA_HEAD0, grp),
                  pl.BlockSpec(memory_space=pltpu.SMEM)] + w2_in + wo_in,
        out_specs=[pl.BlockSpec((seq, HEAD_DIM), lambda b, h: (b, h)), w2_out, wo_out],
        out_shape=[jax.ShapeDtypeStruct((m, A_Q), jnp.bfloat16), w2_shape, wo_shape],
        scratch_shapes=[pltpu.VMEM((3, seq, HEAD_DIM), jnp.bfloat16)],
        compiler_params=pltpu.CompilerParams(
            dimension_semantics=("arbitrary", "arbitrary"), vmem_limit_bytes=VMEM_LIMIT),
        name="window_attn",
    )(proj, proj, proj, sink, w2, w_o)


def _mix2(a, b):
    m = jnp.maximum(a[1], b[1])
    fa = jnp.exp2(a[1] - m)
    fb = jnp.exp2(b[1] - m)
    return fa * a[0] + fb * b[0], m, fa * a[2] + fb * b[2]


def _dilated_attn_kernel(q_ref, k_ref, v_ref, w1_ref, wg_ref, o_ref, w1gb_ref,
                         nat_ref, ca_ref, cab_ref, cbb_ref, vn_ref, va_ref, vb_ref, pc_ref, pn_ref, *, seq):
    _cast_rows([w1_ref, wg_ref], w1gb_ref)
    bf = jnp.bfloat16
    (w1, r1), (wa, ra), (wb, rb) = sorted(DILATED_PATTERNS, key=lambda pat: pat[1])
    assert r1 == 1 and rb % ra == 0
    sub = rb // ra
    len_a, len_b = seq // ra, seq // rb
    halo1, halo_a, halo_b = w1 // 2, wa // (2 * ra), wb // (2 * rb)
    biases = {}

    def band_scores(qk_src, base, q0, w0, window, halo):
        key = (q0 - w0, Q_BLOCK, window, halo)
        if key not in biases:
            biases[key] = _band_bias(*key)
        return _qk(qk_src[0, base + q0:base + q0 + Q_BLOCK, :],
                   qk_src[1, base + w0:base + w0 + window, :]) + biases[key]

    def band_output(s, v_src, base, w0, window):
        o, m, denom = _softmax_pv(s, v_src[base + w0:base + w0 + window, :])
        return o, jnp.broadcast_to(m, (Q_BLOCK, LANES)), denom

    def blocks(length, halo):
        window = min(Q_BLOCK + 2 * halo, length)
        for j in range(length // Q_BLOCK):
            q0 = j * Q_BLOCK
            yield q0, _window_start(q0, halo, window, length), window

    def load3(ref, rows):
        return tuple(ref[t, rows, :] for t in range(3))

    def store3(ref, rows, triple):
        for t in range(3):
            ref[t, rows, :] = triple[t]

    for t, src in enumerate((q_ref, k_ref, v_ref)):
        if t < 2:
            nat_ref[t] = src[...].astype(bf)
        else:
            _store_v_ones(vn_ref, src[...])
        for c in range(ra):
            x = src[pl.ds(c, len_a, stride=ra), :]
            ca_ref[t, c * len_a:(c + 1) * len_a, :] = x
            if t < 2:
                cab_ref[t, c * len_a:(c + 1) * len_a, :] = x.astype(bf)
            else:
                _store_v_ones(va_ref.at[c * len_a:(c + 1) * len_a, :], x)
    for t in range(3):
        for c in range(ra):
            for u in range(sub):
                rows = slice((c * sub + u) * len_b, (c * sub + u + 1) * len_b)
                x = ca_ref[t, pl.ds(c * len_a + u, len_b, stride=sub), :]
                if t < 2:
                    cbb_ref[t, rows, :] = x.astype(bf)
                else:
                    _store_v_ones(vb_ref.at[rows, :], x)

    units = []

    def add_unit(qk_src, v_src, base, q0, w0, window, halo, finish):
        units.append((lambda: band_scores(qk_src, base, q0, w0, window, halo),
                      lambda s: finish(band_output(s, v_src, base, w0, window))))

    for c in range(ra):
        for u in range(sub):
            cu = c * sub + u
            for q0, w0, window in blocks(len_b, halo_b):
                def finish_b(triple, c=c, u=u, q0=q0):
                    store3(pc_ref, pl.ds(c * len_a + q0 * sub + u, Q_BLOCK, stride=sub), triple)
                add_unit(cbb_ref, vb_ref, cu * len_b, q0, w0, window, halo_b, finish_b)

    for c in range(ra):
        for q0, w0, window in blocks(len_a, halo_a):
            def finish_a(triple, c=c, q0=q0):
                merged = _mix2(triple, load3(pc_ref, slice(c * len_a + q0, c * len_a + q0 + Q_BLOCK)))
                store3(pn_ref, pl.ds(q0 * ra + c, Q_BLOCK, stride=ra), merged)
            add_unit(cab_ref, va_ref, c * len_a, q0, w0, window, halo_a, finish_a)

    for q0, w0, window in blocks(seq, halo1):
        def finish_1(triple, q0=q0):
            o, _, denom = _mix2(triple, load3(pn_ref, slice(q0, q0 + Q_BLOCK)))
            o_ref[q0:q0 + Q_BLOCK, :] = (o / denom).astype(o_ref.dtype)
        add_unit(nat_ref, vn_ref, 0, q0, w0, window, halo1, finish_1)

    _run_pipelined(len(units), lambda j: units[j][0](), lambda j, s: units[j][1](s))


def _dilated_attn(proj, w1, w_gate, *, batch, seq):
    m = batch * seq
    head_spec = lambda h0: pl.BlockSpec((None, seq, HEAD_DIM), lambda b, h: (h0 + h, b, 0))
    w_in_specs, w_out_spec, w_out_shape = _cast_rows_specs([w1, w_gate], batch * B_HEADS,
                                                           lambda b, h: b * B_HEADS + h)
    qk_buf = pltpu.VMEM((2, seq, HEAD_DIM), jnp.bfloat16)
    v_ones_buf = pltpu.VMEM((seq, 2 * HEAD_DIM), jnp.bfloat16)
    return pl.pallas_call(
        functools.partial(_dilated_attn_kernel, seq=seq),
        grid=(batch, B_HEADS),
        in_specs=[head_spec(QB_HEAD0), head_spec(KB_HEAD0), head_spec(VB_HEAD0)] + w_in_specs,
        out_specs=[pl.BlockSpec((seq, HEAD_DIM), lambda b, h: (b, h)), w_out_spec],
        out_shape=[jax.ShapeDtypeStruct((m, B_QKV), jnp.bfloat16), w_out_shape],
        scratch_shapes=[qk_buf,
                        pltpu.VMEM((3, seq, HEAD_DIM), jnp.float32),
                        qk_buf, qk_buf,
                        v_ones_buf, v_ones_buf, v_ones_buf]
                       + [pltpu.VMEM((3, seq, HEAD_DIM), jnp.float32)] * 2,
        compiler_params=pltpu.CompilerParams(
            dimension_semantics=("arbitrary", "arbitrary"), vmem_limit_bytes=VMEM_LIMIT),
        name="dilated_attn",
    )(proj, proj, proj, w1, w_gate)


def _layer_norm(y, g, b):
    mu = jnp.mean(y, axis=-1, keepdims=True)
    d = y - mu
    var = jnp.mean(d * d, axis=-1, keepdims=True)
    return d * lax.rsqrt(var + LN_EPS) * g + b


def _rms_norm(y, g):
    return y * lax.rsqrt(jnp.mean(y * y, axis=-1, keepdims=True) + RMS_EPS) * g


def _mix_ln1_kernel(oa_ref, ob_ref, x_ref, wo_ref, gna_ref, gnb_ref, g_ref, b_ref, h_ref):
    for r in range(MIX_TM // MIX_ROWS):
        rows = slice(r * MIX_ROWS, (r + 1) * MIX_ROWS)
        ya = _rms_norm(oa_ref[rows, :].astype(jnp.float32), gna_ref[...])
        yb = _rms_norm(ob_ref[rows, :].astype(jnp.float32), gnb_ref[...])
        mix = (jnp.dot(ya.astype(jnp.bfloat16), wo_ref[:A_Q, :], preferred_element_type=jnp.float32)
               + jnp.dot(yb.astype(jnp.bfloat16), wo_ref[A_Q:, :], preferred_element_type=jnp.float32))
        h_ref[rows, :] = _layer_norm(ALPHA * x_ref[rows, :] + mix, g_ref[...], b_ref[...])


def _mix_ln1(oa, ob, x2, wo_b, gn_a, gn_b, ln_g, ln_b):
    m = x2.shape[0]
    row = lambda w: pl.BlockSpec((MIX_TM, w), lambda i: (i, 0))
    full = lambda r, w: pl.BlockSpec((r, w), lambda i: (0, 0))
    return pl.pallas_call(
        _mix_ln1_kernel,
        grid=(m // MIX_TM,),
        in_specs=[row(A_Q), row(B_QKV), row(D_MODEL), full(D_MODEL, D_MODEL), full(1, A_Q),
                  full(1, B_QKV), full(1, D_MODEL), full(1, D_MODEL)],
        out_specs=row(D_MODEL),
        out_shape=jax.ShapeDtypeStruct((m, D_MODEL), jnp.float32),
        compiler_params=pltpu.CompilerParams(
            dimension_semantics=("arbitrary",), vmem_limit_bytes=VMEM_LIMIT),
        name="mix_ln1",
    )(oa, ob, x2, wo_b, gn_a, gn_b, ln_g, ln_b)


def _ffn_ln2_kernel(h_ref, w1g_ref, w2_ref, p_ref, wple_ref, g_ref, b_ref, o_ref, hb_ref, gate_ref):
    j = pl.program_id(1)
    n_ff = D_FF // FFN_TF
    n_all = (D_FF + D_MODEL) // FFN_TF
    n_sub = FFN_TF // FFN_SUB

    @pl.when(j == 0)
    def _():
        h = h_ref[...]
        hb_ref[...] = h.astype(jnp.bfloat16)
        o_ref[...] = ALPHA * h

    def up(c):
        return jnp.dot(hb_ref[...], w1g_ref[:, c * FFN_SUB:(c + 1) * FFN_SUB],
                       preferred_element_type=jnp.float32)

    @pl.when(j < n_ff)
    def _():
        acts = []
        for c in range(n_sub):
            r = jnp.maximum(up(c), 0.0)
            acts.append((r * r).astype(jnp.bfloat16))
        for n in range(D_MODEL // FFN_SUB):
            cols = slice(n * FFN_SUB, (n + 1) * FFN_SUB)
            part = sum(jnp.dot(acts[c], w2_ref[c * FFN_SUB:(c + 1) * FFN_SUB, cols],
                               preferred_element_type=jnp.float32) for c in range(n_sub))
            o_ref[:, cols] += part

    def gate_chunk(t, c, rows):
        lo = t * FFN_TF + c * FFN_SUB
        a = jnp.dot(hb_ref[rows, :], w1g_ref[:, c * FFN_SUB:(c + 1) * FFN_SUB], preferred_element_type=jnp.float32)
        gate_ref[rows, lo:lo + FFN_SUB] = 0.5 * jnp.tanh(0.5 * a) + 0.5

    for t in range(n_all - n_ff - 1):
        @pl.when(j == n_ff + t)
        def _():
            for c in range(n_sub):
                gate_chunk(t, c, slice(None))

    @pl.when(j == n_all - 1)
    def _():
        for r in range(FFN_TM // FFN_LAST_ROWS):
            rows = slice(r * FFN_LAST_ROWS, (r + 1) * FFN_LAST_ROWS)
            for c in range(n_sub):
                gate_chunk(n_all - n_ff - 1, c, rows)
            ple = jnp.dot(p_ref[rows, :].astype(jnp.bfloat16), wple_ref[...], preferred_element_type=jnp.float32)
            o_ref[rows, :] = _layer_norm(o_ref[rows, :] + ple * gate_ref[rows, :], g_ref[...], b_ref[...])


def _ffn_ln2(h1, w1g_b, w2_b, p_flat, layer, wple_b, ln_g, ln_b):
    m = h1.shape[0]
    p_tile0 = layer * (m // FFN_TM)
    n_ff = D_FF // FFN_TF
    n_all = (D_FF + D_MODEL) // FFN_TF
    return pl.pallas_call(
        _ffn_ln2_kernel,
        grid=(m // FFN_TM, n_all),
        in_specs=[pl.BlockSpec((FFN_TM, D_MODEL), lambda i, j: (i, 0)),
                  pl.BlockSpec((D_MODEL, FFN_TF), lambda i, j: (0, j)),
                  pl.BlockSpec((FFN_TF, D_MODEL), lambda i, j: (jnp.minimum(j, n_ff - 1), 0)),
                  pl.BlockSpec((FFN_TM, PLE_DIM), lambda i, j: (p_tile0 + i, 0)),
                  pl.BlockSpec((PLE_DIM, D_MODEL), lambda i, j: (0, 0)),
                  pl.BlockSpec((1, D_MODEL), lambda i, j: (0, 0)),
                  pl.BlockSpec((1, D_MODEL), lambda i, j: (0, 0))],
        out_specs=pl.BlockSpec((FFN_TM, D_MODEL), lambda i, j: (i, 0)),
        out_shape=jax.ShapeDtypeStruct((m, D_MODEL), jnp.float32),
        scratch_shapes=[pltpu.VMEM((FFN_TM, D_MODEL), jnp.bfloat16),
                        pltpu.VMEM((FFN_TM, D_MODEL), jnp.float32)],
        compiler_params=pltpu.CompilerParams(
            dimension_semantics=("arbitrary", "arbitrary"), vmem_limit_bytes=VMEM_LIMIT),
        name="ffn_ln2",
    )(h1, w1g_b, w2_b, p_flat, wple_b, ln_g, ln_b)


def kernel(x, p, positions, w_in, sink_a, gn_a, gn_b, w_o, ln1_g, ln1_b, w1, w2, w_ple, w_ple_gate,
           ln2_g, ln2_b):
    batch, seq, d = x.shape
    m = batch * seq
    bf = jnp.bfloat16
    cos_t, sin_t, w_in_b = _rope_tables(positions, w_in)
    p_flat = p.reshape(-1, PLE_DIM)
    h = x.reshape(m, d)
    for i in range(DEPTH):
        proj = _in_proj(h, w_in_b, i, cos_t, sin_t)
        oa, w2_b, wo_b = _window_attn(proj, sink_a[i], w2[i], w_o[i], batch=batch, seq=seq)
        ob, w1g_b = _dilated_attn(proj, w1[i], w_ple_gate[i], batch=batch, seq=seq)
        h1 = _mix_ln1(oa, ob, h, wo_b, gn_a[i].reshape(1, -1), gn_b[i].reshape(1, -1),
                      ln1_g[i].reshape(1, -1), ln1_b[i].reshape(1, -1))
        h = _ffn_ln2(h1, w1g_b, w2_b, p_flat, i, w_ple[i].astype(bf),
                     ln2_g[i].reshape(1, -1), ln2_b[i].reshape(1, -1))
    return h.reshape(batch, seq, d)
```

```python
import functools

import jax
import jax.numpy as jnp
from jax import lax
from jax.experimental import pallas as pl
from jax.experimental.pallas import tpu as pltpu

D_MODEL = 2048
HEAD_DIM = 128
A_HEADS = 8
A_KV_HEADS = 2
A_WINDOW = 128
B_HEADS = 8
DILATED_PATTERNS = ((128, 1), (512, 4), (2048, 16))
ROT_DIM = HEAD_DIM // 4
ROPE_THETA = 500000.0
D_FF = 4 * D_MODEL
PLE_DIM = 256
DEPTH = 1
ALPHA = (2.0 * DEPTH) ** 0.25
LN_EPS = 1e-5
RMS_EPS = 1e-6
NEG_INF = -1e30
LOG2_E = 1.4426950408889634

A_Q = A_HEADS * HEAD_DIM
A_KV = A_KV_HEADS * HEAD_DIM
B_QKV = B_HEADS * HEAD_DIM
D_IN = A_Q + 2 * A_KV + 3 * B_QKV
N_PROJ_HEADS = D_IN // HEAD_DIM
QA_HEAD0 = 0
KA_HEAD0 = A_HEADS
VA_HEAD0 = A_HEADS + A_KV_HEADS
QB_HEAD0 = A_HEADS + 2 * A_KV_HEADS
KB_HEAD0 = QB_HEAD0 + B_HEADS
VB_HEAD0 = KB_HEAD0 + B_HEADS

LANES = 128
MXU_COLS = 256
BF16_TILE_ROWS = 16
Q_BLOCK = 128
SCORE_LOOKAHEAD = 5
VMEM_LIMIT = 56 * 1024 * 1024

PROJ_TM = 1024
PROJ_TN = 1536
MIX_TM = 512
MIX_ROWS = 256
FFN_TM = 512
FFN_TF = 1024
FFN_SUB = 512
FFN_LAST_ROWS = 256
ROPE_TM = 2048
KIND_Q, KIND_K, KIND_V = 0, 1, 2


def _rope_table_kernel(pos_ref, freq_ref, w_ref, cos_ref, sin_ref, wb_ref):
    _cast_rows([w_ref], wb_ref)
    ang = pos_ref[...].astype(jnp.float32) * freq_ref[...]
    lane = lax.broadcasted_iota(jnp.int32, ang.shape, 1)
    half = ROT_DIM // 2
    c = jnp.where(lane < ROT_DIM, jnp.cos(ang), 1.0)
    s = jnp.sin(ang)
    s = jnp.where(lane < half, -s, jnp.where(lane < ROT_DIM, s, 0.0))
    q_scale = LOG2_E * HEAD_DIM ** -0.5
    cos_ref[KIND_Q] = c * q_scale
    sin_ref[KIND_Q] = s * q_scale
    cos_ref[KIND_K] = c
    sin_ref[KIND_K] = s
    cos_ref[KIND_V] = jnp.ones_like(c)
    sin_ref[KIND_V] = jnp.zeros_like(s)


def _rope_tables(positions, w_in):
    m = positions.size
    half = ROT_DIM // 2
    inv_freq = ROPE_THETA ** (-jnp.arange(0, ROT_DIM, 2, dtype=jnp.float32) / ROT_DIM)
    freq = jnp.tile(inv_freq, LANES // half).reshape(1, LANES)
    pos = positions.reshape(m, 1)
    out = jax.ShapeDtypeStruct((3, m, LANES), jnp.float32)
    w_flat = w_in.reshape(-1, w_in.shape[-1])
    w_specs, wb_spec, wb_shape = _cast_rows_specs([w_flat], m // ROPE_TM, lambda i: i)
    cos_t, sin_t, w_b = pl.pallas_call(
        _rope_table_kernel,
        grid=(m // ROPE_TM,),
        in_specs=[pl.BlockSpec((ROPE_TM, 1), lambda i: (i, 0)),
                  pl.BlockSpec((1, LANES), lambda i: (0, 0))] + w_specs,
        out_specs=[pl.BlockSpec((3, ROPE_TM, LANES), lambda i: (0, i, 0))] * 2 + [wb_spec],
        out_shape=[out, out, wb_shape],
        name="rope_table",
    )(pos, freq, w_flat)
    return cos_t, sin_t, w_b.reshape(w_in.shape)


def _in_proj_kernel(x_ref, w_ref, cos_ref, sin_ref, o_ref, xb_ref):
    j = pl.program_id(1)

    @pl.when(j == 0)
    def _():
        xb_ref[...] = x_ref[...].astype(jnp.bfloat16)

    half = ROT_DIM // 2
    heads_per_tile = PROJ_TN // HEAD_DIM
    heads_per_chunk = MXU_COLS // HEAD_DIM
    n_chunks = PROJ_TN // MXU_COLS

    def chunk(c, rows, n_rows):
        lane = lax.broadcasted_iota(jnp.int32, (n_rows, LANES), 1)
        partner = jnp.where(lane < half, lane + half, jnp.where(lane < ROT_DIM, lane - half, lane))
        acc = jnp.dot(xb_ref[rows, :], w_ref[:, c * MXU_COLS:(c + 1) * MXU_COLS],
                      preferred_element_type=jnp.float32)
        for g in range(heads_per_chunk):
            local = c * heads_per_chunk + g
            head = j * heads_per_tile + local
            is_q = (head < KA_HEAD0) | ((head >= QB_HEAD0) & (head < KB_HEAD0))
            is_k = ((head >= KA_HEAD0) & (head < VA_HEAD0)) | ((head >= KB_HEAD0) & (head < VB_HEAD0))
            kind = jnp.where(is_q, KIND_Q, jnp.where(is_k, KIND_K, KIND_V))
            t = acc[:, g * HEAD_DIM:(g + 1) * HEAD_DIM]
            swapped = jnp.take_along_axis(t, partner, axis=1)
            o_ref[local, rows, :] = t * cos_ref[kind, rows, :] + swapped * sin_ref[kind, rows, :]

    for c in range(n_chunks - 2):
        chunk(c, slice(None), PROJ_TM)
    for r in range(2):
        for c in range(n_chunks - 2, n_chunks):
            chunk(c, slice(r * (PROJ_TM // 2), (r + 1) * (PROJ_TM // 2)), PROJ_TM // 2)


def _in_proj(x2, w_in_b, layer, cos_t, sin_t):
    m = x2.shape[0]
    heads_per_tile = PROJ_TN // HEAD_DIM
    return pl.pallas_call(
        _in_proj_kernel,
        grid=(m // PROJ_TM, D_IN // PROJ_TN),
        in_specs=[pl.BlockSpec((PROJ_TM, D_MODEL), lambda i, j: (i, 0)),
                  pl.BlockSpec((None, D_MODEL, PROJ_TN), lambda i, j: (layer, 0, j)),
                  pl.BlockSpec((3, PROJ_TM, LANES), lambda i, j: (0, i, 0)),
                  pl.BlockSpec((3, PROJ_TM, LANES), lambda i, j: (0, i, 0))],
        out_specs=pl.BlockSpec((heads_per_tile, PROJ_TM, HEAD_DIM), lambda i, j: (j, i, 0)),
        out_shape=jax.ShapeDtypeStruct((N_PROJ_HEADS, m, HEAD_DIM), jnp.float32),
        scratch_shapes=[pltpu.VMEM((PROJ_TM, D_MODEL), jnp.bfloat16)],
        compiler_params=pltpu.CompilerParams(
            dimension_semantics=("arbitrary", "arbitrary"), vmem_limit_bytes=VMEM_LIMIT),
        name="in_proj",
    )(x2, w_in_b, cos_t, sin_t)


def _band_bias(delta, rows, cols, halo):
    d = delta + lax.broadcasted_iota(jnp.int32, (rows, cols), 0) - lax.broadcasted_iota(jnp.int32, (rows, cols), 1)
    return jnp.where(jnp.abs(d) <= halo, 0.0, NEG_INF).astype(jnp.float32)


def _softmax_pv(s, v, sink=None):
    m = jnp.max(s, axis=-1, keepdims=True)
    if sink is not None:
        m = jnp.maximum(m, sink)
    e = jnp.exp2(s - m)
    acc = jnp.dot(e.astype(v.dtype), v, preferred_element_type=jnp.float32)
    if v.shape[1] == HEAD_DIM:
        o, denom = acc, jnp.sum(e, axis=-1, keepdims=True)
    else:
        o, denom = acc[:, :HEAD_DIM], acc[:, HEAD_DIM:]
    if sink is not None:
        denom = denom + jnp.exp2(sink - m)
    return o, m, denom


def _store_v_ones(vo_ref, v):
    vo_ref[:, :HEAD_DIM] = v.astype(vo_ref.dtype)
    vo_ref[:, HEAD_DIM:] = jnp.ones(v.shape, vo_ref.dtype)


def _qk(q, k):
    return lax.dot_general(q, k, (((1,), (1,)), ((), ())), preferred_element_type=jnp.float32)


def _window_start(q0, halo, window, length):
    return min(max(q0 - halo, 0), length - window)


def _run_pipelined(n, scores, finish, depth=SCORE_LOOKAHEAD):
    pending = {}
    for j in range(n + depth):
        if j < n:
            pending[j] = scores(j)
        if j >= depth:
            finish(j - depth, pending.pop(j - depth))


def _cast_rows_specs(weights, grid_steps, step_of):
    rows = weights[0].shape[0] // grid_steps
    assert all(w.shape[0] == weights[0].shape[0] for w in weights) and rows % BF16_TILE_ROWS == 0
    cols = sum(w.shape[1] for w in weights)
    in_specs = [pl.BlockSpec((rows, w.shape[1]), lambda *g: (step_of(*g), 0)) for w in weights]
    out_spec = pl.BlockSpec((rows, cols), lambda *g: (step_of(*g), 0))
    out_shape = jax.ShapeDtypeStruct((weights[0].shape[0], cols), jnp.bfloat16)
    return in_specs, out_spec, out_shape


def _cast_rows(w_refs, out_ref):
    lo = 0
    for w_ref in w_refs:
        out_ref[:, lo:lo + w_ref.shape[1]] = w_ref[...].astype(out_ref.dtype)
        lo += w_ref.shape[1]


def _window_attn_kernel(q_ref, k_ref, v_ref, sink_ref, wo_ref, o_ref, wob_ref, qkv_ref, *, seq):
    _cast_rows([wo_ref], wob_ref)
    window = Q_BLOCK + 2 * A_WINDOW
    head = pl.program_id(1)
    sink = sink_ref[head] * LOG2_E
    qkv_ref[0] = q_ref[...].astype(jnp.bfloat16)

    @pl.when(head % (A_HEADS // A_KV_HEADS) == 0)
    def _():
        qkv_ref[1] = k_ref[...].astype(jnp.bfloat16)
        qkv_ref[2] = v_ref[...].astype(jnp.bfloat16)

    biases = {}

    def scores(j):
        q0 = j * Q_BLOCK
        w0 = _window_start(q0, A_WINDOW, window, seq)
        if q0 - w0 not in biases:
            biases[q0 - w0] = _band_bias(q0 - w0, Q_BLOCK, window, A_WINDOW)
        return _qk(qkv_ref[0, q0:q0 + Q_BLOCK, :], qkv_ref[1, w0:w0 + window, :]) + biases[q0 - w0]

    def finish(j, s):
        q0 = j * Q_BLOCK
        w0 = _window_start(q0, A_WINDOW, window, seq)
        o, _, denom = _softmax_pv(s, qkv_ref[2, w0:w0 + window, :], sink)
        o_ref[q0:q0 + Q_BLOCK, :] = (o / denom).astype(o_ref.dtype)

    _run_pipelined(seq // Q_BLOCK, scores, finish)


def _window_attn(proj, sink, w_o, *, batch, seq):
    m = batch * seq
    grp = A_HEADS // A_KV_HEADS
    head_spec = lambda h0, div: pl.BlockSpec((None, seq, HEAD_DIM), lambda b, h: (h0 + h // div, b, 0))
    step_of = lambda b, h: b * A_HEADS + h
    wo_in, wo_out, wo_shape = _cast_rows_specs([w_o], batch * A_HEADS, step_of)
    return pl.pallas_call(
        functools.partial(_window_attn_kernel, seq=seq),
        grid=(batch, A_HEADS),
        in_specs=[head_spec(QA_HEAD0, 1), head_spec(KA_HEAD0, grp), head_spec(VA_HEAD0, grp),
                  pl.BlockSpec(memory_space=pltpu.SMEM)] + wo_in,
        out_specs=[pl.BlockSpec((seq, HEAD_DIM), lambda b, h: (b, h)), wo_out],
        out_shape=[jax.ShapeDtypeStruct((m, A_Q), jnp.bfloat16), wo_shape],
        scratch_shapes=[pltpu.VMEM((3, seq, HEAD_DIM), jnp.bfloat16)],
        compiler_params=pltpu.CompilerParams(
            dimension_semantics=("arbitrary", "arbitrary"), vmem_limit_bytes=VMEM_LIMIT),
        name="window_attn",
    )(proj, proj, proj, sink, w_o)


def _mix2(a, b):
    m = jnp.maximum(a[1], b[1])
    fa = jnp.exp2(a[1] - m)
    fb = jnp.exp2(b[1] - m)
    return fa * a[0] + fb * b[0], m, fa * a[2] + fb * b[2]


def _dilated_attn_kernel(q_ref, k_ref, v_ref, w1_ref, wg_ref, w2_ref, o_ref, w1gb_ref, w2b_ref,
                         nat_ref, ca_ref, cab_ref, cbb_ref, vn_ref, va_ref, vb_ref, pc_ref, pn_ref, *, seq):
    _cast_rows([w1_ref, wg_ref], w1gb_ref)
    _cast_rows([w2_ref], w2b_ref)
    bf = jnp.bfloat16
    (w1, r1), (wa, ra), (wb, rb) = sorted(DILATED_PATTERNS, key=lambda pat: pat[1])
    assert r1 == 1 and rb % ra == 0
    sub = rb // ra
    len_a, len_b = seq // ra, seq // rb
    halo1, halo_a, halo_b = w1 // 2, wa // (2 * ra), wb // (2 * rb)
    biases = {}

    def band_scores(qk_src, base, q0, w0, window, halo):
        key = (q0 - w0, Q_BLOCK, window, halo)
        if key not in biases:
            biases[key] = _band_bias(*key)
        return _qk(qk_src[0, base + q0:base + q0 + Q_BLOCK, :],
                   qk_src[1, base + w0:base + w0 + window, :]) + biases[key]

    def band_output(s, v_src, base, w0, window):
        o, m, denom = _softmax_pv(s, v_src[base + w0:base + w0 + window, :])
        return o, jnp.broadcast_to(m, (Q_BLOCK, LANES)), denom

    def blocks(length, halo):
        window = min(Q_BLOCK + 2 * halo, length)
        for j in range(length // Q_BLOCK):
            q0 = j * Q_BLOCK
            yield q0, _window_start(q0, halo, window, length), window

    def load3(ref, rows):
        return tuple(ref[t, rows, :] for t in range(3))

    def store3(ref, rows, triple):
        for t in range(3):
            ref[t, rows, :] = triple[t]

    for t, src in enumerate((q_ref, k_ref, v_ref)):
        if t < 2:
            nat_ref[t] = src[...].astype(bf)
        else:
            _store_v_ones(vn_ref, src[...])
        for c in range(ra):
            x = src[pl.ds(c, len_a, stride=ra), :]
            ca_ref[t, c * len_a:(c + 1) * len_a, :] = x
            if t < 2:
                cab_ref[t, c * len_a:(c + 1) * len_a, :] = x.astype(bf)
            else:
                _store_v_ones(va_ref.at[c * len_a:(c + 1) * len_a, :], x)
    for t in range(3):
        for c in range(ra):
            for u in range(sub):
                rows = slice((c * sub + u) * len_b, (c * sub + u + 1) * len_b)
                x = ca_ref[t, pl.ds(c * len_a + u, len_b, stride=sub), :]
                if t < 2:
                    cbb_ref[t, rows, :] = x.astype(bf)
                else:
                    _store_v_ones(vb_ref.at[rows, :], x)

    units = []

    def add_unit(qk_src, v_src, base, q0, w0, window, halo, finish):
        units.append((lambda: band_scores(qk_src, base, q0, w0, window, halo),
                      lambda s: finish(band_output(s, v_src, base, w0, window))))

    for c in range(ra):
        for u in range(sub):
            cu = c * sub + u
            for q0, w0, window in blocks(len_b, halo_b):
                def finish_b(triple, c=c, u=u, q0=q0):
                    store3(pc_ref, pl.ds(c * len_a + q0 * sub + u, Q_BLOCK, stride=sub), triple)
                add_unit(cbb_ref, vb_ref, cu * len_b, q0, w0, window, halo_b, finish_b)

    for c in range(ra):
        for q0, w0, window in blocks(len_a, halo_a):
            def finish_a(triple, c=c, q0=q0):
                merged = _mix2(triple, load3(pc_ref, slice(c * len_a + q0, c * len_a + q0 + Q_BLOCK)))
                store3(pn_ref, pl.ds(q0 * ra + c, Q_BLOCK, stride=ra), merged)
            add_unit(cab_ref, va_ref, c * len_a, q0, w0, window, halo_a, finish_a)

    for q0, w0, window in blocks(seq, halo1):
        def finish_1(triple, q0=q0):
            o, _, denom = _mix2(triple, load3(pn_ref, slice(q0, q0 + Q_BLOCK)))
            o_ref[q0:q0 + Q_BLOCK, :] = (o / denom).astype(o_ref.dtype)
        add_unit(nat_ref, vn_ref, 0, q0, w0, window, halo1, finish_1)

    _run_pipelined(len(units), lambda j: units[j][0](), lambda j, s: units[j][1](s))


def _dilated_attn(proj, w1, w_gate, w2, *, batch, seq):
    m = batch * seq
    head_spec = lambda h0: pl.BlockSpec((None, seq, HEAD_DIM), lambda b, h: (h0 + h, b, 0))
    step_of = lambda b, h: b * B_HEADS + h
    w_in_specs, w_out_spec, w_out_shape = _cast_rows_specs([w1, w_gate], batch * B_HEADS, step_of)
    w2_in_specs, w2_out_spec, w2_out_shape = _cast_rows_specs([w2], batch * B_HEADS, step_of)
    qk_buf = pltpu.VMEM((2, seq, HEAD_DIM), jnp.bfloat16)
    v_ones_buf = pltpu.VMEM((seq, 2 * HEAD_DIM), jnp.bfloat16)
    return pl.pallas_call(
        functools.partial(_dilated_attn_kernel, seq=seq),
        grid=(batch, B_HEADS),
        in_specs=[head_spec(QB_HEAD0), head_spec(KB_HEAD0), head_spec(VB_HEAD0)] + w_in_specs + w2_in_specs,
        out_specs=[pl.BlockSpec((seq, HEAD_DIM), lambda b, h: (b, h)), w_out_spec, w2_out_spec],
        out_shape=[jax.ShapeDtypeStruct((m, B_QKV), jnp.bfloat16), w_out_shape, w2_out_shape],
        scratch_shapes=[qk_buf,
                        pltpu.VMEM((3, seq, HEAD_DIM), jnp.float32),
                        qk_buf, qk_buf,
                        v_ones_buf, v_ones_buf, v_ones_buf]
                       + [pltpu.VMEM((3, seq, HEAD_DIM), jnp.float32)] * 2,
        compiler_params=pltpu.CompilerParams(
            dimension_semantics=("arbitrary", "arbitrary"), vmem_limit_bytes=VMEM_LIMIT),
        name="dilated_attn",
    )(proj, proj, proj, w1, w_gate, w2)


def _layer_norm(y, g, b):
    mu = jnp.mean(y, axis=-1, keepdims=True)
    d = y - mu
    var = jnp.mean(d * d, axis=-1, keepdims=True)
    return d * lax.rsqrt(var + LN_EPS) * g + b


def _rms_norm(y, g):
    return y * lax.rsqrt(jnp.mean(y * y, axis=-1, keepdims=True) + RMS_EPS) * g


def _mix_ln1_kernel(oa_ref, ob_ref, x_ref, wo_ref, gna_ref, gnb_ref, g_ref, b_ref, h_ref):
    for r in range(MIX_TM // MIX_ROWS):
        rows = slice(r * MIX_ROWS, (r + 1) * MIX_ROWS)
        ya = _rms_norm(oa_ref[rows, :].astype(jnp.float32), gna_ref[...])
        yb = _rms_norm(ob_ref[rows, :].astype(jnp.float32), gnb_ref[...])
        mix = (jnp.dot(ya.astype(jnp.bfloat16), wo_ref[:A_Q, :], preferred_element_type=jnp.float32)
               + jnp.dot(yb.astype(jnp.bfloat16), wo_ref[A_Q:, :], preferred_element_type=jnp.float32))
        h_ref[rows, :] = _layer_norm(ALPHA * x_ref[rows, :] + mix, g_ref[...], b_ref[...])


def _mix_ln1(oa, ob, x2, wo_b, gn_a, gn_b, ln_g, ln_b):
    m = x2.shape[0]
    row = lambda w: pl.BlockSpec((MIX_TM, w), lambda i: (i, 0))
    full = lambda r, w: pl.BlockSpec((r, w), lambda i: (0, 0))
    return pl.pallas_call(
        _mix_ln1_kernel,
        grid=(m // MIX_TM,),
        in_specs=[row(A_Q), row(B_QKV), row(D_MODEL), full(D_MODEL, D_MODEL), full(1, A_Q),
                  full(1, B_QKV), full(1, D_MODEL), full(1, D_MODEL)],
        out_specs=row(D_MODEL),
        out_shape=jax.ShapeDtypeStruct((m, D_MODEL), jnp.float32),
        compiler_params=pltpu.CompilerParams(
            dimension_semantics=("arbitrary",), vmem_limit_bytes=VMEM_LIMIT),
        name="mix_ln1",
    )(oa, ob, x2, wo_b, gn_a, gn_b, ln_g, ln_b)


def _ffn_ln2_kernel(h_ref, w1g_ref, w2_ref, p_ref, wple_ref, g_ref, b_ref, o_ref, hb_ref, gate_ref):
    j = pl.program_id(1)
    n_ff = D_FF // FFN_TF
    n_all = (D_FF + D_MODEL) // FFN_TF
    n_sub = FFN_TF // FFN_SUB

    @pl.when(j == 0)
    def _():
        h = h_ref[...]
        hb_ref[...] = h.astype(jnp.bfloat16)
        o_ref[...] = ALPHA * h

    def up(c):
        return jnp.dot(hb_ref[...], w1g_ref[:, c * FFN_SUB:(c + 1) * FFN_SUB],
                       preferred_element_type=jnp.float32)

    @pl.when(j < n_ff)
    def _():
        acts = []
        for c in range(n_sub):
            r = jnp.maximum(up(c), 0.0)
            acts.append((r * r).astype(jnp.bfloat16))
        for n in range(D_MODEL // FFN_SUB):
            cols = slice(n * FFN_SUB, (n + 1) * FFN_SUB)
            part = sum(jnp.dot(acts[c], w2_ref[c * FFN_SUB:(c + 1) * FFN_SUB, cols],
                               preferred_element_type=jnp.float32) for c in range(n_sub))
            o_ref[:, cols] += part

    def gate_chunk(t, c, rows):
        lo = t * FFN_TF + c * FFN_SUB
        a = jnp.dot(hb_ref[rows, :], w1g_ref[:, c * FFN_SUB:(c + 1) * FFN_SUB], preferred_element_type=jnp.float32)
        gate_ref[rows, lo:lo + FFN_SUB] = 0.5 * jnp.tanh(0.5 * a) + 0.5

    for t in range(n_all - n_ff - 1):
        @pl.when(j == n_ff + t)
        def _():
            for c in range(n_sub):
                gate_chunk(t, c, slice(None))

    @pl.when(j == n_all - 1)
    def _():
        for r in range(FFN_TM // FFN_LAST_ROWS):
            rows = slice(r * FFN_LAST_ROWS, (r + 1) * FFN_LAST_ROWS)
            for c in range(n_sub):
                gate_chunk(n_all - n_ff - 1, c, rows)
            ple = jnp.dot(p_ref[rows, :].astype(jnp.bfloat16), wple_ref[...], preferred_element_type=jnp.float32)
            o_ref[rows, :] = _layer_norm(o_ref[rows, :] + ple * gate_ref[rows, :], g_ref[...], b_ref[...])


def _ffn_ln2(h1, w1g_b, w2_b, p_flat, layer, wple_b, ln_g, ln_b):
    m = h1.shape[0]
    p_tile0 = layer * (m // FFN_TM)
    n_ff = D_FF // FFN_TF
    n_all = (D_FF + D_MODEL) // FFN_TF
    return pl.pallas_call(
        _ffn_ln2_kernel,
        grid=(m // FFN_TM, n_all),
        in_specs=[pl.BlockSpec((FFN_TM, D_MODEL), lambda i, j: (i, 0)),
                  pl.BlockSpec((D_MODEL, FFN_TF), lambda i, j: (0, j)),
                  pl.BlockSpec((FFN_TF, D_MODEL), lambda i, j: (jnp.minimum(j, n_ff - 1), 0)),
                  pl.BlockSpec((FFN_TM, PLE_DIM), lambda i, j: (p_tile0 + i, 0)),
                  pl.BlockSpec((PLE_DIM, D_MODEL), lambda i, j: (0, 0)),
                  pl.BlockSpec((1, D_MODEL), lambda i, j: (0, 0)),
                  pl.BlockSpec((1, D_MODEL), lambda i, j: (0, 0))],
        out_specs=pl.BlockSpec((FFN_TM, D_MODEL), lambda i, j: (i, 0)),
        out_shape=jax.ShapeDtypeStruct((m, D_MODEL), jnp.float32),
        scratch_shapes=[pltpu.VMEM((FFN_TM, D_MODEL), jnp.bfloat16),
                        pltpu.VMEM((FFN_TM, D_MODEL), jnp.float32)],
        compiler_params=pltpu.CompilerParams(
            dimension_semantics=("arbitrary", "arbitrary"), vmem_limit_bytes=VMEM_LIMIT),
        name="ffn_ln2",
    )(h1, w1g_b, w2_b, p_flat, wple_b, ln_g, ln_b)


def kernel(x, p, positions, w_in, sink_a, gn_a, gn_b, w_o, ln1_g, ln1_b, w1, w2, w_ple, w_ple_gate,
           ln2_g, ln2_b):
    batch, seq, d = x.shape
    m = batch * seq
    bf = jnp.bfloat16
    cos_t, sin_t, w_in_b = _rope_tables(positions, w_in)
    p_flat = p.reshape(-1, PLE_DIM)
    h = x.reshape(m, d)
    for i in range(DEPTH):
        proj = _in_proj(h, w_in_b, i, cos_t, sin_t)
        oa, wo_b = _window_attn(proj, sink_a[i], w_o[i], batch=batch, seq=seq)
        ob, w1g_b, w2_b = _dilated_attn(proj, w1[i], w_ple_gate[i], w2[i], batch=batch, seq=seq)
        h1 = _mix_ln1(oa, ob, h, wo_b, gn_a[i].reshape(1, -1), gn_b[i].reshape(1, -1),
                      ln1_g[i].reshape(1, -1), ln1_b[i].reshape(1, -1))
        h = _ffn_ln2(h1, w1g_b, w2_b, p_flat, i, w_ple[i].astype(bf),
                     ln2_g[i].reshape(1, -1), ln2_b[i].reshape(1, -1))
    return h.reshape(batch, seq, d)
```

```python
import functools

import jax
import jax.numpy as jnp
from jax import lax
from jax.experimental import pallas as pl
from jax.experimental.pallas import tpu as pltpu

D_MODEL = 2048
HEAD_DIM = 128
A_HEADS = 8
A_KV_HEADS = 2
A_WINDOW = 128
B_HEADS = 8
DILATED_PATTERNS = ((128, 1), (512, 4), (2048, 16))
ROT_DIM = HEAD_DIM // 4
ROPE_THETA = 500000.0
D_FF = 4 * D_MODEL
PLE_DIM = 256
DEPTH = 1
ALPHA = (2.0 * DEPTH) ** 0.25
LN_EPS = 1e-5
RMS_EPS = 1e-6
NEG_INF = -1e30
LOG2_E = 1.4426950408889634

A_Q = A_HEADS * HEAD_DIM
A_KV = A_KV_HEADS * HEAD_DIM
B_QKV = B_HEADS * HEAD_DIM
D_IN = A_Q + 2 * A_KV + 3 * B_QKV
N_PROJ_HEADS = D_IN // HEAD_DIM
QA_HEAD0 = 0
KA_HEAD0 = A_HEADS
VA_HEAD0 = A_HEADS + A_KV_HEADS
QB_HEAD0 = A_HEADS + 2 * A_KV_HEADS
KB_HEAD0 = QB_HEAD0 + B_HEADS
VB_HEAD0 = KB_HEAD0 + B_HEADS

LANES = 128
MXU_COLS = 256
BF16_TILE_ROWS = 16
Q_BLOCK = 128
SCORE_LOOKAHEAD = 5
VMEM_LIMIT = 56 * 1024 * 1024

PROJ_TM = 1024
PROJ_TN = 1536
MIX_TM = 512
MIX_ROWS = 256
FFN_TM = 512
FFN_TF = 1024
FFN_SUB = 512
FFN_LAST_ROWS = 256
ROPE_TM = 2048
KIND_Q, KIND_K, KIND_V = 0, 1, 2


def _rope_table_kernel(pos_ref, freq_ref, w_ref, cos_ref, sin_ref, wb_ref):
    _cast_rows([w_ref], wb_ref)
    pos = pos_ref[...].astype(jnp.float32)
    tiles = [jnp.broadcast_to(pos[r:r + 1, :], (LANES, LANES)).T for r in range(pos.shape[0])]
    ang = jnp.concatenate(tiles, axis=0) * freq_ref[...]
    lane = lax.broadcasted_iota(jnp.int32, ang.shape, 1)
    half = ROT_DIM // 2
    c = jnp.where(lane < ROT_DIM, jnp.cos(ang), 1.0)
    s = jnp.sin(ang)
    s = jnp.where(lane < half, -s, jnp.where(lane < ROT_DIM, s, 0.0))
    q_scale = LOG2_E * HEAD_DIM ** -0.5
    cos_ref[KIND_Q] = c * q_scale
    sin_ref[KIND_Q] = s * q_scale
    cos_ref[KIND_K] = c
    sin_ref[KIND_K] = s
    cos_ref[KIND_V] = jnp.ones_like(c)
    sin_ref[KIND_V] = jnp.zeros_like(s)


def _rope_tables(positions, w_in):
    m = positions.size
    half = ROT_DIM // 2
    inv_freq = ROPE_THETA ** (-jnp.arange(0, ROT_DIM, 2, dtype=jnp.float32) / ROT_DIM)
    freq = jnp.tile(inv_freq, LANES // half).reshape(1, LANES)
    pos = positions.reshape(m // LANES, LANES)
    out = jax.ShapeDtypeStruct((3, m, LANES), jnp.float32)
    w_flat = w_in.reshape(-1, w_in.shape[-1])
    w_specs, wb_spec, wb_shape = _cast_rows_specs([w_flat], m // ROPE_TM, lambda i: i)
    cos_t, sin_t, w_b = pl.pallas_call(
        _rope_table_kernel,
        grid=(m // ROPE_TM,),
        in_specs=[pl.BlockSpec((ROPE_TM // LANES, LANES), lambda i: (i, 0)),
                  pl.BlockSpec((1, LANES), lambda i: (0, 0))] + w_specs,
        out_specs=[pl.BlockSpec((3, ROPE_TM, LANES), lambda i: (0, i, 0))] * 2 + [wb_spec],
        out_shape=[out, out, wb_shape],
        name="rope_table",
    )(pos, freq, w_flat)
    return cos_t, sin_t, w_b.reshape(w_in.shape)


def _in_proj_kernel(x_ref, w_ref, cos_ref, sin_ref, o_ref, xb_ref):
    j = pl.program_id(1)

    @pl.when(j == 0)
    def _():
        xb_ref[...] = x_ref[...].astype(jnp.bfloat16)

    half = ROT_DIM // 2
    heads_per_tile = PROJ_TN // HEAD_DIM
    heads_per_chunk = MXU_COLS // HEAD_DIM
    n_chunks = PROJ_TN // MXU_COLS

    def chunk(c, rows, n_rows):
        lane = lax.broadcasted_iota(jnp.int32, (n_rows, LANES), 1)
        partner = jnp.where(lane < half, lane + half, jnp.where(lane < ROT_DIM, lane - half, lane))
        acc = jnp.dot(xb_ref[rows, :], w_ref[:, c * MXU_COLS:(c + 1) * MXU_COLS],
                      preferred_element_type=jnp.float32)
        for g in range(heads_per_chunk):
            local = c * heads_per_chunk + g
            head = j * heads_per_tile + local
            is_q = (head < KA_HEAD0) | ((head >= QB_HEAD0) & (head < KB_HEAD0))
            is_k = ((head >= KA_HEAD0) & (head < VA_HEAD0)) | ((head >= KB_HEAD0) & (head < VB_HEAD0))
            kind = jnp.where(is_q, KIND_Q, jnp.where(is_k, KIND_K, KIND_V))
            t = acc[:, g * HEAD_DIM:(g + 1) * HEAD_DIM]
            swapped = jnp.take_along_axis(t, partner, axis=1)
            o_ref[local, rows, :] = t * cos_ref[kind, rows, :] + swapped * sin_ref[kind, rows, :]

    for c in range(n_chunks - 2):
        chunk(c, slice(None), PROJ_TM)
    for r in range(2):
        for c in range(n_chunks - 2, n_chunks):
            chunk(c, slice(r * (PROJ_TM // 2), (r + 1) * (PROJ_TM // 2)), PROJ_TM // 2)


def _in_proj(x2, w_in_b, layer, cos_t, sin_t):
    m = x2.shape[0]
    heads_per_tile = PROJ_TN // HEAD_DIM
    return pl.pallas_call(
        _in_proj_kernel,
        grid=(m // PROJ_TM, D_IN // PROJ_TN),
        in_specs=[pl.BlockSpec((PROJ_TM, D_MODEL), lambda i, j: (i, 0)),
                  pl.BlockSpec((None, D_MODEL, PROJ_TN), lambda i, j: (layer, 0, j)),
                  pl.BlockSpec((3, PROJ_TM, LANES), lambda i, j: (0, i, 0)),
                  pl.BlockSpec((3, PROJ_TM, LANES), lambda i, j: (0, i, 0))],
        out_specs=pl.BlockSpec((heads_per_tile, PROJ_TM, HEAD_DIM), lambda i, j: (j, i, 0)),
        out_shape=jax.ShapeDtypeStruct((N_PROJ_HEADS, m, HEAD_DIM), jnp.float32),
        scratch_shapes=[pltpu.VMEM((PROJ_TM, D_MODEL), jnp.bfloat16)],
        compiler_params=pltpu.CompilerParams(
            dimension_semantics=("arbitrary", "arbitrary"), vmem_limit_bytes=VMEM_LIMIT),
        name="in_proj",
    )(x2, w_in_b, cos_t, sin_t)


def _band_bias(delta, rows, cols, halo):
    d = delta + lax.broadcasted_iota(jnp.int32, (rows, cols), 0) - lax.broadcasted_iota(jnp.int32, (rows, cols), 1)
    return jnp.where(jnp.abs(d) <= halo, 0.0, NEG_INF).astype(jnp.float32)


def _softmax_pv(s, v, sink=None):
    m = jnp.max(s, axis=-1, keepdims=True)
    if sink is not None:
        m = jnp.maximum(m, sink)
    e = jnp.exp2(s - m)
    acc = jnp.dot(e.astype(v.dtype), v, preferred_element_type=jnp.float32)
    if v.shape[1] == HEAD_DIM:
        o, denom = acc, jnp.sum(e, axis=-1, keepdims=True)
    else:
        o, denom = acc[:, :HEAD_DIM], acc[:, HEAD_DIM:]
    if sink is not None:
        denom = denom + jnp.exp2(sink - m)
    return o, m, denom


def _store_v_ones(vo_ref, v):
    vo_ref[:, :HEAD_DIM] = v.astype(vo_ref.dtype)
    vo_ref[:, HEAD_DIM:] = jnp.ones(v.shape, vo_ref.dtype)


def _qk(q, k):
    return lax.dot_general(q, k, (((1,), (1,)), ((), ())), preferred_element_type=jnp.float32)


def _window_start(q0, halo, window, length):
    return min(max(q0 - halo, 0), length - window)


def _run_pipelined(n, scores, finish, depth=SCORE_LOOKAHEAD):
    pending = {}
    for j in range(n + depth):
        if j < n:
            pending[j] = scores(j)
        if j >= depth:
            finish(j - depth, pending.pop(j - depth))


def _cast_rows_specs(weights, grid_steps, step_of):
    rows = weights[0].shape[0] // grid_steps
    assert all(w.shape[0] == weights[0].shape[0] for w in weights) and rows % BF16_TILE_ROWS == 0
    cols = sum(w.shape[1] for w in weights)
    in_specs = [pl.BlockSpec((rows, w.shape[1]), lambda *g: (step_of(*g), 0)) for w in weights]
    out_spec = pl.BlockSpec((rows, cols), lambda *g: (step_of(*g), 0))
    out_shape = jax.ShapeDtypeStruct((weights[0].shape[0], cols), jnp.bfloat16)
    return in_specs, out_spec, out_shape


def _cast_rows(w_refs, out_ref):
    lo = 0
    for w_ref in w_refs:
        out_ref[:, lo:lo + w_ref.shape[1]] = w_ref[...].astype(out_ref.dtype)
        lo += w_ref.shape[1]


def _window_attn_kernel(q_ref, k_ref, v_ref, sink_ref, wo_ref, o_ref, wob_ref, qkv_ref, *, seq):
    _cast_rows([wo_ref], wob_ref)
    window = Q_BLOCK + 2 * A_WINDOW
    head = pl.program_id(1)
    sink = sink_ref[head] * LOG2_E
    qkv_ref[0] = q_ref[...].astype(jnp.bfloat16)

    @pl.when(head % (A_HEADS // A_KV_HEADS) == 0)
    def _():
        qkv_ref[1] = k_ref[...].astype(jnp.bfloat16)
        qkv_ref[2] = v_ref[...].astype(jnp.bfloat16)

    biases = {}

    def scores(j):
        q0 = j * Q_BLOCK
        w0 = _window_start(q0, A_WINDOW, window, seq)
        if q0 - w0 not in biases:
            biases[q0 - w0] = _band_bias(q0 - w0, Q_BLOCK, window, A_WINDOW)
        return _qk(qkv_ref[0, q0:q0 + Q_BLOCK, :], qkv_ref[1, w0:w0 + window, :]) + biases[q0 - w0]

    def finish(j, s):
        q0 = j * Q_BLOCK
        w0 = _window_start(q0, A_WINDOW, window, seq)
        o, _, denom = _softmax_pv(s, qkv_ref[2, w0:w0 + window, :], sink)
        o_ref[q0:q0 + Q_BLOCK, :] = (o / denom).astype(o_ref.dtype)

    _run_pipelined(seq // Q_BLOCK, scores, finish)


def _window_attn(proj, sink, w_o, *, batch, seq):
    m = batch * seq
    grp = A_HEADS // A_KV_HEADS
    head_spec = lambda h0, div: pl.BlockSpec((None, seq, HEAD_DIM), lambda b, h: (h0 + h // div, b, 0))
    step_of = lambda b, h: b * A_HEADS + h
    wo_in, wo_out, wo_shape = _cast_rows_specs([w_o], batch * A_HEADS, step_of)
    return pl.pallas_call(
        functools.partial(_window_attn_kernel, seq=seq),
        grid=(batch, A_HEADS),
        in_specs=[head_spec(QA_HEAD0, 1), head_spec(KA_HEAD0, grp), head_spec(VA_HEAD0, grp),
                  pl.BlockSpec(memory_space=pltpu.SMEM)] + wo_in,
        out_specs=[pl.BlockSpec((seq, HEAD_DIM), lambda b, h: (b, h)), wo_out],
        out_shape=[jax.ShapeDtypeStruct((m, A_Q), jnp.bfloat16), wo_shape],
        scratch_shapes=[pltpu.VMEM((3, seq, HEAD_DIM), jnp.bfloat16)],
        compiler_params=pltpu.CompilerParams(
            dimension_semantics=("arbitrary", "arbitrary"), vmem_limit_bytes=VMEM_LIMIT),
        name="window_attn",
    )(proj, proj, proj, sink, w_o)


def _mix2(a, b):
    m = jnp.maximum(a[1], b[1])
    fa = jnp.exp2(a[1] - m)
    fb = jnp.exp2(b[1] - m)
    return fa * a[0] + fb * b[0], m, fa * a[2] + fb * b[2]


def _dilated_attn_kernel(q_ref, k_ref, v_ref, w1_ref, wg_ref, w2_ref, o_ref, w1gb_ref, w2b_ref,
                         nat_ref, ca_ref, cab_ref, cbb_ref, vn_ref, va_ref, vb_ref, pc_ref, pn_ref, *, seq):
    _cast_rows([w1_ref, wg_ref], w1gb_ref)
    _cast_rows([w2_ref], w2b_ref)
    bf = jnp.bfloat16
    (w1, r1), (wa, ra), (wb, rb) = sorted(DILATED_PATTERNS, key=lambda pat: pat[1])
    assert r1 == 1 and rb % ra == 0
    sub = rb // ra
    len_a, len_b = seq // ra, seq // rb
    halo1, halo_a, halo_b = w1 // 2, wa // (2 * ra), wb // (2 * rb)
    biases = {}

    def band_scores(qk_src, base, q0, w0, window, halo):
        key = (q0 - w0, Q_BLOCK, window, halo)
        if key not in biases:
            biases[key] = _band_bias(*key)
        return _qk(qk_src[0, base + q0:base + q0 + Q_BLOCK, :],
                   qk_src[1, base + w0:base + w0 + window, :]) + biases[key]

    def band_output(s, v_src, base, w0, window):
        o, m, denom = _softmax_pv(s, v_src[base + w0:base + w0 + window, :])
        return o, jnp.broadcast_to(m, (Q_BLOCK, LANES)), denom

    def blocks(length, halo):
        window = min(Q_BLOCK + 2 * halo, length)
        for j in range(length // Q_BLOCK):
            q0 = j * Q_BLOCK
            yield q0, _window_start(q0, halo, window, length), window

    def load3(ref, rows):
        return tuple(ref[t, rows, :] for t in range(3))

    def store3(ref, rows, triple):
        for t in range(3):
            ref[t, rows, :] = triple[t]

    for t, src in enumerate((q_ref, k_ref, v_ref)):
        if t < 2:
            nat_ref[t] = src[...].astype(bf)
        else:
            _store_v_ones(vn_ref, src[...])
        for c in range(ra):
            x = src[pl.ds(c, len_a, stride=ra), :]
            ca_ref[t, c * len_a:(c + 1) * len_a, :] = x
            if t < 2:
                cab_ref[t, c * len_a:(c + 1) * len_a, :] = x.astype(bf)
            else:
                _store_v_ones(va_ref.at[c * len_a:(c + 1) * len_a, :], x)
    for t in range(3):
        for c in range(ra):
            for u in range(sub):
                rows = slice((c * sub + u) * len_b, (c * sub + u + 1) * len_b)
                x = ca_ref[t, pl.ds(c * len_a + u, len_b, stride=sub), :]
                if t < 2:
                    cbb_ref[t, rows, :] = x.astype(bf)
                else:
                    _store_v_ones(vb_ref.at[rows, :], x)

    units = []

    def add_unit(qk_src, v_src, base, q0, w0, window, halo, finish):
        units.append((lambda: band_scores(qk_src, base, q0, w0, window, halo),
                      lambda s: finish(band_output(s, v_src, base, w0, window))))

    for c in range(ra):
        for u in range(sub):
            cu = c * sub + u
            for q0, w0, window in blocks(len_b, halo_b):
                def finish_b(triple, c=c, u=u, q0=q0):
                    store3(pc_ref, pl.ds(c * len_a + q0 * sub + u, Q_BLOCK, stride=sub), triple)
                add_unit(cbb_ref, vb_ref, cu * len_b, q0, w0, window, halo_b, finish_b)

    for c in range(ra):
        for q0, w0, window in blocks(len_a, halo_a):
            def finish_a(triple, c=c, q0=q0):
                merged = _mix2(triple, load3(pc_ref, slice(c * len_a + q0, c * len_a + q0 + Q_BLOCK)))
                store3(pn_ref, pl.ds(q0 * ra + c, Q_BLOCK, stride=ra), merged)
            add_unit(cab_ref, va_ref, c * len_a, q0, w0, window, halo_a, finish_a)

    for q0, w0, window in blocks(seq, halo1):
        def finish_1(triple, q0=q0):
            o, _, denom = _mix2(triple, load3(pn_ref, slice(q0, q0 + Q_BLOCK)))
            o_ref[q0:q0 + Q_BLOCK, :] = (o / denom).astype(o_ref.dtype)
        add_unit(nat_ref, vn_ref, 0, q0, w0, window, halo1, finish_1)

    _run_pipelined(len(units), lambda j: units[j][0](), lambda j, s: units[j][1](s))


def _dilated_attn(proj, w1, w_gate, w2, *, batch, seq):
    m = batch * seq
    head_spec = lambda h0: pl.BlockSpec((None, seq, HEAD_DIM), lambda b, h: (h0 + h, b, 0))
    step_of = lambda b, h: b * B_HEADS + h
    w_in_specs, w_out_spec, w_out_shape = _cast_rows_specs([w1, w_gate], batch * B_HEADS, step_of)
    w2_in_specs, w2_out_spec, w2_out_shape = _cast_rows_specs([w2], batch * B_HEADS, step_of)
    qk_buf = pltpu.VMEM((2, seq, HEAD_DIM), jnp.bfloat16)
    v_ones_buf = pltpu.VMEM((seq, 2 * HEAD_DIM), jnp.bfloat16)
    return pl.pallas_call(
        functools.partial(_dilated_attn_kernel, seq=seq),
        grid=(batch, B_HEADS),
        in_specs=[head_spec(QB_HEAD0), head_spec(KB_HEAD0), head_spec(VB_HEAD0)] + w_in_specs + w2_in_specs,
        out_specs=[pl.BlockSpec((seq, HEAD_DIM), lambda b, h: (b, h)), w_out_spec, w2_out_spec],
        out_shape=[jax.ShapeDtypeStruct((m, B_QKV), jnp.bfloat16), w_out_shape, w2_out_shape],
        scratch_shapes=[qk_buf,
                        pltpu.VMEM((3, seq, HEAD_DIM), jnp.float32),
                        qk_buf, qk_buf,
                        v_ones_buf, v_ones_buf, v_ones_buf]
                       + [pltpu.VMEM((3, seq, HEAD_DIM), jnp.float32)] * 2,
        compiler_params=pltpu.CompilerParams(
            dimension_semantics=("arbitrary", "arbitrary"), vmem_limit_bytes=VMEM_LIMIT),
        name="dilated_attn",
    )(proj, proj, proj, w1, w_gate, w2)


def _layer_norm(y, g, b):
    mu = jnp.mean(y, axis=-1, keepdims=True)
    d = y - mu
    var = jnp.mean(d * d, axis=-1, keepdims=True)
    return d * lax.rsqrt(var + LN_EPS) * g + b


def _rms_norm(y, g):
    return y * lax.rsqrt(jnp.mean(y * y, axis=-1, keepdims=True) + RMS_EPS) * g


def _mix_ln1_kernel(oa_ref, ob_ref, x_ref, wo_ref, gna_ref, gnb_ref, g_ref, b_ref, h_ref):
    for r in range(MIX_TM // MIX_ROWS):
        rows = slice(r * MIX_ROWS, (r + 1) * MIX_ROWS)
        ya = _rms_norm(oa_ref[rows, :].astype(jnp.float32), gna_ref[...])
        yb = _rms_norm(ob_ref[rows, :].astype(jnp.float32), gnb_ref[...])
        mix = (jnp.dot(ya.astype(jnp.bfloat16), wo_ref[:A_Q, :], preferred_element_type=jnp.float32)
               + jnp.dot(yb.astype(jnp.bfloat16), wo_ref[A_Q:, :], preferred_element_type=jnp.float32))
        h_ref[rows, :] = _layer_norm(ALPHA * x_ref[rows, :] + mix, g_ref[...], b_ref[...])


def _mix_ln1(oa, ob, x2, wo_b, gn_a, gn_b, ln_g, ln_b):
    m = x2.shape[0]
    row = lambda w: pl.BlockSpec((MIX_TM, w), lambda i: (i, 0))
    full = lambda r, w: pl.BlockSpec((r, w), lambda i: (0, 0))
    return pl.pallas_call(
        _mix_ln1_kernel,
        grid=(m // MIX_TM,),
        in_specs=[row(A_Q), row(B_QKV), row(D_MODEL), full(D_MODEL, D_MODEL), full(1, A_Q),
                  full(1, B_QKV), full(1, D_MODEL), full(1, D_MODEL)],
        out_specs=row(D_MODEL),
        out_shape=jax.ShapeDtypeStruct((m, D_MODEL), jnp.float32),
        compiler_params=pltpu.CompilerParams(
            dimension_semantics=("arbitrary",), vmem_limit_bytes=VMEM_LIMIT),
        name="mix_ln1",
    )(oa, ob, x2, wo_b, gn_a, gn_b, ln_g, ln_b)


def _ffn_ln2_kernel(h_ref, w1g_ref, w2_ref, p_ref, wple_ref, g_ref, b_ref, o_ref, hb_ref, gate_ref):
    j = pl.program_id(1)
    n_ff = D_FF // FFN_TF
    n_all = (D_FF + D_MODEL) // FFN_TF
    n_sub = FFN_TF // FFN_SUB

    @pl.when(j == 0)
    def _():
        h = h_ref[...]
        hb_ref[...] = h.astype(jnp.bfloat16)
        o_ref[...] = ALPHA * h

    def up(c):
        return jnp.dot(hb_ref[...], w1g_ref[:, c * FFN_SUB:(c + 1) * FFN_SUB],
                       preferred_element_type=jnp.float32)

    @pl.when(j < n_ff)
    def _():
        acts = []
        for c in range(n_sub):
            r = jnp.maximum(up(c), 0.0)
            acts.append((r * r).astype(jnp.bfloat16))
        for n in range(D_MODEL // FFN_SUB):
            cols = slice(n * FFN_SUB, (n + 1) * FFN_SUB)
            part = sum(jnp.dot(acts[c], w2_ref[c * FFN_SUB:(c + 1) * FFN_SUB, cols],
                               preferred_element_type=jnp.float32) for c in range(n_sub))
            o_ref[:, cols] += part

    def gate_chunk(t, c, rows):
        lo = t * FFN_TF + c * FFN_SUB
        a = jnp.dot(hb_ref[rows, :], w1g_ref[:, c * FFN_SUB:(c + 1) * FFN_SUB], preferred_element_type=jnp.float32)
        gate_ref[rows, lo:lo + FFN_SUB] = 0.5 * jnp.tanh(0.5 * a) + 0.5

    for t in range(n_all - n_ff - 1):
        @pl.when(j == n_ff + t)
        def _():
            for c in range(n_sub):
                gate_chunk(t, c, slice(None))

    @pl.when(j == n_all - 1)
    def _():
        for r in range(FFN_TM // FFN_LAST_ROWS):
            rows = slice(r * FFN_LAST_ROWS, (r + 1) * FFN_LAST_ROWS)
            for c in range(n_sub):
                gate_chunk(n_all - n_ff - 1, c, rows)
            ple = jnp.dot(p_ref[rows, :].astype(jnp.bfloat16), wple_ref[...], preferred_element_type=jnp.float32)
            o_ref[rows, :] = _layer_norm(o_ref[rows, :] + ple * gate_ref[rows, :], g_ref[...], b_ref[...])


def _ffn_ln2(h1, w1g_b, w2_b, p_flat, layer, wple_b, ln_g, ln_b):
    m = h1.shape[0]
    p_tile0 = layer * (m // FFN_TM)
    n_ff = D_FF // FFN_TF
    n_all = (D_FF + D_MODEL) // FFN_TF
    return pl.pallas_call(
        _ffn_ln2_kernel,
        grid=(m // FFN_TM, n_all),
        in_specs=[pl.BlockSpec((FFN_TM, D_MODEL), lambda i, j: (i, 0)),
                  pl.BlockSpec((D_MODEL, FFN_TF), lambda i, j: (0, j)),
                  pl.BlockSpec((FFN_TF, D_MODEL), lambda i, j: (jnp.minimum(j, n_ff - 1), 0)),
                  pl.BlockSpec((FFN_TM, PLE_DIM), lambda i, j: (p_tile0 + i, 0)),
                  pl.BlockSpec((PLE_DIM, D_MODEL), lambda i, j: (0, 0)),
                  pl.BlockSpec((1, D_MODEL), lambda i, j: (0, 0)),
                  pl.BlockSpec((1, D_MODEL), lambda i, j: (0, 0))],
        out_specs=pl.BlockSpec((FFN_TM, D_MODEL), lambda i, j: (i, 0)),
        out_shape=jax.ShapeDtypeStruct((m, D_MODEL), jnp.float32),
        scratch_shapes=[pltpu.VMEM((FFN_TM, D_MODEL), jnp.bfloat16),
                        pltpu.VMEM((FFN_TM, D_MODEL), jnp.float32)],
        compiler_params=pltpu.CompilerParams(
            dimension_semantics=("arbitrary", "arbitrary"), vmem_limit_bytes=VMEM_LIMIT),
        name="ffn_ln2",
    )(h1, w1g_b, w2_b, p_flat, wple_b, ln_g, ln_b)


def kernel(x, p, positions, w_in, sink_a, gn_a, gn_b, w_o, ln1_g, ln1_b, w1, w2, w_ple, w_ple_gate,
           ln2_g, ln2_b):
    batch, seq, d = x.shape
    m = batch * seq
    bf = jnp.bfloat16
    cos_t, sin_t, w_in_b = _rope_tables(positions, w_in)
    p_flat = p.reshape(-1, PLE_DIM)
    h = x.reshape(m, d)
    for i in range(DEPTH):
        proj = _in_proj(h, w_in_b, i, cos_t, sin_t)
        oa, wo_b = _window_attn(proj, sink_a[i], w_o[i], batch=batch, seq=seq)
        ob, w1g_b, w2_b = _dilated_attn(proj, w1[i], w_ple_gate[i], w2[i], batch=batch, seq=seq)
        h1 = _mix_ln1(oa, ob, h, wo_b, gn_a[i].reshape(1, -1), gn_b[i].reshape(1, -1),
                      ln1_g[i].reshape(1, -1), ln1_b[i].reshape(1, -1))
        h = _ffn_ln2(h1, w1g_b, w2_b, p_flat, i, w_ple[i].astype(bf),
                     ln2_g[i].reshape(1, -1), ln2_b[i].reshape(1, -1))
    return h.reshape(batch, seq, d)
```

```python
import functools

import jax
import jax.numpy as jnp
from jax import lax
from jax.experimental import pallas as pl
from jax.experimental.pallas import tpu as pltpu

D_MODEL = 2048
HEAD_DIM = 128
A_HEADS = 8
A_KV_HEADS = 2
A_WINDOW = 128
B_HEADS = 8
DILATED_PATTERNS = ((128, 1), (512, 4), (2048, 16))
ROT_DIM = HEAD_DIM // 4
ROPE_THETA = 500000.0
D_FF = 4 * D_MODEL
PLE_DIM = 256
DEPTH = 1
ALPHA = (2.0 * DEPTH) ** 0.25
LN_EPS = 1e-5
RMS_EPS = 1e-6
NEG_INF = -1e30
LOG2_E = 1.4426950408889634

A_Q = A_HEADS * HEAD_DIM
A_KV = A_KV_HEADS * HEAD_DIM
B_QKV = B_HEADS * HEAD_DIM
D_IN = A_Q + 2 * A_KV + 3 * B_QKV
N_PROJ_HEADS = D_IN // HEAD_DIM
QA_HEAD0 = 0
KA_HEAD0 = A_HEADS
VA_HEAD0 = A_HEADS + A_KV_HEADS
QB_HEAD0 = A_HEADS + 2 * A_KV_HEADS
KB_HEAD0 = QB_HEAD0 + B_HEADS
VB_HEAD0 = KB_HEAD0 + B_HEADS

LANES = 128
MXU_COLS = 256
BF16_TILE_ROWS = 16
Q_BLOCK = 128
SCORE_LOOKAHEAD = 5
VMEM_LIMIT = 56 * 1024 * 1024

PROJ_TM = 1024
PROJ_TN = 1536
MIX_TM = 512
MIX_ROWS = 256
FFN_TM = 512
FFN_TF = 1024
FFN_SUB = 512
FFN_LAST_ROWS = 256
ROPE_TM = 2048
ROPE_PACK = 8
KIND_Q, KIND_K, KIND_V = 0, 1, 2


def _rope_table_kernel(pos_ref, freq_ref, w_ref, cos_ref, sin_ref, wb_ref):
    _cast_rows([w_ref], wb_ref)
    half = ROT_DIM // 2
    ang = pos_ref[...].astype(jnp.float32) * freq_ref[...]
    rows = ang.shape[0] * ROPE_PACK
    sub = lax.broadcasted_iota(jnp.int32, (rows, LANES), 0) % ROPE_PACK
    lane = lax.broadcasted_iota(jnp.int32, (rows, LANES), 1)
    src_lane = sub * half + lane % half

    def spread(compact):
        per_token = jnp.broadcast_to(compact[:, None, :], (compact.shape[0], ROPE_PACK, LANES)).reshape(rows, LANES)
        return jnp.take_along_axis(per_token, src_lane, axis=1)

    c = jnp.where(lane < ROT_DIM, spread(jnp.cos(ang)), 1.0)
    s = spread(jnp.sin(ang))
    s = jnp.where(lane < half, -s, jnp.where(lane < ROT_DIM, s, 0.0))
    q_scale = LOG2_E * HEAD_DIM ** -0.5
    cos_ref[KIND_Q] = c * q_scale
    sin_ref[KIND_Q] = s * q_scale
    cos_ref[KIND_K] = c
    sin_ref[KIND_K] = s
    cos_ref[KIND_V] = jnp.ones_like(c)
    sin_ref[KIND_V] = jnp.zeros_like(s)


def _rope_tables(positions, w_in):
    m = positions.size
    half = ROT_DIM // 2
    assert ROPE_PACK * half == LANES
    inv_freq = ROPE_THETA ** (-jnp.arange(0, ROT_DIM, 2, dtype=jnp.float32) / ROT_DIM)
    freq = jnp.tile(inv_freq, ROPE_PACK).reshape(1, LANES)
    pos = jnp.repeat(positions.reshape(m // ROPE_PACK, ROPE_PACK), half, axis=1)
    out = jax.ShapeDtypeStruct((3, m, LANES), jnp.float32)
    w_flat = w_in.reshape(-1, w_in.shape[-1])
    w_specs, wb_spec, wb_shape = _cast_rows_specs([w_flat], m // ROPE_TM, lambda i: i)
    cos_t, sin_t, w_b = pl.pallas_call(
        _rope_table_kernel,
        grid=(m // ROPE_TM,),
        in_specs=[pl.BlockSpec((ROPE_TM // ROPE_PACK, LANES), lambda i: (i, 0)),
                  pl.BlockSpec((1, LANES), lambda i: (0, 0))] + w_specs,
        out_specs=[pl.BlockSpec((3, ROPE_TM, LANES), lambda i: (0, i, 0))] * 2 + [wb_spec],
        out_shape=[out, out, wb_shape],
        name="rope_table",
    )(pos, freq, w_flat)
    return cos_t, sin_t, w_b.reshape(w_in.shape)


def _in_proj_kernel(x_ref, w_ref, cos_ref, sin_ref, o_ref, xb_ref):
    j = pl.program_id(1)

    @pl.when(j == 0)
    def _():
        xb_ref[...] = x_ref[...].astype(jnp.bfloat16)

    half = ROT_DIM // 2
    heads_per_tile = PROJ_TN // HEAD_DIM
    heads_per_chunk = MXU_COLS // HEAD_DIM
    n_chunks = PROJ_TN // MXU_COLS

    def chunk(c, rows, n_rows):
        lane = lax.broadcasted_iota(jnp.int32, (n_rows, LANES), 1)
        partner = jnp.where(lane < half, lane + half, jnp.where(lane < ROT_DIM, lane - half, lane))
        acc = jnp.dot(xb_ref[rows, :], w_ref[:, c * MXU_COLS:(c + 1) * MXU_COLS],
                      preferred_element_type=jnp.float32)
        for g in range(heads_per_chunk):
            local = c * heads_per_chunk + g
            head = j * heads_per_tile + local
            is_q = (head < KA_HEAD0) | ((head >= QB_HEAD0) & (head < KB_HEAD0))
            is_k = ((head >= KA_HEAD0) & (head < VA_HEAD0)) | ((head >= KB_HEAD0) & (head < VB_HEAD0))
            kind = jnp.where(is_q, KIND_Q, jnp.where(is_k, KIND_K, KIND_V))
            t = acc[:, g * HEAD_DIM:(g + 1) * HEAD_DIM]
            swapped = jnp.take_along_axis(t, partner, axis=1)
            o_ref[local, rows, :] = t * cos_ref[kind, rows, :] + swapped * sin_ref[kind, rows, :]

    for c in range(n_chunks - 2):
        chunk(c, slice(None), PROJ_TM)
    for r in range(2):
        for c in range(n_chunks - 2, n_chunks):
            chunk(c, slice(r * (PROJ_TM // 2), (r + 1) * (PROJ_TM // 2)), PROJ_TM // 2)


def _in_proj(x2, w_in_b, layer, cos_t, sin_t):
    m = x2.shape[0]
    heads_per_tile = PROJ_TN // HEAD_DIM
    return pl.pallas_call(
        _in_proj_kernel,
        grid=(m // PROJ_TM, D_IN // PROJ_TN),
        in_specs=[pl.BlockSpec((PROJ_TM, D_MODEL), lambda i, j: (i, 0)),
                  pl.BlockSpec((None, D_MODEL, PROJ_TN), lambda i, j: (layer, 0, j)),
                  pl.BlockSpec((3, PROJ_TM, LANES), lambda i, j: (0, i, 0)),
                  pl.BlockSpec((3, PROJ_TM, LANES), lambda i, j: (0, i, 0))],
        out_specs=pl.BlockSpec((heads_per_tile, PROJ_TM, HEAD_DIM), lambda i, j: (j, i, 0)),
        out_shape=jax.ShapeDtypeStruct((N_PROJ_HEADS, m, HEAD_DIM), jnp.float32),
        scratch_shapes=[pltpu.VMEM((PROJ_TM, D_MODEL), jnp.bfloat16)],
        compiler_params=pltpu.CompilerParams(
            dimension_semantics=("arbitrary", "arbitrary"), vmem_limit_bytes=VMEM_LIMIT),
        name="in_proj",
    )(x2, w_in_b, cos_t, sin_t)


def _band_bias(delta, rows, cols, halo):
    d = delta + lax.broadcasted_iota(jnp.int32, (rows, cols), 0) - lax.broadcasted_iota(jnp.int32, (rows, cols), 1)
    return jnp.where(jnp.abs(d) <= halo, 0.0, NEG_INF).astype(jnp.float32)


def _softmax_pv(s, v, sink=None):
    m = jnp.max(s, axis=-1, keepdims=True)
    if sink is not None:
        m = jnp.maximum(m, sink)
    e = jnp.exp2(s - m)
    acc = jnp.dot(e.astype(v.dtype), v, preferred_element_type=jnp.float32)
    if v.shape[1] == HEAD_DIM:
        o, denom = acc, jnp.sum(e, axis=-1, keepdims=True)
    else:
        o, denom = acc[:, :HEAD_DIM], acc[:, HEAD_DIM:]
    if sink is not None:
        denom = denom + jnp.exp2(sink - m)
    return o, m, denom


def _store_v_ones(vo_ref, v):
    vo_ref[:, :HEAD_DIM] = v.astype(vo_ref.dtype)
    vo_ref[:, HEAD_DIM:] = jnp.ones(v.shape, vo_ref.dtype)


def _qk(q, k):
    return lax.dot_general(q, k, (((1,), (1,)), ((), ())), preferred_element_type=jnp.float32)


def _window_start(q0, halo, window, length):
    return min(max(q0 - halo, 0), length - window)


def _run_pipelined(n, scores, finish, depth=SCORE_LOOKAHEAD):
    pending = {}
    for j in range(n + depth):
        if j < n:
            pending[j] = scores(j)
        if j >= depth:
            finish(j - depth, pending.pop(j - depth))


def _cast_rows_specs(weights, grid_steps, step_of):
    rows = weights[0].shape[0] // grid_steps
    assert all(w.shape[0] == weights[0].shape[0] for w in weights) and rows % BF16_TILE_ROWS == 0
    cols = sum(w.shape[1] for w in weights)
    in_specs = [pl.BlockSpec((rows, w.shape[1]), lambda *g: (step_of(*g), 0)) for w in weights]
    out_spec = pl.BlockSpec((rows, cols), lambda *g: (step_of(*g), 0))
    out_shape = jax.ShapeDtypeStruct((weights[0].shape[0], cols), jnp.bfloat16)
    return in_specs, out_spec, out_shape


def _cast_rows(w_refs, out_ref):
    lo = 0
    for w_ref in w_refs:
        out_ref[:, lo:lo + w_ref.shape[1]] = w_ref[...].astype(out_ref.dtype)
        lo += w_ref.shape[1]


def _window_attn_kernel(q_ref, k_ref, v_ref, sink_ref, wo_ref, o_ref, wob_ref, qkv_ref, *, seq):
    _cast_rows([wo_ref], wob_ref)
    window = Q_BLOCK + 2 * A_WINDOW
    head = pl.program_id(1)
    sink = sink_ref[head] * LOG2_E
    qkv_ref[0] = q_ref[...].astype(jnp.bfloat16)

    @pl.when(head % (A_HEADS // A_KV_HEADS) == 0)
    def _():
        qkv_ref[1] = k_ref[...].astype(jnp.bfloat16)
        qkv_ref[2] = v_ref[...].astype(jnp.bfloat16)

    biases = {}

    def scores(j):
        q0 = j * Q_BLOCK
        w0 = _window_start(q0, A_WINDOW, window, seq)
        if q0 - w0 not in biases:
            biases[q0 - w0] = _band_bias(q0 - w0, Q_BLOCK, window, A_WINDOW)
        return _qk(qkv_ref[0, q0:q0 + Q_BLOCK, :], qkv_ref[1, w0:w0 + window, :]) + biases[q0 - w0]

    def finish(j, s):
        q0 = j * Q_BLOCK
        w0 = _window_start(q0, A_WINDOW, window, seq)
        o, _, denom = _softmax_pv(s, qkv_ref[2, w0:w0 + window, :], sink)
        o_ref[q0:q0 + Q_BLOCK, :] = (o / denom).astype(o_ref.dtype)

    _run_pipelined(seq // Q_BLOCK, scores, finish)


def _window_attn(proj, sink, w_o, *, batch, seq):
    m = batch * seq
    grp = A_HEADS // A_KV_HEADS
    head_spec = lambda h0, div: pl.BlockSpec((None, seq, HEAD_DIM), lambda b, h: (h0 + h // div, b, 0))
    step_of = lambda b, h: b * A_HEADS + h
    wo_in, wo_out, wo_shape = _cast_rows_specs([w_o], batch * A_HEADS, step_of)
    return pl.pallas_call(
        functools.partial(_window_attn_kernel, seq=seq),
        grid=(batch, A_HEADS),
        in_specs=[head_spec(QA_HEAD0, 1), head_spec(KA_HEAD0, grp), head_spec(VA_HEAD0, grp),
                  pl.BlockSpec(memory_space=pltpu.SMEM)] + wo_in,
        out_specs=[pl.BlockSpec((seq, HEAD_DIM), lambda b, h: (b, h)), wo_out],
        out_shape=[jax.ShapeDtypeStruct((m, A_Q), jnp.bfloat16), wo_shape],
        scratch_shapes=[pltpu.VMEM((3, seq, HEAD_DIM), jnp.bfloat16)],
        compiler_params=pltpu.CompilerParams(
            dimension_semantics=("arbitrary", "arbitrary"), vmem_limit_bytes=VMEM_LIMIT),
        name="window_attn",
    )(proj, proj, proj, sink, w_o)


def _mix2(a, b):
    m = jnp.maximum(a[1], b[1])
    fa = jnp.exp2(a[1] - m)
    fb = jnp.exp2(b[1] - m)
    return fa * a[0] + fb * b[0], m, fa * a[2] + fb * b[2]


def _dilated_attn_kernel(q_ref, k_ref, v_ref, w1_ref, wg_ref, w2_ref, o_ref, w1gb_ref, w2b_ref,
                         nat_ref, ca_ref, cab_ref, cbb_ref, vn_ref, va_ref, vb_ref, pc_ref, pn_ref, *, seq):
    _cast_rows([w1_ref, wg_ref], w1gb_ref)
    _cast_rows([w2_ref], w2b_ref)
    bf = jnp.bfloat16
    (w1, r1), (wa, ra), (wb, rb) = sorted(DILATED_PATTERNS, key=lambda pat: pat[1])
    assert r1 == 1 and rb % ra == 0
    sub = rb // ra
    len_a, len_b = seq // ra, seq // rb
    halo1, halo_a, halo_b = w1 // 2, wa // (2 * ra), wb // (2 * rb)
    biases = {}

    def band_scores(qk_src, base, q0, w0, window, halo):
        key = (q0 - w0, Q_BLOCK, window, halo)
        if key not in biases:
            biases[key] = _band_bias(*key)
        return _qk(qk_src[0, base + q0:base + q0 + Q_BLOCK, :],
                   qk_src[1, base + w0:base + w0 + window, :]) + biases[key]

    def band_output(s, v_src, base, w0, window):
        o, m, denom = _softmax_pv(s, v_src[base + w0:base + w0 + window, :])
        return o, jnp.broadcast_to(m, (Q_BLOCK, LANES)), denom

    def blocks(length, halo):
        window = min(Q_BLOCK + 2 * halo, length)
        for j in range(length // Q_BLOCK):
            q0 = j * Q_BLOCK
            yield q0, _window_start(q0, halo, window, length), window

    def load3(ref, rows):
        return tuple(ref[t, rows, :] for t in range(3))

    def store3(ref, rows, triple):
        for t in range(3):
            ref[t, rows, :] = triple[t]

    for t, src in enumerate((q_ref, k_ref, v_ref)):
        if t < 2:
            nat_ref[t] = src[...].astype(bf)
        else:
            _store_v_ones(vn_ref, src[...])
        for c in range(ra):
            x = src[pl.ds(c, len_a, stride=ra), :]
            ca_ref[t, c * len_a:(c + 1) * len_a, :] = x
            if t < 2:
                cab_ref[t, c * len_a:(c + 1) * len_a, :] = x.astype(bf)
            else:
                _store_v_ones(va_ref.at[c * len_a:(c + 1) * len_a, :], x)
    for t in range(3):
        for c in range(ra):
            for u in range(sub):
                rows = slice((c * sub + u) * len_b, (c * sub + u + 1) * len_b)
                x = ca_ref[t, pl.ds(c * len_a + u, len_b, stride=sub), :]
                if t < 2:
                    cbb_ref[t, rows, :] = x.astype(bf)
                else:
                    _store_v_ones(vb_ref.at[rows, :], x)

    units = []

    def add_unit(qk_src, v_src, base, q0, w0, window, halo, finish):
        units.append((lambda: band_scores(qk_src, base, q0, w0, window, halo),
                      lambda s: finish(band_output(s, v_src, base, w0, window))))

    for c in range(ra):
        for u in range(sub):
            cu = c * sub + u
            for q0, w0, window in blocks(len_b, halo_b):
                def finish_b(triple, c=c, u=u, q0=q0):
                    store3(pc_ref, pl.ds(c * len_a + q0 * sub + u, Q_BLOCK, stride=sub), triple)
                add_unit(cbb_ref, vb_ref, cu * len_b, q0, w0, window, halo_b, finish_b)

    for c in range(ra):
        for q0, w0, window in blocks(len_a, halo_a):
            def finish_a(triple, c=c, q0=q0):
                merged = _mix2(triple, load3(pc_ref, slice(c * len_a + q0, c * len_a + q0 + Q_BLOCK)))
                store3(pn_ref, pl.ds(q0 * ra + c, Q_BLOCK, stride=ra), merged)
            add_unit(cab_ref, va_ref, c * len_a, q0, w0, window, halo_a, finish_a)

    for q0, w0, window in blocks(seq, halo1):
        def finish_1(triple, q0=q0):
            o, _, denom = _mix2(triple, load3(pn_ref, slice(q0, q0 + Q_BLOCK)))
            o_ref[q0:q0 + Q_BLOCK, :] = (o / denom).astype(o_ref.dtype)
        add_unit(nat_ref, vn_ref, 0, q0, w0, window, halo1, finish_1)

    _run_pipelined(len(units), lambda j: units[j][0](), lambda j, s: units[j][1](s))


def _dilated_attn(proj, w1, w_gate, w2, *, batch, seq):
    m = batch * seq
    head_spec = lambda h0: pl.BlockSpec((None, seq, HEAD_DIM), lambda b, h: (h0 + h, b, 0))
    step_of = lambda b, h: b * B_HEADS + h
    w_in_specs, w_out_spec, w_out_shape = _cast_rows_specs([w1, w_gate], batch * B_HEADS, step_of)
    w2_in_specs, w2_out_spec, w2_out_shape = _cast_rows_specs([w2], batch * B_HEADS, step_of)
    qk_buf = pltpu.VMEM((2, seq, HEAD_DIM), jnp.bfloat16)
    v_ones_buf = pltpu.VMEM((seq, 2 * HEAD_DIM), jnp.bfloat16)
    return pl.pallas_call(
        functools.partial(_dilated_attn_kernel, seq=seq),
        grid=(batch, B_HEADS),
        in_specs=[head_spec(QB_HEAD0), head_spec(KB_HEAD0), head_spec(VB_HEAD0)] + w_in_specs + w2_in_specs,
        out_specs=[pl.BlockSpec((seq, HEAD_DIM), lambda b, h: (b, h)), w_out_spec, w2_out_spec],
        out_shape=[jax.ShapeDtypeStruct((m, B_QKV), jnp.bfloat16), w_out_shape, w2_out_shape],
        scratch_shapes=[qk_buf,
                        pltpu.VMEM((3, seq, HEAD_DIM), jnp.float32),
                        qk_buf, qk_buf,
                        v_ones_buf, v_ones_buf, v_ones_buf]
                       + [pltpu.VMEM((3, seq, HEAD_DIM), jnp.float32)] * 2,
        compiler_params=pltpu.CompilerParams(
            dimension_semantics=("arbitrary", "arbitrary"), vmem_limit_bytes=VMEM_LIMIT),
        name="dilated_attn",
    )(proj, proj, proj, w1, w_gate, w2)


def _layer_norm(y, g, b):
    mu = jnp.mean(y, axis=-1, keepdims=True)
    d = y - mu
    var = jnp.mean(d * d, axis=-1, keepdims=True)
    return d * lax.rsqrt(var + LN_EPS) * g + b


def _rms_norm(y, g):
    return y * lax.rsqrt(jnp.mean(y * y, axis=-1, keepdims=True) + RMS_EPS) * g


def _mix_ln1_kernel(oa_ref, ob_ref, x_ref, wo_ref, gna_ref, gnb_ref, g_ref, b_ref, h_ref):
    for r in range(MIX_TM // MIX_ROWS):
        rows = slice(r * MIX_ROWS, (r + 1) * MIX_ROWS)
        ya = _rms_norm(oa_ref[rows, :].astype(jnp.float32), gna_ref[...])
        yb = _rms_norm(ob_ref[rows, :].astype(jnp.float32), gnb_ref[...])
        mix = (jnp.dot(ya.astype(jnp.bfloat16), wo_ref[:A_Q, :], preferred_element_type=jnp.float32)
               + jnp.dot(yb.astype(jnp.bfloat16), wo_ref[A_Q:, :], preferred_element_type=jnp.float32))
        h_ref[rows, :] = _layer_norm(ALPHA * x_ref[rows, :] + mix, g_ref[...], b_ref[...])


def _mix_ln1(oa, ob, x2, wo_b, gn_a, gn_b, ln_g, ln_b):
    m = x2.shape[0]
    row = lambda w: pl.BlockSpec((MIX_TM, w), lambda i: (i, 0))
    full = lambda r, w: pl.BlockSpec((r, w), lambda i: (0, 0))
    return pl.pallas_call(
        _mix_ln1_kernel,
        grid=(m // MIX_TM,),
        in_specs=[row(A_Q), row(B_QKV), row(D_MODEL), full(D_MODEL, D_MODEL), full(1, A_Q),
                  full(1, B_QKV), full(1, D_MODEL), full(1, D_MODEL)],
        out_specs=row(D_MODEL),
        out_shape=jax.ShapeDtypeStruct((m, D_MODEL), jnp.float32),
        compiler_params=pltpu.CompilerParams(
            dimension_semantics=("arbitrary",), vmem_limit_bytes=VMEM_LIMIT),
        name="mix_ln1",
    )(oa, ob, x2, wo_b, gn_a, gn_b, ln_g, ln_b)


def _ffn_ln2_kernel(h_ref, w1g_ref, w2_ref, p_ref, wple_ref, g_ref, b_ref, o_ref, hb_ref, gate_ref):
    j = pl.program_id(1)
    n_ff = D_FF // FFN_TF
    n_all = (D_FF + D_MODEL) // FFN_TF
    n_sub = FFN_TF // FFN_SUB

    @pl.when(j == 0)
    def _():
        h = h_ref[...]
        hb_ref[...] = h.astype(jnp.bfloat16)
        o_ref[...] = ALPHA * h

    def up(c):
        return jnp.dot(hb_ref[...], w1g_ref[:, c * FFN_SUB:(c + 1) * FFN_SUB],
                       preferred_element_type=jnp.float32)

    @pl.when(j < n_ff)
    def _():
        acts = []
        for c in range(n_sub):
            r = jnp.maximum(up(c), 0.0)
            acts.append((r * r).astype(jnp.bfloat16))
        for n in range(D_MODEL // FFN_SUB):
            cols = slice(n * FFN_SUB, (n + 1) * FFN_SUB)
            part = sum(jnp.dot(acts[c], w2_ref[c * FFN_SUB:(c + 1) * FFN_SUB, cols],
                               preferred_element_type=jnp.float32) for c in range(n_sub))
            o_ref[:, cols] += part

    def gate_chunk(t, c, rows):
        lo = t * FFN_TF + c * FFN_SUB
        a = jnp.dot(hb_ref[rows, :], w1g_ref[:, c * FFN_SUB:(c + 1) * FFN_SUB], preferred_element_type=jnp.float32)
        gate_ref[rows, lo:lo + FFN_SUB] = 0.5 * jnp.tanh(0.5 * a) + 0.5

    for t in range(n_all - n_ff - 1):
        @pl.when(j == n_ff + t)
        def _():
            for c in range(n_sub):
                gate_chunk(t, c, slice(None))

    @pl.when(j == n_all - 1)
    def _():
        for r in range(FFN_TM // FFN_LAST_ROWS):
            rows = slice(r * FFN_LAST_ROWS, (r + 1) * FFN_LAST_ROWS)
            for c in range(n_sub):
                gate_chunk(n_all - n_ff - 1, c, rows)
            ple = jnp.dot(p_ref[rows, :].astype(jnp.bfloat16), wple_ref[...], preferred_element_type=jnp.float32)
            o_ref[rows, :] = _layer_norm(o_ref[rows, :] + ple * gate_ref[rows, :], g_ref[...], b_ref[...])


def _ffn_ln2(h1, w1g_b, w2_b, p_flat, layer, wple_b, ln_g, ln_b):
    m = h1.shape[0]
    p_tile0 = layer * (m // FFN_TM)
    n_ff = D_FF // FFN_TF
    n_all = (D_FF + D_MODEL) // FFN_TF
    return pl.pallas_call(
        _ffn_ln2_kernel,
        grid=(m // FFN_TM, n_all),
        in_specs=[pl.BlockSpec((FFN_TM, D_MODEL), lambda i, j: (i, 0)),
                  pl.BlockSpec((D_MODEL, FFN_TF), lambda i, j: (0, j)),
                  pl.BlockSpec((FFN_TF, D_MODEL), lambda i, j: (jnp.minimum(j, n_ff - 1), 0)),
                  pl.BlockSpec((FFN_TM, PLE_DIM), lambda i, j: (p_tile0 + i, 0)),
                  pl.BlockSpec((PLE_DIM, D_MODEL), lambda i, j: (0, 0)),
                  pl.BlockSpec((1, D_MODEL), lambda i, j: (0, 0)),
                  pl.BlockSpec((1, D_MODEL), lambda i, j: (0, 0))],
        out_specs=pl.BlockSpec((FFN_TM, D_MODEL), lambda i, j: (i, 0)),
        out_shape=jax.ShapeDtypeStruct((m, D_MODEL), jnp.float32),
        scratch_shapes=[pltpu.VMEM((FFN_TM, D_MODEL), jnp.bfloat16),
                        pltpu.VMEM((FFN_TM, D_MODEL), jnp.float32)],
        compiler_params=pltpu.CompilerParams(
            dimension_semantics=("arbitrary", "arbitrary"), vmem_limit_bytes=VMEM_LIMIT),
        name="ffn_ln2",
    )(h1, w1g_b, w2_b, p_flat, wple_b, ln_g, ln_b)


def kernel(x, p, positions, w_in, sink_a, gn_a, gn_b, w_o, ln1_g, ln1_b, w1, w2, w_ple, w_ple_gate,
           ln2_g, ln2_b):
    batch, seq, d = x.shape
    m = batch * seq
    bf = jnp.bfloat16
    cos_t, sin_t, w_in_b = _rope_tables(positions, w_in)
    p_flat = p.reshape(-1, PLE_DIM)
    h = x.reshape(m, d)
    for i in range(DEPTH):
        proj = _in_proj(h, w_in_b, i, cos_t, sin_t)
        oa, wo_b = _window_attn(proj, sink_a[i], w_o[i], batch=batch, seq=seq)
        ob, w1g_b, w2_b = _dilated_attn(proj, w1[i], w_ple_gate[i], w2[i], batch=batch, seq=seq)
        h1 = _mix_ln1(oa, ob, h, wo_b, gn_a[i].reshape(1, -1), gn_b[i].reshape(1, -1),
                      ln1_g[i].reshape(1, -1), ln1_b[i].reshape(1, -1))
        h = _ffn_ln2(h1, w1g_b, w2_b, p_flat, i, w_ple[i].astype(bf),
                     ln2_g[i].reshape(1, -1), ln2_b[i].reshape(1, -1))
    return h.reshape(batch, seq, d)
```

```python
import functools

import jax
import jax.numpy as jnp
from jax import lax
from jax.experimental import pallas as pl
from jax.experimental.pallas import tpu as pltpu

D_MODEL = 2048
HEAD_DIM = 128
A_HEADS = 8
A_KV_HEADS = 2
A_WINDOW = 128
B_HEADS = 8
DILATED_PATTERNS = ((128, 1), (512, 4), (2048, 16))
ROT_DIM = HEAD_DIM // 4
ROPE_THETA = 500000.0
D_FF = 4 * D_MODEL
PLE_DIM = 256
DEPTH = 1
ALPHA = (2.0 * DEPTH) ** 0.25
LN_EPS = 1e-5
RMS_EPS = 1e-6
NEG_INF = -1e30
LOG2_E = 1.4426950408889634

A_Q = A_HEADS * HEAD_DIM
A_KV = A_KV_HEADS * HEAD_DIM
B_QKV = B_HEADS * HEAD_DIM
D_IN = A_Q + 2 * A_KV + 3 * B_QKV
N_PROJ_HEADS = D_IN // HEAD_DIM
QA_HEAD0 = 0
KA_HEAD0 = A_HEADS
VA_HEAD0 = A_HEADS + A_KV_HEADS
QB_HEAD0 = A_HEADS + 2 * A_KV_HEADS
KB_HEAD0 = QB_HEAD0 + B_HEADS
VB_HEAD0 = KB_HEAD0 + B_HEADS

LANES = 128
MXU_COLS = 256
BF16_TILE_ROWS = 16
Q_BLOCK = 128
SCORE_LOOKAHEAD = 5
VMEM_LIMIT = 56 * 1024 * 1024

PROJ_TM = 1024
PROJ_TN = 1536
MIX_TM = 512
MIX_ROWS = 256
FFN_TM = 512
FFN_TF = 1024
FFN_SUB = 512
FFN_LAST_ROWS = 256
ROPE_TM = 2048
ROPE_PACK = 8
KIND_Q, KIND_K, KIND_V = 0, 1, 2


def _rope_table_kernel(pos_ref, freq_ref, w_ref, cos_ref, sin_ref, wb_ref):
    _cast_rows([w_ref], wb_ref)
    half = ROT_DIM // 2
    ang = pos_ref[...].astype(jnp.float32) * freq_ref[...]
    rows = ang.shape[0] * ROPE_PACK
    sub = lax.broadcasted_iota(jnp.int32, (rows, LANES), 0) % ROPE_PACK
    lane = lax.broadcasted_iota(jnp.int32, (rows, LANES), 1)
    src_lane = sub * half + lane % half

    def spread(compact):
        per_token = jnp.broadcast_to(compact[:, None, :], (compact.shape[0], ROPE_PACK, LANES)).reshape(rows, LANES)
        return jnp.take_along_axis(per_token, src_lane, axis=1)

    c = jnp.where(lane < ROT_DIM, spread(jnp.cos(ang)), 1.0)
    s = spread(jnp.sin(ang))
    s = jnp.where(lane < half, -s, jnp.where(lane < ROT_DIM, s, 0.0))
    q_scale = LOG2_E * HEAD_DIM ** -0.5
    cos_ref[KIND_Q] = c * q_scale
    sin_ref[KIND_Q] = s * q_scale
    cos_ref[KIND_K] = c
    sin_ref[KIND_K] = s
    cos_ref[KIND_V] = jnp.ones_like(c)
    sin_ref[KIND_V] = jnp.zeros_like(s)


def _rope_tables(positions, w_in):
    m = positions.size
    half = ROT_DIM // 2
    assert ROPE_PACK * half == LANES
    inv_freq = ROPE_THETA ** (-jnp.arange(0, ROT_DIM, 2, dtype=jnp.float32) / ROT_DIM)
    freq = jnp.tile(inv_freq, ROPE_PACK).reshape(1, LANES)
    pos = jnp.repeat(positions.reshape(m // ROPE_PACK, ROPE_PACK), half, axis=1)
    out = jax.ShapeDtypeStruct((3, m, LANES), jnp.float32)
    w_flat = w_in.reshape(-1, w_in.shape[-1])
    w_specs, wb_spec, wb_shape = _cast_rows_specs([w_flat], m // ROPE_TM, lambda i: i)
    cos_t, sin_t, w_b = pl.pallas_call(
        _rope_table_kernel,
        grid=(m // ROPE_TM,),
        in_specs=[pl.BlockSpec((ROPE_TM // ROPE_PACK, LANES), lambda i: (i, 0)),
                  pl.BlockSpec((1, LANES), lambda i: (0, 0))] + w_specs,
        out_specs=[pl.BlockSpec((3, ROPE_TM, LANES), lambda i: (0, i, 0))] * 2 + [wb_spec],
        out_shape=[out, out, wb_shape],
        name="rope_table",
    )(pos, freq, w_flat)
    return cos_t, sin_t, w_b.reshape(w_in.shape)


def _in_proj_kernel(x_ref, w_ref, cos_ref, sin_ref, o_ref, xb_ref):
    j = pl.program_id(1)

    @pl.when(j == 0)
    def _():
        xb_ref[...] = x_ref[...].astype(jnp.bfloat16)

    half = ROT_DIM // 2
    heads_per_tile = PROJ_TN // HEAD_DIM
    heads_per_chunk = MXU_COLS // HEAD_DIM
    n_chunks = PROJ_TN // MXU_COLS

    def chunk(c, rows, n_rows):
        lane = lax.broadcasted_iota(jnp.int32, (n_rows, LANES), 1)
        partner = jnp.where(lane < half, lane + half, jnp.where(lane < ROT_DIM, lane - half, lane))
        acc = jnp.dot(xb_ref[rows, :], w_ref[:, c * MXU_COLS:(c + 1) * MXU_COLS],
                      preferred_element_type=jnp.float32)
        for g in range(heads_per_chunk):
            local = c * heads_per_chunk + g
            head = j * heads_per_tile + local
            is_q = (head < KA_HEAD0) | ((head >= QB_HEAD0) & (head < KB_HEAD0))
            is_k = ((head >= KA_HEAD0) & (head < VA_HEAD0)) | ((head >= KB_HEAD0) & (head < VB_HEAD0))
            kind = jnp.where(is_q, KIND_Q, jnp.where(is_k, KIND_K, KIND_V))
            t = acc[:, g * HEAD_DIM:(g + 1) * HEAD_DIM]
            swapped = jnp.take_along_axis(t, partner, axis=1)
            o_ref[local, rows, :] = t * cos_ref[kind, rows, :] + swapped * sin_ref[kind, rows, :]

    for c in range(n_chunks - 2):
        chunk(c, slice(None), PROJ_TM)
    for r in range(2):
        for c in range(n_chunks - 2, n_chunks):
            chunk(c, slice(r * (PROJ_TM // 2), (r + 1) * (PROJ_TM // 2)), PROJ_TM // 2)


def _in_proj(x2, w_in_b, layer, cos_t, sin_t):
    m = x2.shape[0]
    heads_per_tile = PROJ_TN // HEAD_DIM
    return pl.pallas_call(
        _in_proj_kernel,
        grid=(m // PROJ_TM, D_IN // PROJ_TN),
        in_specs=[pl.BlockSpec((PROJ_TM, D_MODEL), lambda i, j: (i, 0)),
                  pl.BlockSpec((None, D_MODEL, PROJ_TN), lambda i, j: (layer, 0, j)),
                  pl.BlockSpec((3, PROJ_TM, LANES), lambda i, j: (0, i, 0)),
                  pl.BlockSpec((3, PROJ_TM, LANES), lambda i, j: (0, i, 0))],
        out_specs=pl.BlockSpec((heads_per_tile, PROJ_TM, HEAD_DIM), lambda i, j: (j, i, 0)),
        out_shape=jax.ShapeDtypeStruct((N_PROJ_HEADS, m, HEAD_DIM), jnp.float32),
        scratch_shapes=[pltpu.VMEM((PROJ_TM, D_MODEL), jnp.bfloat16)],
        compiler_params=pltpu.CompilerParams(
            dimension_semantics=("arbitrary", "arbitrary"), vmem_limit_bytes=VMEM_LIMIT),
        name="in_proj",
    )(x2, w_in_b, cos_t, sin_t)


def _band_bias(delta, rows, cols, halo):
    d = delta + lax.broadcasted_iota(jnp.int32, (rows, cols), 0) - lax.broadcasted_iota(jnp.int32, (rows, cols), 1)
    return jnp.where(jnp.abs(d) <= halo, 0.0, NEG_INF).astype(jnp.float32)


def _softmax_pv(s, v, sink=None):
    m = jnp.max(s, axis=-1, keepdims=True)
    if sink is not None:
        m = jnp.maximum(m, sink)
    e = jnp.exp2(s - m)
    acc = jnp.dot(e.astype(v.dtype), v, preferred_element_type=jnp.float32)
    if v.shape[1] == HEAD_DIM:
        o, denom = acc, jnp.sum(e, axis=-1, keepdims=True)
    else:
        o, denom = acc[:, :HEAD_DIM], acc[:, HEAD_DIM:]
    if sink is not None:
        denom = denom + jnp.exp2(sink - m)
    return o, m, denom


def _store_v_ones(vo_ref, v):
    vo_ref[:, :HEAD_DIM] = v.astype(vo_ref.dtype)
    vo_ref[:, HEAD_DIM:] = jnp.ones(v.shape, vo_ref.dtype)


def _qk(q, k):
    return lax.dot_general(q, k, (((1,), (1,)), ((), ())), preferred_element_type=jnp.float32)


def _window_start(q0, halo, window, length):
    return min(max(q0 - halo, 0), length - window)


def _run_pipelined(n, scores, finish, depth=SCORE_LOOKAHEAD):
    pending = {}
    for j in range(n + depth):
        if j < n:
            pending[j] = scores(j)
        if j >= depth:
            finish(j - depth, pending.pop(j - depth))


def _cast_rows_specs(weights, grid_steps, step_of):
    rows = weights[0].shape[0] // grid_steps
    assert all(w.shape[0] == weights[0].shape[0] for w in weights) and rows % BF16_TILE_ROWS == 0
    cols = sum(w.shape[1] for w in weights)
    in_specs = [pl.BlockSpec((rows, w.shape[1]), lambda *g: (step_of(*g), 0)) for w in weights]
    out_spec = pl.BlockSpec((rows, cols), lambda *g: (step_of(*g), 0))
    out_shape = jax.ShapeDtypeStruct((weights[0].shape[0], cols), jnp.bfloat16)
    return in_specs, out_spec, out_shape


def _cast_rows(w_refs, out_ref):
    lo = 0
    for w_ref in w_refs:
        out_ref[:, lo:lo + w_ref.shape[1]] = w_ref[...].astype(out_ref.dtype)
        lo += w_ref.shape[1]


def _window_attn_kernel(q_ref, k_ref, v_ref, sink_ref, wo_ref, o_ref, wob_ref, qkv_ref, *, seq):
    _cast_rows([wo_ref], wob_ref)
    window = Q_BLOCK + 2 * A_WINDOW
    head = pl.program_id(1)
    sink = sink_ref[head] * LOG2_E
    qkv_ref[0] = q_ref[...].astype(jnp.bfloat16)

    @pl.when(head % (A_HEADS // A_KV_HEADS) == 0)
    def _():
        qkv_ref[1] = k_ref[...].astype(jnp.bfloat16)
        qkv_ref[2] = v_ref[...].astype(jnp.bfloat16)

    biases = {}

    def scores(j):
        q0 = j * Q_BLOCK
        w0 = _window_start(q0, A_WINDOW, window, seq)
        if q0 - w0 not in biases:
            biases[q0 - w0] = _band_bias(q0 - w0, Q_BLOCK, window, A_WINDOW)
        return _qk(qkv_ref[0, q0:q0 + Q_BLOCK, :], qkv_ref[1, w0:w0 + window, :]) + biases[q0 - w0]

    def finish(j, s):
        q0 = j * Q_BLOCK
        w0 = _window_start(q0, A_WINDOW, window, seq)
        o, _, denom = _softmax_pv(s, qkv_ref[2, w0:w0 + window, :], sink)
        o_ref[q0:q0 + Q_BLOCK, :] = (o / denom).astype(o_ref.dtype)

    _run_pipelined(seq // Q_BLOCK, scores, finish)


def _window_attn(proj, sink, w_o, *, batch, seq):
    m = batch * seq
    grp = A_HEADS // A_KV_HEADS
    head_spec = lambda h0, div: pl.BlockSpec((None, seq, HEAD_DIM), lambda b, h: (h0 + h // div, b, 0))
    step_of = lambda b, h: b * A_HEADS + h
    wo_in, wo_out, wo_shape = _cast_rows_specs([w_o], batch * A_HEADS, step_of)
    return pl.pallas_call(
        functools.partial(_window_attn_kernel, seq=seq),
        grid=(batch, A_HEADS),
        in_specs=[head_spec(QA_HEAD0, 1), head_spec(KA_HEAD0, grp), head_spec(VA_HEAD0, grp),
                  pl.BlockSpec(memory_space=pltpu.SMEM)] + wo_in,
        out_specs=[pl.BlockSpec((seq, HEAD_DIM), lambda b, h: (b, h)), wo_out],
        out_shape=[jax.ShapeDtypeStruct((m, A_Q), jnp.bfloat16), wo_shape],
        scratch_shapes=[pltpu.VMEM((3, seq, HEAD_DIM), jnp.bfloat16)],
        compiler_params=pltpu.CompilerParams(
            dimension_semantics=("arbitrary", "arbitrary"), vmem_limit_bytes=VMEM_LIMIT),
        name="window_attn",
    )(proj, proj, proj, sink, w_o)


def _mix2(a, b):
    m = jnp.maximum(a[1], b[1])
    fa = jnp.exp2(a[1] - m)
    fb = jnp.exp2(b[1] - m)
    return fa * a[0] + fb * b[0], m, fa * a[2] + fb * b[2]


def _dilated_attn_kernel(q_ref, k_ref, v_ref, w1_ref, wg_ref, w2_ref, o_ref, w1gb_ref, w2b_ref,
                         nat_ref, ca_ref, cab_ref, cbb_ref, vn_ref, va_ref, vb_ref, pc_ref, pn_ref, *, seq):
    _cast_rows([w1_ref, wg_ref], w1gb_ref)
    _cast_rows([w2_ref], w2b_ref)
    bf = jnp.bfloat16
    (w1, r1), (wa, ra), (wb, rb) = sorted(DILATED_PATTERNS, key=lambda pat: pat[1])
    assert r1 == 1 and rb % ra == 0
    sub = rb // ra
    len_a, len_b = seq // ra, seq // rb
    halo1, halo_a, halo_b = w1 // 2, wa // (2 * ra), wb // (2 * rb)
    biases = {}

    def band_scores(qk_src, base, q0, w0, window, halo):
        key = (q0 - w0, Q_BLOCK, window, halo)
        if key not in biases:
            biases[key] = _band_bias(*key)
        return _qk(qk_src[0, base + q0:base + q0 + Q_BLOCK, :],
                   qk_src[1, base + w0:base + w0 + window, :]) + biases[key]

    def band_output(s, v_src, base, w0, window):
        o, m, denom = _softmax_pv(s, v_src[base + w0:base + w0 + window, :])
        return o, jnp.broadcast_to(m, (Q_BLOCK, LANES)), denom

    def blocks(length, halo):
        window = min(Q_BLOCK + 2 * halo, length)
        for j in range(length // Q_BLOCK):
            q0 = j * Q_BLOCK
            yield q0, _window_start(q0, halo, window, length), window

    def load3(ref, rows):
        return tuple(ref[t, rows, :] for t in range(3))

    def store3(ref, rows, triple):
        for t in range(3):
            ref[t, rows, :] = triple[t]

    for t, src in enumerate((q_ref, k_ref, v_ref)):
        if t < 2:
            nat_ref[t] = src[...].astype(bf)
        else:
            _store_v_ones(vn_ref, src[...])
        for c in range(ra):
            x = src[pl.ds(c, len_a, stride=ra), :]
            ca_ref[t, c * len_a:(c + 1) * len_a, :] = x
            if t < 2:
                cab_ref[t, c * len_a:(c + 1) * len_a, :] = x.astype(bf)
            else:
                _store_v_ones(va_ref.at[c * len_a:(c + 1) * len_a, :], x)
    for t in range(3):
        for c in range(ra):
            for u in range(sub):
                rows = slice((c * sub + u) * len_b, (c * sub + u + 1) * len_b)
                x = ca_ref[t, pl.ds(c * len_a + u, len_b, stride=sub), :]
                if t < 2:
                    cbb_ref[t, rows, :] = x.astype(bf)
                else:
                    _store_v_ones(vb_ref.at[rows, :], x)

    units = []

    def add_unit(qk_src, v_src, base, q0, w0, window, halo, finish):
        units.append((lambda: band_scores(qk_src, base, q0, w0, window, halo),
                      lambda s: finish(band_output(s, v_src, base, w0, window))))

    for c in range(ra):
        for u in range(sub):
            cu = c * sub + u
            for q0, w0, window in blocks(len_b, halo_b):
                def finish_b(triple, c=c, u=u, q0=q0):
                    store3(pc_ref, pl.ds(c * len_a + q0 * sub + u, Q_BLOCK, stride=sub), triple)
                add_unit(cbb_ref, vb_ref, cu * len_b, q0, w0, window, halo_b, finish_b)

    for c in range(ra):
        for q0, w0, window in blocks(len_a, halo_a):
            def finish_a(triple, c=c, q0=q0):
                merged = _mix2(triple, load3(pc_ref, slice(c * len_a + q0, c * len_a + q0 + Q_BLOCK)))
                store3(pn_ref, pl.ds(q0 * ra + c, Q_BLOCK, stride=ra), merged)
            add_unit(cab_ref, va_ref, c * len_a, q0, w0, window, halo_a, finish_a)

    for q0, w0, window in blocks(seq, halo1):
        def finish_1(triple, q0=q0):
            o, _, denom = _mix2(triple, load3(pn_ref, slice(q0, q0 + Q_BLOCK)))
            o_ref[q0:q0 + Q_BLOCK, :] = (o / denom).astype(o_ref.dtype)
        add_unit(nat_ref, vn_ref, 0, q0, w0, window, halo1, finish_1)

    _run_pipelined(len(units), lambda j: units[j][0](), lambda j, s: units[j][1](s))


def _dilated_attn(proj, w1, w_gate, w2, *, batch, seq):
    m = batch * seq
    head_spec = lambda h0: pl.BlockSpec((None, seq, HEAD_DIM), lambda b, h: (h0 + h, b, 0))
    step_of = lambda b, h: b * B_HEADS + h
    w_in_specs, w_out_spec, w_out_shape = _cast_rows_specs([w1, w_gate], batch * B_HEADS, step_of)
    w2_in_specs, w2_out_spec, w2_out_shape = _cast_rows_specs([w2], batch * B_HEADS, step_of)
    qk_buf = pltpu.VMEM((2, seq, HEAD_DIM), jnp.bfloat16)
    v_ones_buf = pltpu.VMEM((seq, 2 * HEAD_DIM), jnp.bfloat16)
    return pl.pallas_call(
        functools.partial(_dilated_attn_kernel, seq=seq),
        grid=(batch, B_HEADS),
        in_specs=[head_spec(QB_HEAD0), head_spec(KB_HEAD0), head_spec(VB_HEAD0)] + w_in_specs + w2_in_specs,
        out_specs=[pl.BlockSpec((seq, HEAD_DIM), lambda b, h: (b, h)), w_out_spec, w2_out_spec],
        out_shape=[jax.ShapeDtypeStruct((m, B_QKV), jnp.bfloat16), w_out_shape, w2_out_shape],
        scratch_shapes=[qk_buf,
                        pltpu.VMEM((3, seq, HEAD_DIM), jnp.float32),
                        qk_buf, qk_buf,
                        v_ones_buf, v_ones_buf, v_ones_buf]
                       + [pltpu.VMEM((3, seq, HEAD_DIM), jnp.float32)] * 2,
        compiler_params=pltpu.CompilerParams(
            dimension_semantics=("arbitrary", "arbitrary"), vmem_limit_bytes=VMEM_LIMIT),
        name="dilated_attn",
    )(proj, proj, proj, w1, w_gate, w2)


def _layer_norm(y, g, b):
    mu = jnp.mean(y, axis=-1, keepdims=True)
    d = y - mu
    var = jnp.mean(d * d, axis=-1, keepdims=True)
    return d * lax.rsqrt(var + LN_EPS) * g + b


def _rms_norm(y, g):
    return y * lax.rsqrt(jnp.mean(y * y, axis=-1, keepdims=True) + RMS_EPS) * g


def _mix_ln1_kernel(oa_ref, ob_ref, x_ref, wo_ref, gna_ref, gnb_ref, g_ref, b_ref, h_ref):
    for r in range(MIX_TM // MIX_ROWS):
        rows = slice(r * MIX_ROWS, (r + 1) * MIX_ROWS)
        ya = _rms_norm(oa_ref[rows, :].astype(jnp.float32), gna_ref[...])
        yb = _rms_norm(ob_ref[rows, :].astype(jnp.float32), gnb_ref[...])
        mix = (jnp.dot(ya.astype(jnp.bfloat16), wo_ref[:A_Q, :], preferred_element_type=jnp.float32)
               + jnp.dot(yb.astype(jnp.bfloat16), wo_ref[A_Q:, :], preferred_element_type=jnp.float32))
        h_ref[rows, :] = _layer_norm(ALPHA * x_ref[rows, :] + mix, g_ref[...], b_ref[...])


def _mix_ln1(oa, ob, x2, wo_b, gn_a, gn_b, ln_g, ln_b):
    m = x2.shape[0]
    row = lambda w: pl.BlockSpec((MIX_TM, w), lambda i: (i, 0))
    full = lambda r, w: pl.BlockSpec((r, w), lambda i: (0, 0), pipeline_mode=pl.Buffered(1))
    return pl.pallas_call(
        _mix_ln1_kernel,
        grid=(m // MIX_TM,),
        in_specs=[row(A_Q), row(B_QKV), row(D_MODEL), full(D_MODEL, D_MODEL), full(1, A_Q),
                  full(1, B_QKV), full(1, D_MODEL), full(1, D_MODEL)],
        out_specs=row(D_MODEL),
        out_shape=jax.ShapeDtypeStruct((m, D_MODEL), jnp.float32),
        compiler_params=pltpu.CompilerParams(
            dimension_semantics=("arbitrary",), vmem_limit_bytes=VMEM_LIMIT),
        name="mix_ln1",
    )(oa, ob, x2, wo_b, gn_a, gn_b, ln_g, ln_b)


def _ffn_ln2_kernel(h_ref, w1g_ref, w2_ref, p_ref, wple_ref, g_ref, b_ref, o_ref, hb_ref, gate_ref):
    j = pl.program_id(1)
    n_ff = D_FF // FFN_TF
    n_all = (D_FF + D_MODEL) // FFN_TF
    n_sub = FFN_TF // FFN_SUB

    @pl.when(j == 0)
    def _():
        h = h_ref[...]
        hb_ref[...] = h.astype(jnp.bfloat16)
        o_ref[...] = ALPHA * h

    def up(c):
        return jnp.dot(hb_ref[...], w1g_ref[:, c * FFN_SUB:(c + 1) * FFN_SUB],
                       preferred_element_type=jnp.float32)

    @pl.when(j < n_ff)
    def _():
        acts = []
        for c in range(n_sub):
            r = jnp.maximum(up(c), 0.0)
            acts.append((r * r).astype(jnp.bfloat16))
        for n in range(D_MODEL // FFN_SUB):
            cols = slice(n * FFN_SUB, (n + 1) * FFN_SUB)
            part = sum(jnp.dot(acts[c], w2_ref[c * FFN_SUB:(c + 1) * FFN_SUB, cols],
                               preferred_element_type=jnp.float32) for c in range(n_sub))
            o_ref[:, cols] += part

    def gate_chunk(t, c, rows):
        lo = t * FFN_TF + c * FFN_SUB
        a = jnp.dot(hb_ref[rows, :], w1g_ref[:, c * FFN_SUB:(c + 1) * FFN_SUB], preferred_element_type=jnp.float32)
        gate_ref[rows, lo:lo + FFN_SUB] = 0.5 * jnp.tanh(0.5 * a) + 0.5

    for t in range(n_all - n_ff - 1):
        @pl.when(j == n_ff + t)
        def _():
            for c in range(n_sub):
                gate_chunk(t, c, slice(None))

    @pl.when(j == n_all - 1)
    def _():
        for r in range(FFN_TM // FFN_LAST_ROWS):
            rows = slice(r * FFN_LAST_ROWS, (r + 1) * FFN_LAST_ROWS)
            for c in range(n_sub):
                gate_chunk(n_all - n_ff - 1, c, rows)
            ple = jnp.dot(p_ref[rows, :].astype(jnp.bfloat16), wple_ref[...], preferred_element_type=jnp.float32)
            o_ref[rows, :] = _layer_norm(o_ref[rows, :] + ple * gate_ref[rows, :], g_ref[...], b_ref[...])


def _ffn_ln2(h1, w1g_b, w2_b, p_flat, layer, wple_b, ln_g, ln_b):
    m = h1.shape[0]
    p_tile0 = layer * (m // FFN_TM)
    n_ff = D_FF // FFN_TF
    n_all = (D_FF + D_MODEL) // FFN_TF
    return pl.pallas_call(
        _ffn_ln2_kernel,
        grid=(m // FFN_TM, n_all),
        in_specs=[pl.BlockSpec((FFN_TM, D_MODEL), lambda i, j: (i, 0)),
                  pl.BlockSpec((D_MODEL, FFN_TF), lambda i, j: (0, j)),
                  pl.BlockSpec((FFN_TF, D_MODEL), lambda i, j: (jnp.minimum(j, n_ff - 1), 0)),
                  pl.BlockSpec((FFN_TM, PLE_DIM), lambda i, j: (p_tile0 + i, 0)),
                  pl.BlockSpec((PLE_DIM, D_MODEL), lambda i, j: (0, 0), pipeline_mode=pl.Buffered(1)),
                  pl.BlockSpec((1, D_MODEL), lambda i, j: (0, 0), pipeline_mode=pl.Buffered(1)),
                  pl.BlockSpec((1, D_MODEL), lambda i, j: (0, 0), pipeline_mode=pl.Buffered(1))],
        out_specs=pl.BlockSpec((FFN_TM, D_MODEL), lambda i, j: (i, 0)),
        out_shape=jax.ShapeDtypeStruct((m, D_MODEL), jnp.float32),
        scratch_shapes=[pltpu.VMEM((FFN_TM, D_MODEL), jnp.bfloat16),
                        pltpu.VMEM((FFN_TM, D_MODEL), jnp.float32)],
        compiler_params=pltpu.CompilerParams(
            dimension_semantics=("arbitrary", "arbitrary"), vmem_limit_bytes=VMEM_LIMIT),
        name="ffn_ln2",
    )(h1, w1g_b, w2_b, p_flat, wple_b, ln_g, ln_b)


def kernel(x, p, positions, w_in, sink_a, gn_a, gn_b, w_o, ln1_g, ln1_b, w1, w2, w_ple, w_ple_gate,
           ln2_g, ln2_b):
    batch, seq, d = x.shape
    m = batch * seq
    bf = jnp.bfloat16
    cos_t, sin_t, w_in_b = _rope_tables(positions, w_in)
    p_flat = p.reshape(-1, PLE_DIM)
    h = x.reshape(m, d)
    for i in range(DEPTH):
        proj = _in_proj(h, w_in_b, i, cos_t, sin_t)
        oa, wo_b = _window_attn(proj, sink_a[i], w_o[i], batch=batch, seq=seq)
        ob, w1g_b, w2_b = _dilated_attn(proj, w1[i], w_ple_gate[i], w2[i], batch=batch, seq=seq)
        h1 = _mix_ln1(oa, ob, h, wo_b, gn_a[i].reshape(1, -1), gn_b[i].reshape(1, -1),
                      ln1_g[i].reshape(1, -1), ln1_b[i].reshape(1, -1))
        h = _ffn_ln2(h1, w1g_b, w2_b, p_flat, i, w_ple[i].astype(bf),
                     ln2_g[i].reshape(1, -1), ln2_b[i].reshape(1, -1))
    return h.reshape(batch, seq, d)
```
